```python
import jax
import jax.numpy as jnp
from jax import lax
import numpy as np

D_MODEL = 1024
BATCH = 32
SEQ = 256
DEPTH = 2
DEC_BATCH = 2
DEC_SEQ = 2048
PAST_LEN = 256

GRID_W = 64
Q_BLOCK = 128
ROPE_THETA = 10000.0
EPS = 1e-6
GATE_FLOOR = 1e-30
HG_HEADS = 4
HG_DK = 64
HG_DV = 64
HG_CHUNK = 16
MLA_HEADS = 6
MLA_NOPE = 64
MLA_ROPE = 32
MLA_V = 64
MLA_Q_RANK = 256
MLA_KV_RANK = 128
GQA_HEADS = 6
GQA_KV_HEADS = 2
GQA_GROUP = GQA_HEADS // GQA_KV_HEADS
GQA_HD = 64
HG_W = HG_HEADS * HG_DV
MLA_W = MLA_HEADS * MLA_V
GQA_W = GQA_HEADS * GQA_HD
MIX_W = HG_W + MLA_W + GQA_W
D_FF = -(-8 * D_MODEL // (3 * 256)) * 256
MLA_SCALE = (MLA_NOPE + MLA_ROPE) ** -0.5
GQA_SCALE = GQA_HD ** -0.5
ALPHA = (2 * DEPTH) ** 0.25
BETA = (8 * DEPTH) ** -0.25
IN_SIZES = (HG_HEADS * HG_DK, HG_HEADS * HG_DK, HG_HEADS * HG_DK, HG_HEADS * HG_DV, HG_HEADS * HG_DV,
            MLA_Q_RANK, MLA_KV_RANK, MLA_ROPE,
            GQA_HEADS * GQA_HD, GQA_KV_HEADS * GQA_HD, GQA_KV_HEADS * GQA_HD)
IN_OFFSETS = tuple(sum(IN_SIZES[:i + 1]) for i in range(len(IN_SIZES) - 1))
N_IN = sum(IN_SIZES)

kernel_name = "hymba_dit_prefix_denoise_step"

F32 = jnp.float32


def _rms(x, g):
    xf = x.astype(F32)
    y = xf * lax.rsqrt(jnp.mean(xf * xf, axis=-1, keepdims=True) + EPS)
    return (y * g.astype(F32)).astype(x.dtype)


def _layernorm(x, g, b):
    xf = x.astype(F32)
    xc = xf - jnp.mean(xf, axis=-1, keepdims=True)
    var = jnp.mean(xc * xc, axis=-1, keepdims=True)
    return (xc * lax.rsqrt(var + EPS) * g.astype(F32) + b.astype(F32)).astype(x.dtype)


def _grid_positions(n):
    rows = n // GRID_W
    row = jnp.repeat(jnp.arange(rows, dtype=F32), GRID_W)
    col = jnp.tile(jnp.arange(GRID_W, dtype=F32), rows)
    return row, col


def _rope_1d(x, pos):
    half = x.shape[-1] // 2
    freqs = ROPE_THETA ** (-jnp.arange(half, dtype=F32) / half)
    ang = pos[:, None] * freqs[None, :]
    cos = jnp.cos(ang)[None, :, None, :]
    sin = jnp.sin(ang)[None, :, None, :]
    xf = x.astype(F32)
    x1, x2 = xf[..., :half], xf[..., half:]
    return jnp.concatenate([x1 * cos - x2 * sin, x1 * sin + x2 * cos], axis=-1).astype(x.dtype)


def _rope_2d(x, row, col):
    r = x.shape[-1] // 2
    return jnp.concatenate([_rope_1d(x[..., :r], row), _rope_1d(x[..., r:], col)], axis=-1)


def _attention(q, k, v, scale):
    b, sq, kh, g, dk = q.shape
    nb = sq // Q_BLOCK
    qb = q.reshape(b, nb, Q_BLOCK, kh, g, dk).swapaxes(0, 1)

    def one_block(qi):
        s = jnp.einsum('bqhgd,bkhd->bhgqk', qi, k, preferred_element_type=F32) * scale
        p = jax.nn.softmax(s, axis=-1)
        return jnp.einsum('bhgqk,bkhe->bqhge', p.astype(v.dtype), v)

    out = lax.map(one_block, qb)
    return out.swapaxes(0, 1).reshape(b, sq, kh, g, v.shape[-1])


def _hgrn_scan(q, logf, k, i, s0):
    b, s, h, dk = q.shape
    dv = i.shape[-1]
    n = s // HG_CHUNK

    def chunk(t):
        return t.reshape(b, n, HG_CHUNK, h, t.shape[-1]).transpose(1, 0, 3, 2, 4)

    qc, lc, kc, ic = chunk(q), chunk(logf), chunk(k), chunk(i)
    bc = jnp.cumsum(lc, axis=3)
    causal = jnp.tril(jnp.ones((HG_CHUNK, HG_CHUNK), dtype=bool))[:, :, None]
    diff = bc[..., :, None, :] - bc[..., None, :, :]
    decay = jnp.where(causal, jnp.exp(jnp.where(causal, diff, 0.0)), 0.0)
    a = jnp.einsum('nbhtk,nbhtsk,nbhsk->nbhts', qc, decay, kc)
    intra = jnp.einsum('nbhts,nbhsv->nbhtv', a, ic)
    q_dec = qc * jnp.exp(bc)
    k_dec = kc * jnp.exp(bc[..., -1:, :] - bc)
    chunk_decay = jnp.exp(bc[..., -1, :])

    def step(S, xs):
        qd, kd, iv, cd = xs
        inter = jnp.einsum('bhtk,bhkv->bhtv', qd, S)
        S = S * cd[..., None] + jnp.einsum('bhsk,bhsv->bhkv', kd, iv)
        return S, inter

    s_final, inter = lax.scan(step, s0, (q_dec, k_dec, ic, chunk_decay))
    o = (intra + inter).transpose(1, 0, 3, 2, 4).reshape(b, s, h, dv)
    return o, s_final


def _hgrn_mixer(hq, hff, hfb, hi, hg, lb, s0f, s0b, norm_g):
    b, s, _ = hq.shape
    q = jax.nn.silu(hq.astype(F32)).reshape(b, s, HG_HEADS, HG_DK) * (HG_DK ** -0.5)
    i = hi.astype(F32).reshape(b, s, HG_HEADS, HG_DV)

    def gates(z, lbd):
        z = z.astype(F32).reshape(b, s, HG_HEADS, HG_DK)
        lbd = lbd.reshape(HG_HEADS, HG_DK)
        f = lbd + (1.0 - lbd) * jax.nn.sigmoid(z)
        return jnp.log(jnp.maximum(f, GATE_FLOOR)), (1.0 - lbd) * jax.nn.sigmoid(-z)

    lf, kf = gates(hff, lb[0])
    lbw, kb = gates(hfb, lb[1])
    o_f, s_f = _hgrn_scan(q, lf, kf, i, s0f)
    flip = lambda t: jnp.flip(t, axis=1)
    o_b, s_b = _hgrn_scan(flip(q), flip(lbw), flip(kb), flip(i), s0b)
    o = _rms(o_f + flip(o_b), norm_g) * jax.nn.silu(hg.astype(F32).reshape(b, s, HG_HEADS, HG_DV))
    return o.reshape(b, s, HG_W).astype(hq.dtype), s_f, s_b


def _mla_queries(mq, q_norm, w_uq):
    b, s, _ = mq.shape
    q = jnp.einsum('bsr,rn->bsn', _rms(mq, q_norm), w_uq).reshape(b, s, MLA_HEADS, MLA_NOPE + MLA_ROPE)
    return q[..., :MLA_NOPE], q[..., MLA_NOPE:]


def _mla_keys_values(ckv, kpe, w_ukv):
    b, s, _ = ckv.shape
    kv = jnp.einsum('bsr,rn->bsn', ckv, w_ukv).reshape(b, s, MLA_HEADS, MLA_NOPE + MLA_V)
    kpe_h = jnp.broadcast_to(kpe[:, :, None, :], (b, s, MLA_HEADS, MLA_ROPE))
    return jnp.concatenate([kv[..., :MLA_NOPE], kpe_h], axis=-1), kv[..., MLA_NOPE:]


def _mixer(h, P, l, lb, cache):
    b, s, _ = h.shape
    z = jnp.einsum('bsd,dn->bsn', h, P['w_in'][l])
    hq, hff, hfb, hi, hg, mq, mkv, mkr, gq, gk, gv = jnp.split(z, IN_OFFSETS, axis=-1)
    is_ctx = cache is None
    if is_ctx:
        s0f = jnp.zeros((b, HG_HEADS, HG_DK, HG_DV), F32)
        s0b = s0f
    else:
        ckv_c, kpe_c, gk_c, gv_c, st_c = cache
        s0f, s0b = st_c[:, 0].astype(F32), st_c[:, 1].astype(F32)
    hg_out, s_f, s_b = _hgrn_mixer(hq, hff, hfb, hi, hg, lb, s0f, s0b, P['hg_norm'][l])

    q_nope, q_pe = _mla_queries(mq, P['mla_q_norm'][l], P['mla_w_uq'][l])
    ckv = _rms(mkv, P['mla_kv_norm'][l])
    kpe = mkr
    gq = _rms(gq.reshape(b, s, GQA_HEADS, GQA_HD), P['gqa_q_norm'][l])
    gk = _rms(gk.reshape(b, s, GQA_KV_HEADS, GQA_HD), P['gqa_k_norm'][l])
    gv = gv.reshape(b, s, GQA_KV_HEADS, GQA_HD)
    w_ukv = P['mla_w_ukv'][l]
    if is_ctx:
        mk, mv = _mla_keys_values(ckv, kpe, w_ukv)
        ak, av = gk, gv
        new = (ckv, kpe, gk, gv, jnp.stack([s_f, s_b], axis=1).astype(h.dtype))
    else:
        row, col = _grid_positions(s)
        q_pe = _rope_2d(q_pe, row, col)
        kpe_rot = _rope_2d(kpe[:, :, None, :], row, col)[:, :, 0, :]
        mk_l, mv_l = _mla_keys_values(ckv, kpe_rot, w_ukv)
        mk_c, mv_c = _mla_keys_values(ckv_c, kpe_c, w_ukv)
        mk = jnp.concatenate([mk_c, mk_l], axis=1)
        mv = jnp.concatenate([mv_c, mv_l], axis=1)
        gq = _rope_2d(gq, row, col)
        ak = jnp.concatenate([gk_c, _rope_2d(gk, row, col)], axis=1)
        av = jnp.concatenate([gv_c, gv], axis=1)
        new = None
    mla_q = jnp.concatenate([q_nope, q_pe], axis=-1)[:, :, :, None, :]
    mla_out = _attention(mla_q, mk, mv, MLA_SCALE).reshape(b, s, MLA_W)
    gqa_out = _attention(gq.reshape(b, s, GQA_KV_HEADS, GQA_GROUP, GQA_HD), ak, av, GQA_SCALE).reshape(b, s, GQA_W)
    mixed = jnp.concatenate([hg_out, mla_out, gqa_out], axis=-1)
    return jnp.einsum('bsn,nd->bsd', mixed, P['w_out'][l]), new


def _layer(x, cvec, P, l, lb, cache):
    m = jnp.einsum('bd,dn->bn', jax.nn.silu(cvec), P['w_ada'][l]) + P['b_ada'][l]
    sh1, sc1, g1, sh2, sc2, g2 = jnp.split(m[:, None, :], 6, axis=-1)
    y, new = _mixer(x * (1.0 + sc1) + sh1, P, l, lb, cache)
    x = _layernorm(ALPHA * x + g1 * y, P['ln1_g'][l], P['ln1_b'][l])
    hf = x * (1.0 + sc2) + sh2
    gate, up = jnp.split(jnp.einsum('bsd,dn->bsn', hf, P['w_ffn_in'][l]), 2, axis=-1)
    f = jnp.einsum('bsn,nd->bsd', jax.nn.silu(gate) * up, P['w_ffn_out'][l])
    x = _layernorm(ALPHA * x + g2 * f, P['ln2_g'][l], P['ln2_b'][l])
    return x, new


def setup_inputs(seed: int = 0) -> dict:
    key = jax.random.key(seed)
    ks = jax.random.split(key, 32)
    nrm = lambda k, shape, sc: jax.random.normal(k, shape, jnp.float32) * sc
    gain = lambda k, shape: 1.0 + nrm(k, shape, 0.02)
    D = D_MODEL
    return {
        'x_prompt': nrm(ks[0], (BATCH, SEQ, D), 1.0),
        'x_sample': nrm(ks[1], (DEC_BATCH, DEC_SEQ, D), 1.0),
        'cache_mla_ckv': nrm(ks[2], (DEC_BATCH, DEPTH, PAST_LEN, MLA_KV_RANK), 1.0),
        'cache_mla_kpe': nrm(ks[3], (DEC_BATCH, DEPTH, PAST_LEN, MLA_ROPE), 1.0),
        'cache_gqa_k': nrm(ks[4], (DEC_BATCH, DEPTH, PAST_LEN, GQA_KV_HEADS, GQA_HD), 1.0),
        'cache_gqa_v': nrm(ks[5], (DEC_BATCH, DEPTH, PAST_LEN, GQA_KV_HEADS, GQA_HD), 1.0),
        'state_hgrn': nrm(ks[6], (DEC_BATCH, DEPTH, 2, HG_HEADS, HG_DK, HG_DV), 0.5),
        'c': nrm(ks[7], (DEC_BATCH, D), 1.0),
        'c_ctx': nrm(ks[8], (D,), 1.0),
        'w_ada': nrm(ks[9], (DEPTH, D, 6 * D), 0.5 * D ** -0.5),
        'b_ada': nrm(ks[10], (DEPTH, 6 * D), 0.02),
        'w_in': nrm(ks[11], (DEPTH, D, N_IN), D ** -0.5),
        'hg_lb': nrm(ks[12], (DEPTH, 2, HG_HEADS * HG_DK), 0.5),
        'hg_norm': gain(ks[13], (DEPTH, HG_DV)),
        'mla_q_norm': gain(ks[14], (DEPTH, MLA_Q_RANK)),
        'mla_w_uq': nrm(ks[15], (DEPTH, MLA_Q_RANK, MLA_HEADS * (MLA_NOPE + MLA_ROPE)), MLA_Q_RANK ** -0.5),
        'mla_kv_norm': gain(ks[16], (DEPTH, MLA_KV_RANK)),
        'mla_w_ukv': nrm(ks[17], (DEPTH, MLA_KV_RANK, MLA_HEADS * (MLA_NOPE + MLA_V)), MLA_KV_RANK ** -0.5),
        'gqa_q_norm': gain(ks[18], (DEPTH, GQA_HD)),
        'gqa_k_norm': gain(ks[19], (DEPTH, GQA_HD)),
        'w_out': nrm(ks[20], (DEPTH, MIX_W, D), BETA * MIX_W ** -0.5),
        'ln1_g': gain(ks[21], (DEPTH, D)),
        'ln1_b': nrm(ks[22], (DEPTH, D), 0.02),
        'w_ffn_in': nrm(ks[23], (DEPTH, D, 2 * D_FF), D ** -0.5),
        'w_ffn_out': nrm(ks[24], (DEPTH, D_FF, D), BETA * D_FF ** -0.5),
        'ln2_g': gain(ks[25], (DEPTH, D)),
        'ln2_b': nrm(ks[26], (DEPTH, D), 0.02),
    }


def reference(x_prompt, x_sample, cache_mla_ckv, cache_mla_kpe, cache_gqa_k, cache_gqa_v, state_hgrn, c, c_ctx,
              w_ada, b_ada, w_in, hg_lb, hg_norm, mla_q_norm, mla_w_uq, mla_kv_norm, mla_w_ukv,
              gqa_q_norm, gqa_k_norm, w_out, ln1_g, ln1_b, w_ffn_in, w_ffn_out, ln2_g, ln2_b):
    P = {'w_ada': w_ada, 'b_ada': b_ada, 'w_in': w_in, 'hg_norm': hg_norm,
         'mla_q_norm': mla_q_norm, 'mla_w_uq': mla_w_uq, 'mla_kv_norm': mla_kv_norm, 'mla_w_ukv': mla_w_ukv,
         'gqa_q_norm': gqa_q_norm, 'gqa_k_norm': gqa_k_norm, 'w_out': w_out,
         'ln1_g': ln1_g, 'ln1_b': ln1_b, 'w_ffn_in': w_ffn_in, 'w_ffn_out': w_ffn_out,
         'ln2_g': ln2_g, 'ln2_b': ln2_b}
    sm = jax.nn.softmax(hg_lb.astype(F32), axis=0)
    lbs = jnp.cumsum(sm, axis=0) - sm[0:1]

    x = x_prompt
    news = []
    for l in range(DEPTH):
        x, new = _layer(x, c_ctx[None, :], P, l, lbs[l], None)
        news.append(new)
    y_prompt = x
    state_mla_ckv = jnp.stack([n[0] for n in news], axis=1)
    state_mla_kpe = jnp.stack([n[1] for n in news], axis=1)
    state_gqa_k = jnp.stack([n[2] for n in news], axis=1)
    state_gqa_v = jnp.stack([n[3] for n in news], axis=1)
    new_state_hgrn = jnp.stack([n[4] for n in news], axis=1)

    x = x_sample
    for l in range(DEPTH):
        cache = (cache_mla_ckv[:, l], cache_mla_kpe[:, l], cache_gqa_k[:, l], cache_gqa_v[:, l], state_hgrn[:, l])
        x, _ = _layer(x, c, P, l, lbs[l], cache)
    y_sample = x
    return (y_prompt, y_sample, state_mla_ckv, state_mla_kpe, state_gqa_k, state_gqa_v, new_state_hgrn)
```

```python
import functools

import numpy as np
import jax
import jax.numpy as jnp
from jax import lax
from jax.experimental import pallas as pl
from jax.experimental.pallas import tpu as pltpu

F32 = jnp.float32
BF16 = jnp.bfloat16

D_MODEL = 1024
BATCH = 32
SEQ = 256
DEPTH = 2
DEC_BATCH = 2
DEC_SEQ = 2048
PAST_LEN = 256
GRID_W = 64
ROPE_THETA = 10000.0
EPS = 1e-6
GATE_FLOOR = 1e-30
HG_HEADS = 4
HG_DK = 64
HG_DV = 64
MLA_HEADS = 6
MLA_NOPE = 64
MLA_ROPE = 32
MLA_V = 64
MLA_Q_RANK = 256
MLA_KV_RANK = 128
GQA_HEADS = 6
GQA_KV_HEADS = 2
GQA_HD = 64
HG_W = HG_HEADS * HG_DV
MLA_W = MLA_HEADS * MLA_V
GQA_W = GQA_HEADS * GQA_HD
D_FF = 2816
MLA_SCALE = (MLA_NOPE + MLA_ROPE) ** -0.5
GQA_SCALE = GQA_HD ** -0.5
ALPHA = (2 * DEPTH) ** 0.25

LANES = 128
SUBLANES = 8
HEAD_PAD = 128
N_ADA = 6 * D_MODEL
MOD_ROWS = 8

HG_COLS = 5 * HG_HEADS * HG_DK
OFF_MQ = HG_COLS
OFF_MKV = OFF_MQ + MLA_Q_RANK
OFF_GQ = OFF_MKV + MLA_KV_RANK
OFF_GK = OFF_GQ + GQA_W
OFF_GV = OFF_GK + GQA_KV_HEADS * GQA_HD
OFF_KPE_PAD = OFF_GV + GQA_KV_HEADS * GQA_HD
OFF_KPE = OFF_KPE_PAD + HEAD_PAD
N_Z = OFF_KPE + MLA_ROPE

TM_IN = 512
TM_FFN = 512
TQ_DEC = 256
HG_CHUNK = 128
FF_CHUNKS = 2
VMEM_LIMIT = 56 * 1024 * 1024


def _cparams(sem):
    return pltpu.CompilerParams(dimension_semantics=sem, vmem_limit_bytes=VMEM_LIMIT)


def _const_spec(shape):
    nd = len(shape)
    return pl.BlockSpec(shape, lambda *_: (0,) * nd)


def _silu(x):
    return x * jax.nn.sigmoid(x)


def _dot(a, b):
    return jnp.dot(a, b, preferred_element_type=F32)


def _dot_nt(a, b):
    return lax.dot_general(a, b, (((1,), (1,)), ((), ())), preferred_element_type=F32)


def _dot_tn(a, b):
    return lax.dot_general(a, b, (((0,), (0,)), ((), ())), preferred_element_type=F32)


def _rms_rows(x, g):
    y = x * lax.rsqrt(jnp.mean(x * x, axis=-1, keepdims=True) + EPS)
    return y * g


def _layernorm_rows(x, g, b):
    xc = x - jnp.mean(x, axis=-1, keepdims=True)
    var = jnp.mean(xc * xc, axis=-1, keepdims=True)
    return xc * lax.rsqrt(var + EPS) * g + b


def _group_mean(x, ones_blk):
    return _dot(x.astype(BF16), ones_blk)


def _ada_kernel(c_ref, w_ref, b_ref, o_ref):
    s = _silu(c_ref[...]).astype(BF16)
    o_ref[0] = _dot(s, w_ref[0].astype(BF16)) + b_ref[0]


def _ada_call(cvec, w_ada, b_ada):
    tn = 1536
    return pl.pallas_call(
        _ada_kernel,
        grid=(DEPTH, N_ADA // tn),
        in_specs=[
            pl.BlockSpec((MOD_ROWS, D_MODEL), lambda l, j: (0, 0)),
            pl.BlockSpec((1, D_MODEL, tn), lambda l, j: (l, 0, j)),
            pl.BlockSpec((1, 1, tn), lambda l, j: (l, 0, j)),
        ],
        out_specs=pl.BlockSpec((1, MOD_ROWS, tn), lambda l, j: (l, 0, j)),
        out_shape=jax.ShapeDtypeStruct((DEPTH, MOD_ROWS, N_ADA), F32),
        compiler_params=_cparams(("arbitrary", "arbitrary")),
        name="ada_mod",
    )(cvec, w_ada, b_ada.reshape(DEPTH, 1, N_ADA))


def _rope_block(x, c, sa, sb, half):
    return x * c + pltpu.roll(x, LANES - half, 1) * sa + pltpu.roll(x, half, 1) * sb


def _inproj_kernel(rope, x_ref, mod_ref, w_in_ref, w_uq_ref, w_uk_ref, w_uv_ref, qn_ref, kvn_ref, gqn_ref,
                   gkn_ref, ones_ref, *rest):
    if rope:
        (mc_ref, msa_ref, msb_ref, gc_ref, gsa_ref, gsb_ref,
         zh_ref, qm_ref, km_ref, vm_ref, gq_ref, gk_ref, gv_ref) = rest
    else:
        (zh_ref, qm_ref, km_ref, vm_ref, gq_ref, gk_ref, gv_ref,
         ckv_ref, kpe_ref, gk32_ref, gv32_ref) = rest
    mod = mod_ref[0]
    sh1 = mod[:, 0:D_MODEL]
    sc1 = mod[:, D_MODEL:2 * D_MODEL]
    h = (x_ref[...] * (1.0 + sc1) + sh1).astype(BF16)
    z = _dot(h, w_in_ref[...])
    zh_ref[...] = z[:, 0:HG_COLS]
    mq = z[:, OFF_MQ:OFF_MKV]
    mkv = z[:, OFF_MKV:OFF_GQ]
    gq = z[:, OFF_GQ:OFF_GK]
    gk = z[:, OFF_GK:OFF_GV]
    gv = z[:, OFF_GV:OFF_KPE_PAD]
    kpe_pad = z[:, OFF_KPE_PAD:OFF_KPE]
    ones = ones_ref[...]

    q = _dot(_rms_rows(mq, qn_ref[...]).astype(BF16), w_uq_ref[...])
    ckv = _rms_rows(mkv, kvn_ref[...])
    ckv_b = ckv.astype(BF16)
    kn = _dot(ckv_b, w_uk_ref[...])
    vm_ref[...] = _dot(ckv_b, w_uv_ref[...]).astype(BF16)
    if rope:
        mc, msa, msb = mc_ref[...], msa_ref[...], msb_ref[...]
        kpe_pad = _rope_block(kpe_pad, mc, msa, msb, MLA_ROPE // 4)
    for hd in range(MLA_HEADS):
        sl = slice(hd * HEAD_PAD, (hd + 1) * HEAD_PAD)
        qh = q[:, sl]
        if rope:
            qh = _rope_block(qh, mc, msa, msb, MLA_ROPE // 4)
        qm_ref[:, sl] = (qh * MLA_SCALE).astype(BF16)
        km_ref[:, sl] = (kn[:, sl] + kpe_pad).astype(BF16)

    gkn = gk * lax.rsqrt(_group_mean(gk * gk, ones) + EPS) * gkn_ref[...]
    for p in range(GQA_W // LANES):
        sl = slice(p * LANES, (p + 1) * LANES)
        gqp = gq[:, sl]
        gqp = gqp * lax.rsqrt(_group_mean(gqp * gqp, ones) + EPS) * gqn_ref[...]
        if rope:
            gqp = _rope_block(gqp, gc_ref[...], gsa_ref[...], gsb_ref[...], GQA_HD // 4)
        gq_ref[:, sl] = (gqp * GQA_SCALE).astype(BF16)
    if rope:
        gk_ref[...] = _rope_block(gkn, gc_ref[...], gsa_ref[...], gsb_ref[...], GQA_HD // 4).astype(BF16)
    else:
        gk_ref[...] = gkn.astype(BF16)
        ckv_ref[...] = ckv
        kpe_ref[...] = z[:, OFF_KPE:N_Z]
        gk32_ref[...] = gkn
        gv32_ref[...] = gv
    gv_ref[...] = gv.astype(BF16)


def _inproj_call(x, mod_l, wl, cst, rope):
    t = x.shape[0]
    nt = t // TM_IN
    row = lambda i: (i, 0)
    if rope:
        per_b = DEC_SEQ // TM_IN
        mod_map = lambda i: (1 + i // per_b, 0, 0)
        tab = lambda i: (i % per_b, 0)
    else:
        mod_map = lambda i: (0, 0, 0)
    in_specs = [
        pl.BlockSpec((TM_IN, D_MODEL), row),
        pl.BlockSpec((1, 1, N_ADA), mod_map),
        _const_spec((D_MODEL, N_Z)),
        _const_spec((MLA_Q_RANK, MLA_HEADS * HEAD_PAD)),
        _const_spec((MLA_KV_RANK, MLA_HEADS * HEAD_PAD)),
        _const_spec((MLA_KV_RANK, MLA_W)),
        _const_spec((1, MLA_Q_RANK)),
        _const_spec((1, MLA_KV_RANK)),
        _const_spec((1, LANES)),
        _const_spec((1, LANES)),
        _const_spec((LANES, LANES)),
    ]
    args = [x, mod_l, wl["w_in"], wl["w_uq"], wl["w_uk"], wl["w_uv"], wl["qn"], wl["kvn"], wl["gqn"], wl["gkn"],
            cst["ones64"]]
    outs = [(HG_COLS, F32), (MLA_HEADS * HEAD_PAD, BF16), (MLA_HEADS * HEAD_PAD, BF16), (MLA_W, BF16),
            (GQA_W, BF16), (LANES, BF16), (LANES, BF16)]
    if rope:
        in_specs += [pl.BlockSpec((TM_IN, LANES), tab)] * 6
        args += [cst["mla_c"], cst["mla_sa"], cst["mla_sb"], cst["gqa_c"], cst["gqa_sa"], cst["gqa_sb"]]
    else:
        outs += [(MLA_KV_RANK, F32), (MLA_ROPE, F32), (LANES, F32), (LANES, F32)]
    return pl.pallas_call(
        functools.partial(_inproj_kernel, rope),
        grid=(nt,),
        in_specs=in_specs,
        out_specs=[pl.BlockSpec((TM_IN, w), row) for w, _ in outs],
        out_shape=[jax.ShapeDtypeStruct((t, w), dt) for w, dt in outs],
        compiler_params=_cparams(("arbitrary",)),
        name="inproj_dec" if rope else "inproj_ctx",
    )(*args)


def _cachekv_kernel(ckv_ref, kpe_ref, w_uk_ref, w_uv_ref, place_ref, k_ref, v_ref):
    ckv_b = ckv_ref[...].astype(BF16)
    k_ref[...] = (_dot(ckv_b, w_uk_ref[...]) + _dot(kpe_ref[...].astype(BF16), place_ref[...])).astype(BF16)
    v_ref[...] = _dot(ckv_b, w_uv_ref[...]).astype(BF16)


def _cachekv_call(ckv, kpe, wl, cst):
    t = ckv.shape[0]
    return pl.pallas_call(
        _cachekv_kernel,
        grid=(1,),
        in_specs=[_const_spec((t, MLA_KV_RANK)), _const_spec((t, MLA_ROPE)),
                  _const_spec((MLA_KV_RANK, MLA_HEADS * HEAD_PAD)), _const_spec((MLA_KV_RANK, MLA_W)),
                  _const_spec((MLA_ROPE, MLA_HEADS * HEAD_PAD))],
        out_specs=[_const_spec((t, MLA_HEADS * HEAD_PAD)), _const_spec((t, MLA_W))],
        out_shape=[jax.ShapeDtypeStruct((t, MLA_HEADS * HEAD_PAD), BF16), jax.ShapeDtypeStruct((t, MLA_W), BF16)],
        compiler_params=_cparams(("arbitrary",)),
        name="cache_kv",
    )(ckv, kpe, wl["w_uk"], wl["w_uv"], cst["kpe_place"])


def _bcast_row(x, r, rows):
    return jnp.broadcast_to(x[r:r + 1, :], (rows, x.shape[1]))


def _cumsum_rows(tri, x):
    hi = x.astype(BF16)
    r1 = x - hi.astype(F32)
    mid = r1.astype(BF16)
    lo = (r1 - mid.astype(F32)).astype(BF16)
    return _dot(tri, hi) + _dot(tri, mid) + _dot(tri, lo)


def _hgrn_chunk(fwd, q, lf, kk, iv, st, tri, lvl, ones, blk, lane_lo):
    C = HG_CHUNK
    c = _cumsum_rows(tri, lf)
    trow = lax.broadcasted_iota(jnp.int32, (SUBLANES, LANES), 0)

    ws = []
    for j in range(C // SUBLANES):
        sl = slice(j * SUBLANES, (j + 1) * SUBLANES)
        cs, qs, ks = c[sl], q[sl], kk[sl]
        for s in range(SUBLANES):
            e = jnp.exp(jnp.minimum(cs - _bcast_row(cs, s, SUBLANES), 0.0))
            keep = (trow >= s) if fwd else (trow <= s)
            ws.append(jnp.where(keep, qs * e * _bcast_row(ks, s, SUBLANES), 0.0))
    red = _dot(jnp.concatenate(ws, axis=0).astype(BF16), ones)
    diag = []
    for j in range(C // SUBLANES):
        isub = iv[j * SUBLANES:(j + 1) * SUBLANES]
        acc = None
        for s in range(SUBLANES):
            base = (j * SUBLANES + s) * SUBLANES
            term = red[base:base + SUBLANES] * _bcast_row(isub, s, SUBLANES)
            acc = term if acc is None else acc + term
        diag.append(acc)
    o = jnp.concatenate(diag, axis=0)

    rowi = lax.broadcasted_iota(jnp.int32, (C, LANES), 0)
    a_tot = jnp.zeros((2 * C, C), F32)
    m = SUBLANES
    bit = 3
    while m < C:
        pieces = []
        for p in range(C // (2 * m)):
            anchor = p * 2 * m + (m - 1 if fwd else m)
            pieces.append(_bcast_row(c, anchor, 2 * m))
        r = jnp.concatenate(pieces, axis=0)
        hi_row = ((rowi >> bit) & 1) == 1
        q_side = hi_row if fwd else jnp.logical_not(hi_row)
        e = jnp.exp(jnp.where(q_side, c - r, r - c))
        qt = q * e
        kt = (kk * e).astype(BF16)
        qstack = jnp.concatenate([jnp.where(lane_lo, qt, 0.0), jnp.where(lane_lo, 0.0, qt)], axis=0).astype(BF16)
        a_tot = jnp.where(lvl == bit, _dot_nt(qstack, kt), a_tot)
        m *= 2
        bit += 1
    off = _dot(a_tot.astype(BF16), iv.astype(BF16))
    o = o + jnp.where(lane_lo, off[0:C], off[C:2 * C])

    edge = _bcast_row(c, C - 1 if fwd else 0, C)
    o = o + _dot_nt((q * jnp.exp(c)).astype(BF16), st.astype(BF16))
    k_dec = (kk * jnp.exp(edge - c)).astype(BF16)
    st_new = st * jnp.exp(edge[0:1]) + _dot_tn(iv.astype(BF16), k_dec) * blk
    return o, st_new


def _hgrn_kernel(layer, nchunks, has_state, zq_ref, zff_ref, zfb_ref, zi_ref, zg_ref, lb_ref, g_ref,
                 trif_ref, trib_ref, lvlf_ref, lvlb_ref, ones_ref, blk_ref, *rest):
    if has_state:
        s0_ref, o_ref, of_scr, ob_scr = rest
    else:
        o_ref, st_ref, of_scr, ob_scr = rest
    C = HG_CHUNK
    lane_lo = lax.broadcasted_iota(jnp.int32, (1, LANES), 1) < HG_DK

    lb_all = [lb_ref[j] for j in range(DEPTH)]
    mx = functools.reduce(jnp.maximum, lb_all)
    ex = [jnp.exp(v - mx) for v in lb_all]
    den = functools.reduce(lambda a, b: a + b, ex)
    sm = [v / den for v in ex]
    lbs = functools.reduce(lambda a, b: a + b, sm[:layer + 1]) - sm[0]
    lbf, lbb = lbs[0:1], lbs[1:2]

    def gates(zf, lbd):
        f = lbd + (1.0 - lbd) * jax.nn.sigmoid(zf)
        return jnp.log(jnp.maximum(f, GATE_FLOOR)), (1.0 - lbd) * jax.nn.sigmoid(-zf)

    def block_diag_t(s_ref_d):
        za = jnp.zeros((HG_DK, HG_DV), F32)
        top = jnp.concatenate([s_ref_d[0], za], axis=1)
        bot = jnp.concatenate([za, s_ref_d[1]], axis=1)
        return jnp.concatenate([top, bot], axis=0).T

    if has_state:
        st_f0 = block_diag_t(s0_ref[0, 0])
        st_b0 = block_diag_t(s0_ref[0, 1])
    else:
        st_f0 = jnp.zeros((LANES, LANES), F32)
        st_b0 = st_f0

    def body(j, carry):
        st_f, st_b = carry
        rf = pl.ds(pl.multiple_of(j * C, C), C)
        rb = pl.ds(pl.multiple_of((nchunks - 1 - j) * C, C), C)
        lf, kf = gates(zff_ref[rf, :], lbf)
        qf = _silu(zq_ref[rf, :]) * (HG_DK ** -0.5)
        o_f, st_f = _hgrn_chunk(True, qf, lf, kf, zi_ref[rf, :], st_f, trif_ref[...], lvlf_ref[...],
                                ones_ref[...], blk_ref[...], lane_lo)
        of_scr[rf, :] = o_f
        lb_, kb = gates(zfb_ref[rb, :], lbb)
        qb = _silu(zq_ref[rb, :]) * (HG_DK ** -0.5)
        o_b, st_b = _hgrn_chunk(False, qb, lb_, kb, zi_ref[rb, :], st_b, trib_ref[...], lvlb_ref[...],
                                ones_ref[...], blk_ref[...], lane_lo)
        ob_scr[rb, :] = o_b
        return st_f, st_b

    st_f, st_b = lax.fori_loop(0, nchunks, body, (st_f0, st_b0))

    o = of_scr[...] + ob_scr[...]
    ms = _group_mean(o * o, ones_ref[...] * (1.0 / HG_DV))
    o_ref[...] = (o * lax.rsqrt(ms + EPS) * g_ref[...] * _silu(zg_ref[...])).astype(BF16)
    if not has_state:
        for d, st in enumerate((st_f, st_b)):
            s_kv = st.T
            st_ref[0, d, 0] = s_kv[0:HG_DK, 0:HG_DV]
            st_ref[0, d, 1] = s_kv[HG_DK:2 * HG_DK, HG_DV:2 * HG_DV]


def _hgrn_call(zh, lb, g, cst, layer, nb, s, s0):
    has_state = s0 is not None
    nchunks = s // HG_CHUNK
    npair = HG_HEADS // 2
    part = lambda k: pl.BlockSpec((s, LANES), lambda b, hp: (b, k * npair + hp))
    in_specs = [part(0), part(1), part(2), part(3), part(4),
                pl.BlockSpec((DEPTH, 2, LANES), lambda b, hp: (0, 0, hp)),
                _const_spec((1, LANES)),
                _const_spec((HG_CHUNK, HG_CHUNK)), _const_spec((HG_CHUNK, HG_CHUNK)),
                _const_spec((2 * HG_CHUNK, HG_CHUNK)), _const_spec((2 * HG_CHUNK, HG_CHUNK)),
                _const_spec((LANES, LANES)), _const_spec((LANES, LANES))]
    args = [zh, zh, zh, zh, zh, lb, g, cst["tri_f"], cst["tri_b"], cst["lvl_f"], cst["lvl_b"], cst["ones"],
            cst["blk"]]
    out_specs = [pl.BlockSpec((s, LANES), lambda b, hp: (b, hp))]
    out_shape = [jax.ShapeDtypeStruct((nb * s, HG_W), BF16)]
    if has_state:
        in_specs.append(pl.BlockSpec((1, 2, 2, HG_DK, HG_DV), lambda b, hp: (b, 0, hp, 0, 0)))
        args.append(s0)
    else:
        out_specs.append(pl.BlockSpec((1, 2, 2, HG_DK, HG_DV), lambda b, hp: (b, 0, hp, 0, 0)))
        out_shape.append(jax.ShapeDtypeStruct((nb, 2, HG_HEADS, HG_DK, HG_DV), F32))
    return pl.pallas_call(
        functools.partial(_hgrn_kernel, layer, nchunks, has_state),
        grid=(nb, npair),
        in_specs=in_specs,
        out_specs=out_specs,
        out_shape=out_shape,
        scratch_shapes=[pltpu.VMEM((s, LANES), F32), pltpu.VMEM((s, LANES), F32)],
        compiler_params=_cparams(("arbitrary", "arbitrary")),
        name="hgrn_dec" if has_state else "hgrn_ctx",
    )(*args)


def _attend(q, ks, vs):
    ss = [_dot_nt(q, k) for k in ks]
    mx = functools.reduce(jnp.maximum, [jnp.max(s, axis=-1, keepdims=True) for s in ss])
    es = [jnp.exp(s - mx) for s in ss]
    den = functools.reduce(lambda a, b: a + b, [jnp.sum(e, axis=-1, keepdims=True) for e in es])
    o = functools.reduce(lambda a, b: a + b, [_dot(e.astype(BF16), v) for e, v in zip(es, vs)])
    return o / den


def _attn_kernel(has_cache, qm_ref, gq_ref, km_ref, vm_ref, gk_ref, gv_ref, *rest):
    if has_cache:
        kc_ref, vc_ref, gkc_ref, gvc_ref, o_ref = rest
    else:
        (o_ref,) = rest
    sq = qm_ref.shape[0]
    lane_lo = lax.broadcasted_iota(jnp.int32, (sq, LANES), 1) < MLA_V
    for p in range(MLA_HEADS // 2):
        vsl = slice(p * LANES, (p + 1) * LANES)
        vs = [vc_ref[:, vsl], vm_ref[:, vsl]] if has_cache else [vm_ref[:, vsl]]
        outs = []
        for hd in (2 * p, 2 * p + 1):
            sl = slice(hd * HEAD_PAD, (hd + 1) * HEAD_PAD)
            ks = [kc_ref[:, sl], km_ref[:, sl]] if has_cache else [km_ref[:, sl]]
            outs.append(_attend(qm_ref[:, sl], ks, vs))
        o_ref[:, vsl] = jnp.where(lane_lo, outs[0], outs[1]).astype(BF16)
    ks = [gkc_ref[...], gk_ref[...]] if has_cache else [gk_ref[...]]
    vs = [gvc_ref[...], gv_ref[...]] if has_cache else [gv_ref[...]]
    zero = jnp.zeros((sq, LANES), BF16)
    for p in range(GQA_W // LANES):
        sl = slice(p * LANES, (p + 1) * LANES)
        qp = gq_ref[:, sl]
        oa = _attend(jnp.where(lane_lo, qp, zero), ks, vs)
        ob = _attend(jnp.where(lane_lo, zero, qp), ks, vs)
        o_ref[:, MLA_W + p * LANES:MLA_W + (p + 1) * LANES] = jnp.where(lane_lo, oa, ob).astype(BF16)


def _attn_call(qm, gq, km, vm, gk, gv, cache):
    has_cache = cache is not None
    t = qm.shape[0]
    widths = (MLA_HEADS * HEAD_PAD, GQA_W, MLA_HEADS * HEAD_PAD, MLA_W, LANES, LANES)
    if has_cache:
        nq = DEC_SEQ // TQ_DEC
        grid = (DEC_BATCH, nq)
        qmap = lambda b, i: (b * nq + i, 0)
        kmap = lambda b, i: (b, 0)
        in_specs = [pl.BlockSpec((TQ_DEC, widths[0]), qmap), pl.BlockSpec((TQ_DEC, widths[1]), qmap)]
        in_specs += [pl.BlockSpec((DEC_SEQ, w), kmap) for w in widths[2:]]
        in_specs += [pl.BlockSpec((PAST_LEN, w), kmap) for w in widths[2:]]
        args = [qm, gq, km, vm, gk, gv, *cache]
        out_spec = pl.BlockSpec((TQ_DEC, MLA_W + GQA_W), qmap)
        sem = ("arbitrary", "arbitrary")
    else:
        grid = (t // SEQ,)
        bmap = lambda b: (b, 0)
        in_specs = [pl.BlockSpec((SEQ, w), bmap) for w in widths]
        args = [qm, gq, km, vm, gk, gv]
        out_spec = pl.BlockSpec((SEQ, MLA_W + GQA_W), bmap)
        sem = ("arbitrary",)
    return pl.pallas_call(
        functools.partial(_attn_kernel, has_cache),
        grid=grid,
        in_specs=in_specs,
        out_specs=out_spec,
        out_shape=jax.ShapeDtypeStruct((t, MLA_W + GQA_W), BF16),
        compiler_params=_cparams(sem),
        name="attn_dec" if has_cache else "attn_ctx",
    )(*args)


def _ffn_kernel(hg_ref, at_ref, x_ref, mod_ref, w_out_ref, ln1g_ref, ln1b_ref, w_gu_ref, w_dn_ref, ln2g_ref,
                ln2b_ref, o_ref):
    mod = mod_ref[0]
    g1 = mod[:, 2 * D_MODEL:3 * D_MODEL]
    sh2 = mod[:, 3 * D_MODEL:4 * D_MODEL]
    sc2 = mod[:, 4 * D_MODEL:5 * D_MODEL]
    g2 = mod[:, 5 * D_MODEL:6 * D_MODEL]
    y = _dot(hg_ref[...], w_out_ref[0:HG_W, :]) + _dot(at_ref[...], w_out_ref[HG_W:D_MODEL, :])
    x1 = _layernorm_rows(ALPHA * x_ref[...] + g1 * y, ln1g_ref[...], ln1b_ref[...])
    hf = (x1 * (1.0 + sc2) + sh2).astype(BF16)
    ch = D_FF // FF_CHUNKS
    f = None
    for c in range(FF_CHUNKS):
        gate = _dot(hf, w_gu_ref[:, c * ch:(c + 1) * ch])
        up = _dot(hf, w_gu_ref[:, D_FF + c * ch:D_FF + (c + 1) * ch])
        part = _dot((_silu(gate) * up).astype(BF16), w_dn_ref[c * ch:(c + 1) * ch, :])
        f = part if f is None else f + part
    o_ref[...] = _layernorm_rows(ALPHA * x1 + g2 * f, ln2g_ref[...], ln2b_ref[...])


def _ffn_call(hg, at, x, mod_l, wl, dec):
    t = x.shape[0]
    row = lambda i: (i, 0)
    if dec:
        per_b = DEC_SEQ // TM_FFN
        mod_map = lambda i: (1 + i // per_b, 0, 0)
    else:
        mod_map = lambda i: (0, 0, 0)
    once = pl.Buffered(1)
    wspec = lambda shape: pl.BlockSpec(shape, lambda i: (0, 0), pipeline_mode=once)
    return pl.pallas_call(
        _ffn_kernel,
        grid=(t // TM_FFN,),
        in_specs=[
            pl.BlockSpec((TM_FFN, HG_W), row),
            pl.BlockSpec((TM_FFN, MLA_W + GQA_W), row),
            pl.BlockSpec((TM_FFN, D_MODEL), row),
            pl.BlockSpec((1, 1, N_ADA), mod_map),
            wspec((D_MODEL, D_MODEL)),
            wspec((1, D_MODEL)), wspec((1, D_MODEL)),
            wspec((D_MODEL, 2 * D_FF)),
            wspec((D_FF, D_MODEL)),
            wspec((1, D_MODEL)), wspec((1, D_MODEL)),
        ],
        out_specs=pl.BlockSpec((TM_FFN, D_MODEL), row),
        out_shape=jax.ShapeDtypeStruct((t, D_MODEL), F32),
        compiler_params=_cparams(("arbitrary",)),
        name="ffn_dec" if dec else "ffn_ctx",
    )(hg, at, x, mod_l, wl["w_out"], wl["ln1_g"], wl["ln1_b"], wl["w_gu"], wl["w_dn"], wl["ln2_g"], wl["ln2_b"])


def _rope_tables(width, rot_off, rot_dim):
    t = np.arange(DEC_SEQ)
    pos = (t // GRID_W, t % GRID_W)
    half = rot_dim // 4
    freqs = ROPE_THETA ** (-np.arange(half, dtype=np.float64) / half)
    c = np.ones((DEC_SEQ, LANES), np.float64)
    sa = np.zeros((DEC_SEQ, LANES), np.float64)
    sb = np.zeros((DEC_SEQ, LANES), np.float64)
    for blk in range(LANES // width):
        for axis in range(2):
            base = blk * width + rot_off + axis * 2 * half
            ang = pos[axis][:, None].astype(np.float64) * freqs[None, :]
            c[:, base:base + half] = np.cos(ang)
            c[:, base + half:base + 2 * half] = np.cos(ang)
            sa[:, base:base + half] = -np.sin(ang)
            sb[:, base + half:base + 2 * half] = np.sin(ang)
    return tuple(jnp.asarray(a, F32) for a in (c, sa, sb))


def _constants():
    C = HG_CHUNK
    t = np.arange(C)
    tri_f = (t[None, :] <= t[:, None]).astype(np.float32)
    tri_b = (t[None, :] >= t[:, None]).astype(np.float32)
    x = t[:, None] ^ t[None, :]
    top = np.where(x > 0, np.floor(np.log2(np.maximum(x, 1))).astype(np.int32), -1)
    lvl_f = np.where((t[:, None] > t[None, :]) & (top >= 3), top, -1).astype(np.int32)
    lvl_b = np.where((t[:, None] < t[None, :]) & (top >= 3), top, -1).astype(np.int32)
    grp = np.arange(LANES) // HG_DK
    blk = (grp[:, None] == grp[None, :]).astype(np.float32)
    place = np.zeros((MLA_ROPE, MLA_HEADS * HEAD_PAD), np.float32)
    for hd in range(MLA_HEADS):
        place[np.arange(MLA_ROPE), hd * HEAD_PAD + MLA_NOPE + np.arange(MLA_ROPE)] = 1.0
    mla_c, mla_sa, mla_sb = _rope_tables(HEAD_PAD, MLA_NOPE, MLA_ROPE)
    gqa_c, gqa_sa, gqa_sb = _rope_tables(GQA_HD, 0, GQA_HD)
    return {
        "tri_f": jnp.asarray(tri_f, BF16), "tri_b": jnp.asarray(tri_b, BF16),
        "lvl_f": jnp.asarray(np.concatenate([lvl_f, lvl_f], axis=0)),
        "lvl_b": jnp.asarray(np.concatenate([lvl_b, lvl_b], axis=0)),
        "ones": jnp.asarray(blk, BF16), "ones64": jnp.asarray(blk / GQA_HD, BF16), "blk": jnp.asarray(blk, F32),
        "kpe_place": jnp.asarray(place, BF16),
        "mla_c": mla_c, "mla_sa": mla_sa, "mla_sb": mla_sb, "gqa_c": gqa_c, "gqa_sa": gqa_sa, "gqa_sb": gqa_sb,
    }


_GQ_HEAD_ORDER = (0, 3, 1, 4, 2, 5)


def _layer_weights(l, w_in, mla_q_norm, mla_w_uq, mla_kv_norm, mla_w_ukv, gqa_q_norm, gqa_k_norm, w_out, ln1_g,
                   ln1_b, w_ffn_in, w_ffn_out, ln2_g, ln2_b):
    wi = w_in[l]
    sizes = (256, 256, 256, 256, 256, MLA_Q_RANK, MLA_KV_RANK, MLA_ROPE, GQA_W, 128, 128)
    offs = np.concatenate([[0], np.cumsum(sizes)])
    seg = lambda k: wi[:, offs[k]:offs[k + 1]]
    gq_w = seg(8).reshape(D_MODEL, GQA_HEADS, GQA_HD)[:, _GQ_HEAD_ORDER, :].reshape(D_MODEL, GQA_W)
    kr = seg(7)
    kr_pad = jnp.pad(kr, ((0, 0), (MLA_NOPE, HEAD_PAD - MLA_NOPE - MLA_ROPE)))
    w_in_r = jnp.concatenate([seg(0), seg(1), seg(2), seg(3), seg(4), seg(5), seg(6), gq_w, seg(9), seg(10),
                              kr_pad, kr], axis=1).astype(BF16)
    uq = mla_w_uq[l].reshape(MLA_Q_RANK, MLA_HEADS, MLA_NOPE + MLA_ROPE)
    uq = jnp.pad(uq, ((0, 0), (0, 0), (0, HEAD_PAD - MLA_NOPE - MLA_ROPE))).reshape(MLA_Q_RANK, -1).astype(BF16)
    ukv = mla_w_ukv[l].reshape(MLA_KV_RANK, MLA_HEADS, MLA_NOPE + MLA_V)
    uk = jnp.pad(ukv[:, :, :MLA_NOPE], ((0, 0), (0, 0), (0, HEAD_PAD - MLA_NOPE))).reshape(MLA_KV_RANK, -1)
    uv = ukv[:, :, MLA_NOPE:].reshape(MLA_KV_RANK, MLA_W)
    wo = w_out[l]
    wo_gqa = wo[HG_W + MLA_W:].reshape(GQA_HEADS, GQA_HD, D_MODEL)[_GQ_HEAD_ORDER, :, :].reshape(GQA_W, D_MODEL)
    wo_r = jnp.concatenate([wo[:HG_W + MLA_W], wo_gqa], axis=0).astype(BF16)
    row = lambda v: v.reshape(1, -1)
    return {
        "w_in": w_in_r, "w_uq": uq, "w_uk": uk.astype(BF16), "w_uv": uv.astype(BF16),
        "qn": row(mla_q_norm[l]), "kvn": row(mla_kv_norm[l]),
        "gqn": row(jnp.tile(gqa_q_norm[l], LANES // GQA_HD)), "gkn": row(jnp.tile(gqa_k_norm[l], LANES // GQA_HD)),
        "w_out": wo_r, "ln1_g": row(ln1_g[l]), "ln1_b": row(ln1_b[l]),
        "w_gu": w_ffn_in[l].astype(BF16), "w_dn": w_ffn_out[l].astype(BF16),
        "ln2_g": row(ln2_g[l]), "ln2_b": row(ln2_b[l]),
    }


def kernel(x_prompt, x_sample, cache_mla_ckv, cache_mla_kpe, cache_gqa_k, cache_gqa_v, state_hgrn, c, c_ctx, w_ada,
           b_ada, w_in, hg_lb, hg_norm, mla_q_norm, mla_w_uq, mla_kv_norm, mla_w_ukv, gqa_q_norm, gqa_k_norm, w_out,
           ln1_g, ln1_b, w_ffn_in, w_ffn_out, ln2_g, ln2_b):
    cst = _constants()
    cvec = jnp.concatenate([c_ctx[None, :], c, jnp.zeros((MOD_ROWS - 1 - DEC_BATCH, D_MODEL), F32)], axis=0)
    mod = _ada_call(cvec, w_ada, b_ada)

    x_ctx = x_prompt.reshape(BATCH * SEQ, D_MODEL)
    x_dec = x_sample.reshape(DEC_BATCH * DEC_SEQ, D_MODEL)
    states = []
    for l in range(DEPTH):
        wl = _layer_weights(l, w_in, mla_q_norm, mla_w_uq, mla_kv_norm, mla_w_ukv, gqa_q_norm, gqa_k_norm, w_out,
                            ln1_g, ln1_b, w_ffn_in, w_ffn_out, ln2_g, ln2_b)
        mod_l = mod[l].reshape(MOD_ROWS, 1, N_ADA)
        g_hg = jnp.tile(hg_norm[l], LANES // HG_DV).reshape(1, LANES)

        zh, qm, km, vm, gq, gk, gv, ckv, kpe, gk32, gv32 = _inproj_call(x_ctx, mod_l, wl, cst, rope=False)
        hg_out, st = _hgrn_call(zh, hg_lb, g_hg, cst, l, BATCH, SEQ, None)
        at = _attn_call(qm, gq, km, vm, gk, gv, None)
        x_ctx = _ffn_call(hg_out, at, x_ctx, mod_l, wl, dec=False)
        states.append((ckv, kpe, gk32, gv32, st))

        zh, qm, km, vm, gq, gk, gv = _inproj_call(x_dec, mod_l, wl, cst, rope=True)
        hg_out = _hgrn_call(zh, hg_lb, g_hg, cst, l, DEC_BATCH, DEC_SEQ, state_hgrn[:, l])[0]
        kc, vc = _cachekv_call(cache_mla_ckv[:, l].reshape(DEC_BATCH * PAST_LEN, MLA_KV_RANK),
                               cache_mla_kpe[:, l].reshape(DEC_BATCH * PAST_LEN, MLA_ROPE), wl, cst)
        gkc = cache_gqa_k[:, l].reshape(DEC_BATCH * PAST_LEN, LANES).astype(BF16)
        gvc = cache_gqa_v[:, l].reshape(DEC_BATCH * PAST_LEN, LANES).astype(BF16)
        at = _attn_call(qm, gq, km, vm, gk, gv, (kc, vc, gkc, gvc))
        x_dec = _ffn_call(hg_out, at, x_dec, mod_l, wl, dec=True)

    y_prompt = x_ctx.reshape(BATCH, SEQ, D_MODEL)
    y_sample = x_dec.reshape(DEC_BATCH, DEC_SEQ, D_MODEL)
    stack = lambda k, shape: jnp.stack([s[k].reshape(shape) for s in states], axis=1)
    return (y_prompt, y_sample,
            stack(0, (BATCH, SEQ, MLA_KV_RANK)),
            stack(1, (BATCH, SEQ, MLA_ROPE)),
            stack(2, (BATCH, SEQ, GQA_KV_HEADS, GQA_HD)),
            stack(3, (BATCH, SEQ, GQA_KV_HEADS, GQA_HD)),
            stack(4, (BATCH, 2, HG_HEADS, HG_DK, HG_DV)))
```

```python
import functools

import numpy as np
import jax
import jax.numpy as jnp
from jax import lax
from jax.experimental import pallas as pl
from jax.experimental.pallas import tpu as pltpu

F32 = jnp.float32
BF16 = jnp.bfloat16

D_MODEL = 1024
BATCH = 32
SEQ = 256
DEPTH = 2
DEC_BATCH = 2
DEC_SEQ = 2048
PAST_LEN = 256
GRID_W = 64
ROPE_THETA = 10000.0
EPS = 1e-6
GATE_FLOOR = 1e-30
HG_HEADS = 4
HG_DK = 64
HG_DV = 64
MLA_HEADS = 6
MLA_NOPE = 64
MLA_ROPE = 32
MLA_V = 64
MLA_Q_RANK = 256
MLA_KV_RANK = 128
GQA_HEADS = 6
GQA_KV_HEADS = 2
GQA_HD = 64
HG_W = HG_HEADS * HG_DV
MLA_W = MLA_HEADS * MLA_V
GQA_W = GQA_HEADS * GQA_HD
D_FF = 2816
MLA_SCALE = (MLA_NOPE + MLA_ROPE) ** -0.5
GQA_SCALE = GQA_HD ** -0.5
ALPHA = (2 * DEPTH) ** 0.25

LANES = 128
SUBLANES = 8
HEAD_PAD = 128
N_ADA = 6 * D_MODEL
MOD_ROWS = 8

HG_COLS = 5 * HG_HEADS * HG_DK
OFF_MQ = HG_COLS
OFF_MKV = OFF_MQ + MLA_Q_RANK
OFF_GQ = OFF_MKV + MLA_KV_RANK
OFF_GK = OFF_GQ + GQA_W
OFF_GV = OFF_GK + GQA_KV_HEADS * GQA_HD
OFF_KPE_PAD = OFF_GV + GQA_KV_HEADS * GQA_HD
OFF_KPE = OFF_KPE_PAD + HEAD_PAD
N_Z = OFF_KPE + MLA_ROPE

TM_IN = 512
TM_FFN = 512
TQ_DEC = 512
GQA_ROWS = 256
HG_CHUNK = 128
HG_UNROLL = 2
FF_CHUNKS = 2
VMEM_LIMIT = 56 * 1024 * 1024


def _cparams(sem):
    return pltpu.CompilerParams(dimension_semantics=sem, vmem_limit_bytes=VMEM_LIMIT)


def _const_spec(shape):
    nd = len(shape)
    return pl.BlockSpec(shape, lambda *_: (0,) * nd)


def _silu(x):
    return x * jax.nn.sigmoid(x)


def _dot(a, b):
    return jnp.dot(a, b, preferred_element_type=F32)


def _dot_nt(a, b):
    return lax.dot_general(a, b, (((1,), (1,)), ((), ())), preferred_element_type=F32)


def _dot_tn(a, b):
    return lax.dot_general(a, b, (((0,), (0,)), ((), ())), preferred_element_type=F32)


def _rms_rows(x, g):
    y = x * lax.rsqrt(jnp.mean(x * x, axis=-1, keepdims=True) + EPS)
    return y * g


def _layernorm_rows(x, g, b):
    xc = x - jnp.mean(x, axis=-1, keepdims=True)
    var = jnp.mean(xc * xc, axis=-1, keepdims=True)
    return xc * lax.rsqrt(var + EPS) * g + b


def _group_mean(x, ones_blk):
    return _dot(x.astype(BF16), ones_blk)


def _ada_kernel(c_ref, w_ref, b_ref, o_ref):
    s = _silu(c_ref[...]).astype(BF16)
    o_ref[0] = _dot(s, w_ref[0].astype(BF16)) + b_ref[0]


def _ada_call(cvec, w_ada, b_ada):
    tn = 1536
    return pl.pallas_call(
        _ada_kernel,
        grid=(DEPTH, N_ADA // tn),
        in_specs=[
            pl.BlockSpec((MOD_ROWS, D_MODEL), lambda l, j: (0, 0)),
            pl.BlockSpec((1, D_MODEL, tn), lambda l, j: (l, 0, j)),
            pl.BlockSpec((1, 1, tn), lambda l, j: (l, 0, j)),
        ],
        out_specs=pl.BlockSpec((1, MOD_ROWS, tn), lambda l, j: (l, 0, j)),
        out_shape=jax.ShapeDtypeStruct((DEPTH, MOD_ROWS, N_ADA), F32),
        compiler_params=_cparams(("arbitrary", "arbitrary")),
        name="ada_mod",
    )(cvec, w_ada, b_ada.reshape(DEPTH, 1, N_ADA))


def _rope_block(x, c, sa, sb, half):
    return x * c + pltpu.roll(x, LANES - half, 1) * sa + pltpu.roll(x, half, 1) * sb


def _inproj_kernel(rope, x_ref, mod_ref, w_in_ref, w_uq_ref, w_uk_ref, w_uv_ref, qn_ref, kvn_ref, gqn_ref,
                   gkn_ref, ones_ref, *rest):
    if rope:
        (mc_ref, msa_ref, msb_ref, gc_ref, gsa_ref, gsb_ref,
         zh_ref, qm_ref, km_ref, vm_ref, gq_ref, gk_ref, gv_ref) = rest
    else:
        (zh_ref, qm_ref, km_ref, vm_ref, gq_ref, gk_ref, gv_ref,
         ckv_ref, kpe_ref, gk32_ref, gv32_ref) = rest
    mod = mod_ref[0]
    sh1 = mod[:, 0:D_MODEL]
    sc1 = mod[:, D_MODEL:2 * D_MODEL]
    h = (x_ref[...] * (1.0 + sc1) + sh1).astype(BF16)
    z = _dot(h, w_in_ref[...])
    zh_ref[...] = z[:, 0:HG_COLS]
    mq = z[:, OFF_MQ:OFF_MKV]
    mkv = z[:, OFF_MKV:OFF_GQ]
    gq = z[:, OFF_GQ:OFF_GK]
    gk = z[:, OFF_GK:OFF_GV]
    gv = z[:, OFF_GV:OFF_KPE_PAD]
    kpe_pad = z[:, OFF_KPE_PAD:OFF_KPE]
    ones = ones_ref[...]

    q = _dot(_rms_rows(mq, qn_ref[...]).astype(BF16), w_uq_ref[...])
    ckv = _rms_rows(mkv, kvn_ref[...])
    ckv_b = ckv.astype(BF16)
    kn = _dot(ckv_b, w_uk_ref[...])
    vm_ref[...] = _dot(ckv_b, w_uv_ref[...]).astype(BF16)
    if rope:
        mc, msa, msb = mc_ref[...], msa_ref[...], msb_ref[...]
        kpe_pad = _rope_block(kpe_pad, mc, msa, msb, MLA_ROPE // 4)
    for hd in range(MLA_HEADS):
        sl = slice(hd * HEAD_PAD, (hd + 1) * HEAD_PAD)
        qh = q[:, sl]
        if rope:
            qh = _rope_block(qh, mc, msa, msb, MLA_ROPE // 4)
        qm_ref[:, sl] = (qh * MLA_SCALE).astype(BF16)
        km_ref[:, sl] = (kn[:, sl] + kpe_pad).astype(BF16)

    gkn = gk * lax.rsqrt(_group_mean(gk * gk, ones) + EPS) * gkn_ref[...]
    for p in range(GQA_W // LANES):
        sl = slice(p * LANES, (p + 1) * LANES)
        gqp = gq[:, sl]
        gqp = gqp * lax.rsqrt(_group_mean(gqp * gqp, ones) + EPS) * gqn_ref[...]
        if rope:
            gqp = _rope_block(gqp, gc_ref[...], gsa_ref[...], gsb_ref[...], GQA_HD // 4)
        gq_ref[:, sl] = (gqp * GQA_SCALE).astype(BF16)
    if rope:
        gk_ref[...] = _rope_block(gkn, gc_ref[...], gsa_ref[...], gsb_ref[...], GQA_HD // 4).astype(BF16)
    else:
        gk_ref[...] = gkn.astype(BF16)
        ckv_ref[...] = ckv
        kpe_ref[...] = z[:, OFF_KPE:N_Z]
        gk32_ref[...] = gkn
        gv32_ref[...] = gv
    gv_ref[...] = gv.astype(BF16)


def _inproj_call(x, mod_l, wl, cst, rope):
    t = x.shape[0]
    nt = t // TM_IN
    row = lambda i: (i, 0)
    if rope:
        per_b = DEC_SEQ // TM_IN
        mod_map = lambda i: (1 + i // per_b, 0, 0)
        tab = lambda i: (i % per_b, 0)
    else:
        mod_map = lambda i: (0, 0, 0)
    in_specs = [
        pl.BlockSpec((TM_IN, D_MODEL), row),
        pl.BlockSpec((1, 1, N_ADA), mod_map),
        _const_spec((D_MODEL, N_Z)),
        _const_spec((MLA_Q_RANK, MLA_HEADS * HEAD_PAD)),
        _const_spec((MLA_KV_RANK, MLA_HEADS * HEAD_PAD)),
        _const_spec((MLA_KV_RANK, MLA_W)),
        _const_spec((1, MLA_Q_RANK)),
        _const_spec((1, MLA_KV_RANK)),
        _const_spec((1, LANES)),
        _const_spec((1, LANES)),
        _const_spec((LANES, LANES)),
    ]
    args = [x, mod_l, wl["w_in"], wl["w_uq"], wl["w_uk"], wl["w_uv"], wl["qn"], wl["kvn"], wl["gqn"], wl["gkn"],
            cst["ones64"]]
    outs = [(HG_COLS, F32), (MLA_HEADS * HEAD_PAD, BF16), (MLA_HEADS * HEAD_PAD, BF16), (MLA_W, BF16),
            (GQA_W, BF16), (LANES, BF16), (LANES, BF16)]
    if rope:
        in_specs += [pl.BlockSpec((TM_IN, LANES), tab)] * 6
        args += [cst["mla_c"], cst["mla_sa"], cst["mla_sb"], cst["gqa_c"], cst["gqa_sa"], cst["gqa_sb"]]
    else:
        outs += [(MLA_KV_RANK, F32), (MLA_ROPE, F32), (LANES, F32), (LANES, F32)]
    return pl.pallas_call(
        functools.partial(_inproj_kernel, rope),
        grid=(nt,),
        in_specs=in_specs,
        out_specs=[pl.BlockSpec((TM_IN, w), row) for w, _ in outs],
        out_shape=[jax.ShapeDtypeStruct((t, w), dt) for w, dt in outs],
        compiler_params=_cparams(("arbitrary",)),
        name="inproj_dec" if rope else "inproj_ctx",
    )(*args)


def _cachekv_kernel(ckv_ref, kpe_ref, w_uk_ref, w_uv_ref, place_ref, k_ref, v_ref):
    ckv_b = ckv_ref[...].astype(BF16)
    k_ref[...] = (_dot(ckv_b, w_uk_ref[...]) + _dot(kpe_ref[...].astype(BF16), place_ref[...])).astype(BF16)
    v_ref[...] = _dot(ckv_b, w_uv_ref[...]).astype(BF16)


def _cachekv_call(ckv, kpe, wl, cst):
    t = ckv.shape[0]
    return pl.pallas_call(
        _cachekv_kernel,
        grid=(1,),
        in_specs=[_const_spec((t, MLA_KV_RANK)), _const_spec((t, MLA_ROPE)),
                  _const_spec((MLA_KV_RANK, MLA_HEADS * HEAD_PAD)), _const_spec((MLA_KV_RANK, MLA_W)),
                  _const_spec((MLA_ROPE, MLA_HEADS * HEAD_PAD))],
        out_specs=[_const_spec((t, MLA_HEADS * HEAD_PAD)), _const_spec((t, MLA_W))],
        out_shape=[jax.ShapeDtypeStruct((t, MLA_HEADS * HEAD_PAD), BF16), jax.ShapeDtypeStruct((t, MLA_W), BF16)],
        compiler_params=_cparams(("arbitrary",)),
        name="cache_kv",
    )(ckv, kpe, wl["w_uk"], wl["w_uv"], cst["kpe_place"])


def _bcast_row(x, r, rows):
    return jnp.broadcast_to(x[r:r + 1, :], (rows, x.shape[1]))


def _cumsum_rows(tri, x):
    hi = x.astype(BF16)
    r1 = x - hi.astype(F32)
    mid = r1.astype(BF16)
    lo = (r1 - mid.astype(F32)).astype(BF16)
    return _dot(tri, hi) + _dot(tri, mid) + _dot(tri, lo)


def _hgrn_chunks(items, states, tri, lvl, stk, blk, lane_lo):
    C = HG_CHUNK
    zero_b = jnp.zeros((SUBLANES, LANES), F32)
    trow = lax.broadcasted_iota(jnp.int32, (SUBLANES, LANES), 0)
    for it in items:
        it["c"] = _cumsum_rows(tri[it["fwd"]], it["lf"])
    for it in items:
        it["q_a"] = jnp.where(lane_lo, it["q"], 0.0)
        it["q_b"] = jnp.where(lane_lo, 0.0, it["q"])
        it["ck"] = it["c"] - jnp.log(it["kk"])

    m = C // 2
    bit = m.bit_length() - 1
    while m >= SUBLANES:
        for it in items:
            fwd, c = it["fwd"], it["c"]
            qa_rows, qb_rows, k_rows = [], [], []
            for b in range(C // m):
                rows = slice(b * m, (b + 1) * m)
                anchor = (b // 2) * 2 * m + (m - 1 if fwd else m)
                r = _bcast_row(c, anchor, m)
                if (b % 2 == 1) == fwd:
                    e = jnp.exp(c[rows] - r)
                    qa_rows.append(it["q_a"][rows] * e)
                    qb_rows.append(it["q_b"][rows] * e)
                    k_rows += [zero_b] * (m // SUBLANES)
                else:
                    e = jnp.exp(r - c[rows])
                    qa_rows += [zero_b] * (m // SUBLANES)
                    qb_rows += [zero_b] * (m // SUBLANES)
                    k_rows.append(it["kk"][rows] * e)
            prod = _dot_nt(jnp.concatenate(qa_rows + qb_rows, axis=0).astype(BF16),
                           jnp.concatenate(k_rows, axis=0).astype(BF16))
            it["a"] = prod if "a" not in it else jnp.where(lvl[fwd] == bit, prod, it["a"])
        m //= 2
        bit -= 1

    for it in items:
        it["w"] = []
    for j in range(C // SUBLANES):
        sl = slice(j * SUBLANES, (j + 1) * SUBLANES)
        for it in items:
            cs, cks, qs = it["c"][sl], it["ck"][sl], it["q"][sl]
            w_s = []
            for s in range(SUBLANES):
                e = jnp.exp(cs - _bcast_row(cks, s, SUBLANES))
                keep = (trow >= s) if it["fwd"] else (trow <= s)
                w_s.append(jnp.where(keep, qs * e, 0.0))
            it["w"].append(jnp.concatenate(w_s, axis=1))
    for it in items:
        red = _dot(jnp.concatenate(it["w"], axis=0).astype(BF16), stk)
        it["a"] = jnp.where(lvl[it["fwd"]] == 0,
                            jnp.concatenate([red[:, 0:LANES], red[:, LANES:2 * LANES]], axis=0), it["a"])
    for it in items:
        off = _dot(it["a"].astype(BF16), it["iv"].astype(BF16))
        it["o"] = jnp.where(lane_lo, off[0:C], off[C:2 * C])
        it["edge"] = _bcast_row(it["c"], C - 1 if it["fwd"] else 0, C)
        it["q_dec"] = (it["q"] * jnp.exp(it["c"])).astype(BF16)
        k_dec = (it["kk"] * jnp.exp(it["edge"] - it["c"])).astype(BF16)
        it["upd"] = _dot_tn(it["iv"].astype(BF16), k_dec) * blk

    states = list(states)
    outs = []
    for it in items:
        st = states[it["sid"]]
        outs.append(it["o"] + _dot_nt(it["q_dec"], st.astype(BF16)))
        states[it["sid"]] = st * jnp.exp(it["edge"][0:1]) + it["upd"]
    return outs, states


def _hgrn_kernel(layer, nchunks, has_state, zq_ref, zff_ref, zfb_ref, zi_ref, zg_ref, lb_ref, g_ref,
                 trif_ref, trib_ref, lvlf_ref, lvlb_ref, stk_ref, ones_ref, blk_ref, *rest):
    if has_state:
        s0_ref, o_ref, of_scr, ob_scr = rest
    else:
        o_ref, st_ref, of_scr, ob_scr = rest
    C = HG_CHUNK
    lane_lo = lax.broadcasted_iota(jnp.int32, (1, LANES), 1) < HG_DK

    lb_all = [lb_ref[j] for j in range(DEPTH)]
    mx = functools.reduce(jnp.maximum, lb_all)
    ex = [jnp.exp(v - mx) for v in lb_all]
    den = functools.reduce(lambda a, b: a + b, ex)
    sm = [v / den for v in ex]
    lbs = functools.reduce(lambda a, b: a + b, sm[:layer + 1]) - sm[0]
    lbf, lbb = lbs[0:1], lbs[1:2]

    def gates(zf, lbd):
        f = lbd + (1.0 - lbd) * jax.nn.sigmoid(zf)
        return jnp.log(jnp.maximum(f, GATE_FLOOR)), (1.0 - lbd) * jax.nn.sigmoid(-zf)

    def block_diag_t(s_ref_d):
        za = jnp.zeros((HG_DK, HG_DV), F32)
        top = jnp.concatenate([s_ref_d[0], za], axis=1)
        bot = jnp.concatenate([za, s_ref_d[1]], axis=1)
        return jnp.concatenate([top, bot], axis=0).T

    if has_state:
        st_f0 = block_diag_t(s0_ref[0, 0])
        st_b0 = block_diag_t(s0_ref[0, 1])
    else:
        st_f0 = jnp.zeros((LANES, LANES), F32)
        st_b0 = st_f0

    def body(j, carry):
        tri = {True: trif_ref[...], False: trib_ref[...]}
        lvl = {True: lvlf_ref[...], False: lvlb_ref[...]}
        items = []
        for u in range(HG_UNROLL):
            jf = j * HG_UNROLL + u
            for fwd in (True, False):
                rows = pl.ds(pl.multiple_of((jf if fwd else nchunks - 1 - jf) * C, C), C)
                lf, kk = gates((zff_ref if fwd else zfb_ref)[rows, :], lbf if fwd else lbb)
                items.append({"fwd": fwd, "sid": 0 if fwd else 1, "rows": rows, "lf": lf, "kk": kk,
                              "q": _silu(zq_ref[rows, :]) * (HG_DK ** -0.5), "iv": zi_ref[rows, :]})
        outs, states = _hgrn_chunks(items, carry, tri, lvl, stk_ref[...], blk_ref[...], lane_lo)
        for it, o in zip(items, outs):
            (of_scr if it["fwd"] else ob_scr)[it["rows"], :] = o
        return tuple(states)

    st_f, st_b = lax.fori_loop(0, nchunks // HG_UNROLL, body, (st_f0, st_b0))

    o = of_scr[...] + ob_scr[...]
    ms = _group_mean(o * o, ones_ref[...] * (1.0 / HG_DV))
    o_ref[...] = (o * lax.rsqrt(ms + EPS) * g_ref[...] * _silu(zg_ref[...])).astype(BF16)
    if not has_state:
        for d, st in enumerate((st_f, st_b)):
            s_kv = st.T
            st_ref[0, d, 0] = s_kv[0:HG_DK, 0:HG_DV]
            st_ref[0, d, 1] = s_kv[HG_DK:2 * HG_DK, HG_DV:2 * HG_DV]


def _hgrn_call(zh, lb, g, cst, layer, nb, s, s0):
    has_state = s0 is not None
    nchunks = s // HG_CHUNK
    npair = HG_HEADS // 2
    part = lambda k: pl.BlockSpec((s, LANES), lambda b, hp: (b, k * npair + hp))
    in_specs = [part(0), part(1), part(2), part(3), part(4),
                pl.BlockSpec((DEPTH, 2, LANES), lambda b, hp: (0, 0, hp)),
                _const_spec((1, LANES)),
                _const_spec((HG_CHUNK, HG_CHUNK)), _const_spec((HG_CHUNK, HG_CHUNK)),
                _const_spec((2 * HG_CHUNK, HG_CHUNK)), _const_spec((2 * HG_CHUNK, HG_CHUNK)),
                _const_spec((SUBLANES * LANES, 2 * LANES)),
                _const_spec((LANES, LANES)), _const_spec((LANES, LANES))]
    args = [zh, zh, zh, zh, zh, lb, g, cst["tri_f"], cst["tri_b"], cst["lvl_f"], cst["lvl_b"], cst["stk"],
            cst["ones"], cst["blk"]]
    out_specs = [pl.BlockSpec((s, LANES), lambda b, hp: (b, hp))]
    out_shape = [jax.ShapeDtypeStruct((nb * s, HG_W), BF16)]
    if has_state:
        in_specs.append(pl.BlockSpec((1, 2, 2, HG_DK, HG_DV), lambda b, hp: (b, 0, hp, 0, 0)))
        args.append(s0)
    else:
        out_specs.append(pl.BlockSpec((1, 2, 2, HG_DK, HG_DV), lambda b, hp: (b, 0, hp, 0, 0)))
        out_shape.append(jax.ShapeDtypeStruct((nb, 2, HG_HEADS, HG_DK, HG_DV), F32))
    return pl.pallas_call(
        functools.partial(_hgrn_kernel, layer, nchunks, has_state),
        grid=(nb, npair),
        in_specs=in_specs,
        out_specs=out_specs,
        out_shape=out_shape,
        scratch_shapes=[pltpu.VMEM((s, LANES), F32), pltpu.VMEM((s, LANES), F32)],
        compiler_params=_cparams(("arbitrary", "arbitrary")),
        name="hgrn_dec" if has_state else "hgrn_ctx",
    )(*args)


def _attend(q, ks, vs):
    ss = [_dot_nt(q, k) for k in ks]
    mx = functools.reduce(jnp.maximum, [jnp.max(s, axis=-1, keepdims=True) for s in ss])
    es = [jnp.exp(s - mx) for s in ss]
    den = functools.reduce(lambda a, b: a + b, [jnp.sum(e, axis=-1, keepdims=True) for e in es])
    o = functools.reduce(lambda a, b: a + b, [_dot(e.astype(BF16), v) for e, v in zip(es, vs)])
    return o / den


def _attn_kernel(has_cache, qm_ref, gq_ref, km_ref, vm_ref, gk_ref, gv_ref, *rest):
    if has_cache:
        kc_ref, vc_ref, gkc_ref, gvc_ref, o_ref = rest
    else:
        (o_ref,) = rest
    sq = qm_ref.shape[0]
    lane_lo = lax.broadcasted_iota(jnp.int32, (sq, LANES), 1) < MLA_V
    for p in range(MLA_HEADS // 2):
        vsl = slice(p * LANES, (p + 1) * LANES)
        vs = [vc_ref[:, vsl], vm_ref[:, vsl]] if has_cache else [vm_ref[:, vsl]]
        outs = []
        for hd in (2 * p, 2 * p + 1):
            sl = slice(hd * HEAD_PAD, (hd + 1) * HEAD_PAD)
            ks = [kc_ref[:, sl], km_ref[:, sl]] if has_cache else [km_ref[:, sl]]
            outs.append(_attend(qm_ref[:, sl], ks, vs))
        o_ref[:, vsl] = jnp.where(lane_lo, outs[0], outs[1]).astype(BF16)
    ks = [gkc_ref[...], gk_ref[...]] if has_cache else [gk_ref[...]]
    vs = [gvc_ref[...], gv_ref[...]] if has_cache else [gv_ref[...]]
    lo = lax.broadcasted_iota(jnp.int32, (GQA_ROWS, LANES), 1) < GQA_HD
    zero = jnp.zeros((GQA_ROWS, LANES), BF16)
    for p in range(GQA_W // LANES):
        sl = slice(p * LANES, (p + 1) * LANES)
        for r0 in range(0, sq, GQA_ROWS):
            qp = gq_ref[r0:r0 + GQA_ROWS, sl]
            o2 = _attend(jnp.concatenate([jnp.where(lo, qp, zero), jnp.where(lo, zero, qp)], axis=0), ks, vs)
            o_ref[r0:r0 + GQA_ROWS, MLA_W + p * LANES:MLA_W + (p + 1) * LANES] = jnp.where(
                lo, o2[0:GQA_ROWS], o2[GQA_ROWS:2 * GQA_ROWS]).astype(BF16)


def _attn_call(qm, gq, km, vm, gk, gv, cache):
    has_cache = cache is not None
    t = qm.shape[0]
    widths = (MLA_HEADS * HEAD_PAD, GQA_W, MLA_HEADS * HEAD_PAD, MLA_W, LANES, LANES)
    if has_cache:
        nq = DEC_SEQ // TQ_DEC
        grid = (DEC_BATCH, nq)
        qmap = lambda b, i: (b * nq + i, 0)
        kmap = lambda b, i: (b, 0)
        in_specs = [pl.BlockSpec((TQ_DEC, widths[0]), qmap), pl.BlockSpec((TQ_DEC, widths[1]), qmap)]
        once = pl.Buffered(1)
        in_specs += [pl.BlockSpec((DEC_SEQ, w), kmap, pipeline_mode=once) for w in widths[2:]]
        in_specs += [pl.BlockSpec((PAST_LEN, w), kmap, pipeline_mode=once) for w in widths[2:]]
        args = [qm, gq, km, vm, gk, gv, *cache]
        out_spec = pl.BlockSpec((TQ_DEC, MLA_W + GQA_W), qmap)
        sem = ("arbitrary", "arbitrary")
    else:
        grid = (t // SEQ,)
        bmap = lambda b: (b, 0)
        in_specs = [pl.BlockSpec((SEQ, w), bmap) for w in widths]
        args = [qm, gq, km, vm, gk, gv]
        out_spec = pl.BlockSpec((SEQ, MLA_W + GQA_W), bmap)
        sem = ("arbitrary",)
    return pl.pallas_call(
        functools.partial(_attn_kernel, has_cache),
        grid=grid,
        in_specs=in_specs,
        out_specs=out_spec,
        out_shape=jax.ShapeDtypeStruct((t, MLA_W + GQA_W), BF16),
        compiler_params=_cparams(sem),
        name="attn_dec" if has_cache else "attn_ctx",
    )(*args)


def _ffn_kernel(hg_ref, at_ref, x_ref, mod_ref, w_out_ref, ln1g_ref, ln1b_ref, w_gu_ref, w_dn_ref, ln2g_ref,
                ln2b_ref, o_ref):
    mod = mod_ref[0]
    g1 = mod[:, 2 * D_MODEL:3 * D_MODEL]
    sh2 = mod[:, 3 * D_MODEL:4 * D_MODEL]
    sc2 = mod[:, 4 * D_MODEL:5 * D_MODEL]
    g2 = mod[:, 5 * D_MODEL:6 * D_MODEL]
    y = _dot(hg_ref[...], w_out_ref[0:HG_W, :]) + _dot(at_ref[...], w_out_ref[HG_W:D_MODEL, :])
    x1 = _layernorm_rows(ALPHA * x_ref[...] + g1 * y, ln1g_ref[...], ln1b_ref[...])
    hf = (x1 * (1.0 + sc2) + sh2).astype(BF16)
    ch = D_FF // FF_CHUNKS
    f = None
    for c in range(FF_CHUNKS):
        gate = _dot(hf, w_gu_ref[:, c * ch:(c + 1) * ch])
        up = _dot(hf, w_gu_ref[:, D_FF + c * ch:D_FF + (c + 1) * ch])
        part = _dot((_silu(gate) * up).astype(BF16), w_dn_ref[c * ch:(c + 1) * ch, :])
        f = part if f is None else f + part
    o_ref[...] = _layernorm_rows(ALPHA * x1 + g2 * f, ln2g_ref[...], ln2b_ref[...])


def _ffn_call(hg, at, x, mod_l, wl, dec):
    t = x.shape[0]
    row = lambda i: (i, 0)
    if dec:
        per_b = DEC_SEQ // TM_FFN
        mod_map = lambda i: (1 + i // per_b, 0, 0)
    else:
        mod_map = lambda i: (0, 0, 0)
    once = pl.Buffered(1)
    wspec = lambda shape: pl.BlockSpec(shape, lambda i: (0, 0), pipeline_mode=once)
    return pl.pallas_call(
        _ffn_kernel,
        grid=(t // TM_FFN,),
        in_specs=[
            pl.BlockSpec((TM_FFN, HG_W), row),
            pl.BlockSpec((TM_FFN, MLA_W + GQA_W), row),
            pl.BlockSpec((TM_FFN, D_MODEL), row),
            pl.BlockSpec((1, 1, N_ADA), mod_map),
            wspec((D_MODEL, D_MODEL)),
            wspec((1, D_MODEL)), wspec((1, D_MODEL)),
            wspec((D_MODEL, 2 * D_FF)),
            wspec((D_FF, D_MODEL)),
            wspec((1, D_MODEL)), wspec((1, D_MODEL)),
        ],
        out_specs=pl.BlockSpec((TM_FFN, D_MODEL), row),
        out_shape=jax.ShapeDtypeStruct((t, D_MODEL), F32),
        compiler_params=_cparams(("arbitrary",)),
        name="ffn_dec" if dec else "ffn_ctx",
    )(hg, at, x, mod_l, wl["w_out"], wl["ln1_g"], wl["ln1_b"], wl["w_gu"], wl["w_dn"], wl["ln2_g"], wl["ln2_b"])


def _rope_tables(width, rot_off, rot_dim):
    t = np.arange(DEC_SEQ)
    pos = (t // GRID_W, t % GRID_W)
    half = rot_dim // 4
    freqs = ROPE_THETA ** (-np.arange(half, dtype=np.float64) / half)
    c = np.ones((DEC_SEQ, LANES), np.float64)
    sa = np.zeros((DEC_SEQ, LANES), np.float64)
    sb = np.zeros((DEC_SEQ, LANES), np.float64)
    for blk in range(LANES // width):
        for axis in range(2):
            base = blk * width + rot_off + axis * 2 * half
            ang = pos[axis][:, None].astype(np.float64) * freqs[None, :]
            c[:, base:base + half] = np.cos(ang)
            c[:, base + half:base + 2 * half] = np.cos(ang)
            sa[:, base:base + half] = -np.sin(ang)
            sb[:, base + half:base + 2 * half] = np.sin(ang)
    return tuple(jnp.asarray(a, F32) for a in (c, sa, sb))


def _constants():
    C = HG_CHUNK
    t = np.arange(C)
    tri_f = (t[None, :] <= t[:, None]).astype(np.float32)
    tri_b = (t[None, :] >= t[:, None]).astype(np.float32)
    x = t[:, None] ^ t[None, :]
    top = np.where(x > 0, np.floor(np.log2(np.maximum(x, 1))).astype(np.int32), -1)
    same = (t[:, None] // SUBLANES) == (t[None, :] // SUBLANES)
    lvl_f = np.where(same, np.where(t[:, None] >= t[None, :], 0, -1),
                     np.where(t[:, None] > t[None, :], top, -1)).astype(np.int32)
    lvl_b = np.where(same, np.where(t[:, None] <= t[None, :], 0, -1),
                     np.where(t[:, None] < t[None, :], top, -1)).astype(np.int32)
    grp = np.arange(LANES) // HG_DK
    blk = (grp[:, None] == grp[None, :]).astype(np.float32)
    stk = np.zeros((SUBLANES, LANES, 2, C), np.float32)
    for s in range(SUBLANES):
        for hb in range(2):
            stk[s, hb * HG_DK:(hb + 1) * HG_DK, hb, s::SUBLANES] = 1.0
    stk = stk.reshape(SUBLANES * LANES, 2 * C)
    place = np.zeros((MLA_ROPE, MLA_HEADS * HEAD_PAD), np.float32)
    for hd in range(MLA_HEADS):
        place[np.arange(MLA_ROPE), hd * HEAD_PAD + MLA_NOPE + np.arange(MLA_ROPE)] = 1.0
    mla_c, mla_sa, mla_sb = _rope_tables(HEAD_PAD, MLA_NOPE, MLA_ROPE)
    gqa_c, gqa_sa, gqa_sb = _rope_tables(GQA_HD, 0, GQA_HD)
    return {
        "tri_f": jnp.asarray(tri_f, BF16), "tri_b": jnp.asarray(tri_b, BF16),
        "lvl_f": jnp.asarray(np.concatenate([lvl_f, lvl_f], axis=0)),
        "lvl_b": jnp.asarray(np.concatenate([lvl_b, lvl_b], axis=0)),
        "stk": jnp.asarray(stk, BF16), "ones": jnp.asarray(blk, BF16), "ones64": jnp.asarray(blk / GQA_HD, BF16), "blk": jnp.asarray(blk, F32),
        "kpe_place": jnp.asarray(place, BF16),
        "mla_c": mla_c, "mla_sa": mla_sa, "mla_sb": mla_sb, "gqa_c": gqa_c, "gqa_sa": gqa_sa, "gqa_sb": gqa_sb,
    }


_GQ_HEAD_ORDER = (0, 3, 1, 4, 2, 5)


def _layer_weights(l, w_in, mla_q_norm, mla_w_uq, mla_kv_norm, mla_w_ukv, gqa_q_norm, gqa_k_norm, w_out, ln1_g,
                   ln1_b, w_ffn_in, w_ffn_out, ln2_g, ln2_b):
    wi = w_in[l]
    sizes = (256, 256, 256, 256, 256, MLA_Q_RANK, MLA_KV_RANK, MLA_ROPE, GQA_W, 128, 128)
    offs = np.concatenate([[0], np.cumsum(sizes)])
    seg = lambda k: wi[:, offs[k]:offs[k + 1]]
    gq_w = seg(8).reshape(D_MODEL, GQA_HEADS, GQA_HD)[:, _GQ_HEAD_ORDER, :].reshape(D_MODEL, GQA_W)
    kr = seg(7)
    kr_pad = jnp.pad(kr, ((0, 0), (MLA_NOPE, HEAD_PAD - MLA_NOPE - MLA_ROPE)))
    w_in_r = jnp.concatenate([seg(0), seg(1), seg(2), seg(3), seg(4), seg(5), seg(6), gq_w, seg(9), seg(10),
                              kr_pad, kr], axis=1).astype(BF16)
    uq = mla_w_uq[l].reshape(MLA_Q_RANK, MLA_HEADS, MLA_NOPE + MLA_ROPE)
    uq = jnp.pad(uq, ((0, 0), (0, 0), (0, HEAD_PAD - MLA_NOPE - MLA_ROPE))).reshape(MLA_Q_RANK, -1).astype(BF16)
    ukv = mla_w_ukv[l].reshape(MLA_KV_RANK, MLA_HEADS, MLA_NOPE + MLA_V)
    uk = jnp.pad(ukv[:, :, :MLA_NOPE], ((0, 0), (0, 0), (0, HEAD_PAD - MLA_NOPE))).reshape(MLA_KV_RANK, -1)
    uv = ukv[:, :, MLA_NOPE:].reshape(MLA_KV_RANK, MLA_W)
    wo = w_out[l]
    wo_gqa = wo[HG_W + MLA_W:].reshape(GQA_HEADS, GQA_HD, D_MODEL)[_GQ_HEAD_ORDER, :, :].reshape(GQA_W, D_MODEL)
    wo_r = jnp.concatenate([wo[:HG_W + MLA_W], wo_gqa], axis=0).astype(BF16)
    row = lambda v: v.reshape(1, -1)
    return {
        "w_in": w_in_r, "w_uq": uq, "w_uk": uk.astype(BF16), "w_uv": uv.astype(BF16),
        "qn": row(mla_q_norm[l]), "kvn": row(mla_kv_norm[l]),
        "gqn": row(jnp.tile(gqa_q_norm[l], LANES // GQA_HD)), "gkn": row(jnp.tile(gqa_k_norm[l], LANES // GQA_HD)),
        "w_out": wo_r, "ln1_g": row(ln1_g[l]), "ln1_b": row(ln1_b[l]),
        "w_gu": w_ffn_in[l].astype(BF16), "w_dn": w_ffn_out[l].astype(BF16),
        "ln2_g": row(ln2_g[l]), "ln2_b": row(ln2_b[l]),
    }


def kernel(x_prompt, x_sample, cache_mla_ckv, cache_mla_kpe, cache_gqa_k, cache_gqa_v, state_hgrn, c, c_ctx, w_ada,
           b_ada, w_in, hg_lb, hg_norm, mla_q_norm, mla_w_uq, mla_kv_norm, mla_w_ukv, gqa_q_norm, gqa_k_norm, w_out,
           ln1_g, ln1_b, w_ffn_in, w_ffn_out, ln2_g, ln2_b):
    cst = _constants()
    cvec = jnp.concatenate([c_ctx[None, :], c, jnp.zeros((MOD_ROWS - 1 - DEC_BATCH, D_MODEL), F32)], axis=0)
    mod = _ada_call(cvec, w_ada, b_ada)

    x_ctx = x_prompt.reshape(BATCH * SEQ, D_MODEL)
    x_dec = x_sample.reshape(DEC_BATCH * DEC_SEQ, D_MODEL)
    states = []
    for l in range(DEPTH):
        wl = _layer_weights(l, w_in, mla_q_norm, mla_w_uq, mla_kv_norm, mla_w_ukv, gqa_q_norm, gqa_k_norm, w_out,
                            ln1_g, ln1_b, w_ffn_in, w_ffn_out, ln2_g, ln2_b)
        mod_l = mod[l].reshape(MOD_ROWS, 1, N_ADA)
        g_hg = jnp.tile(hg_norm[l], LANES // HG_DV).reshape(1, LANES)

        zh, qm, km, vm, gq, gk, gv, ckv, kpe, gk32, gv32 = _inproj_call(x_ctx, mod_l, wl, cst, rope=False)
        hg_out, st = _hgrn_call(zh, hg_lb, g_hg, cst, l, BATCH, SEQ, None)
        at = _attn_call(qm, gq, km, vm, gk, gv, None)
        x_ctx = _ffn_call(hg_out, at, x_ctx, mod_l, wl, dec=False)
        states.append((ckv, kpe, gk32, gv32, st))

        zh, qm, km, vm, gq, gk, gv = _inproj_call(x_dec, mod_l, wl, cst, rope=True)
        hg_out = _hgrn_call(zh, hg_lb, g_hg, cst, l, DEC_BATCH, DEC_SEQ, state_hgrn[:, l])[0]
        kc, vc = _cachekv_call(cache_mla_ckv[:, l].reshape(DEC_BATCH * PAST_LEN, MLA_KV_RANK),
                               cache_mla_kpe[:, l].reshape(DEC_BATCH * PAST_LEN, MLA_ROPE), wl, cst)
        gkc = cache_gqa_k[:, l].reshape(DEC_BATCH * PAST_LEN, LANES).astype(BF16)
        gvc = cache_gqa_v[:, l].reshape(DEC_BATCH * PAST_LEN, LANES).astype(BF16)
        at = _attn_call(qm, gq, km, vm, gk, gv, (kc, vc, gkc, gvc))
        x_dec = _ffn_call(hg_out, at, x_dec, mod_l, wl, dec=True)

    y_prompt = x_ctx.reshape(BATCH, SEQ, D_MODEL)
    y_sample = x_dec.reshape(DEC_BATCH, DEC_SEQ, D_MODEL)
    stack = lambda k, shape: jnp.stack([s[k].reshape(shape) for s in states], axis=1)
    return (y_prompt, y_sample,
            stack(0, (BATCH, SEQ, MLA_KV_RANK)),
            stack(1, (BATCH, SEQ, MLA_ROPE)),
            stack(2, (BATCH, SEQ, GQA_KV_HEADS, GQA_HD)),
            stack(3, (BATCH, SEQ, GQA_KV_HEADS, GQA_HD)),
            stack(4, (BATCH, 2, HG_HEADS, HG_DK, HG_DV)))
```

```python
import functools

import numpy as np
import jax
import jax.numpy as jnp
from jax import lax
from jax.experimental import pallas as pl
from jax.experimental.pallas import tpu as pltpu

F32 = jnp.float32
BF16 = jnp.bfloat16

D_MODEL = 1024
BATCH = 32
SEQ = 256
DEPTH = 2
DEC_BATCH = 2
DEC_SEQ = 2048
PAST_LEN = 256
GRID_W = 64
ROPE_THETA = 10000.0
EPS = 1e-6
GATE_FLOOR = 1e-30
HG_HEADS = 4
HG_DK = 64
HG_DV = 64
MLA_HEADS = 6
MLA_NOPE = 64
MLA_ROPE = 32
MLA_V = 64
MLA_Q_RANK = 256
MLA_KV_RANK = 128
GQA_HEADS = 6
GQA_KV_HEADS = 2
GQA_HD = 64
HG_W = HG_HEADS * HG_DV
MLA_W = MLA_HEADS * MLA_V
GQA_W = GQA_HEADS * GQA_HD
D_FF = 2816
MLA_SCALE = (MLA_NOPE + MLA_ROPE) ** -0.5
GQA_SCALE = GQA_HD ** -0.5
ALPHA = (2 * DEPTH) ** 0.25

LANES = 128
SUBLANES = 8
HEAD_PAD = 128
MLA_QW = MLA_HEADS * HEAD_PAD
N_ADA = 6 * D_MODEL
MOD_ROWS = 8

N_IN = 5 * HG_HEADS * HG_DK + MLA_Q_RANK + MLA_KV_RANK + MLA_ROPE + GQA_W + 2 * GQA_KV_HEADS * GQA_HD
SRC_KPE = 5 * HG_HEADS * HG_DK + MLA_Q_RANK + MLA_KV_RANK
SRC_GQ = SRC_KPE + MLA_ROPE
SRC_GK = SRC_GQ + GQA_W
SRC_GV = SRC_GK + GQA_KV_HEADS * GQA_HD
HG_COLS = 5 * HG_HEADS * HG_DK
OFF_MQ = HG_COLS
OFF_MKV = OFF_MQ + MLA_Q_RANK
OFF_GQ = OFF_MKV + MLA_KV_RANK
OFF_GK = OFF_GQ + GQA_W
OFF_GV = OFF_GK + GQA_KV_HEADS * GQA_HD
OFF_KPE_PAD = OFF_GV + GQA_KV_HEADS * GQA_HD
OFF_KPE = OFF_KPE_PAD + HEAD_PAD
N_Z = OFF_KPE + MLA_ROPE

TM_IN = 512
TM_FFN = 512
TQ_DEC = 512
GQA_ROWS = 256
HG_CHUNK = 128
HG_UNROLL = 2
FF_CHUNKS = 2
PREP_STEPS = 4
VMEM_LIMIT = 56 * 1024 * 1024


def _cparams(sem):
    return pltpu.CompilerParams(dimension_semantics=sem, vmem_limit_bytes=VMEM_LIMIT)


def _const_spec(shape):
    nd = len(shape)
    return pl.BlockSpec(shape, lambda *_: (0,) * nd)


def _layer_spec(shape, layer, **kw):
    nd = len(shape)
    return pl.BlockSpec((None,) + tuple(shape), lambda *_: (layer,) + (0,) * nd, **kw)


_ANY = pl.BlockSpec(memory_space=pl.ANY)


def _silu(x):
    return x * jax.nn.sigmoid(x)


def _dot(a, b):
    return jnp.dot(a, b, preferred_element_type=F32)


def _dot_nt(a, b):
    return lax.dot_general(a, b, (((1,), (1,)), ((), ())), preferred_element_type=F32)


def _dot_tn(a, b):
    return lax.dot_general(a, b, (((0,), (0,)), ((), ())), preferred_element_type=F32)


def _rms_rows(x, g):
    y = x * lax.rsqrt(jnp.mean(x * x, axis=-1, keepdims=True) + EPS)
    return y * g


def _layernorm_rows(x, g, b):
    xc = x - jnp.mean(x, axis=-1, keepdims=True)
    var = jnp.mean(xc * xc, axis=-1, keepdims=True)
    return xc * lax.rsqrt(var + EPS) * g + b


def _group_mean(x, ones_blk):
    return _dot(x.astype(BF16), ones_blk)


def _twice(g):
    return jnp.concatenate([g, g], axis=1)


_GQ_HEAD_ORDER = (0, 3, 1, 4, 2, 5)


def _prep_in_kernel(w_ref, o_ref):
    x = w_ref[...]
    rows = x.shape[0]
    kr = x[:, SRC_KPE:SRC_GQ]
    gq = [x[:, SRC_GQ + h * GQA_HD:SRC_GQ + (h + 1) * GQA_HD] for h in _GQ_HEAD_ORDER]
    pieces = [x[:, 0:SRC_KPE]] + gq + [
        x[:, SRC_GK:SRC_GV], x[:, SRC_GV:N_IN],
        jnp.zeros((rows, MLA_NOPE), F32), kr, jnp.zeros((rows, HEAD_PAD - MLA_NOPE - MLA_ROPE), F32), kr]
    o_ref[...] = jnp.concatenate(pieces, axis=1).astype(BF16)


def _prep_in_call(w_in):
    rb = 128
    return pl.pallas_call(
        _prep_in_kernel,
        grid=(DEPTH, D_MODEL // rb),
        in_specs=[pl.BlockSpec((None, rb, N_IN), lambda l, j: (l, j, 0))],
        out_specs=pl.BlockSpec((None, rb, N_Z), lambda l, j: (l, j, 0)),
        out_shape=jax.ShapeDtypeStruct((DEPTH, D_MODEL, N_Z), BF16),
        compiler_params=_cparams(("arbitrary", "arbitrary")),
        name="prep_in",
    )(w_in)


def _prep_mla_kernel(uq_ref, ukv_ref, oq_ref, ok_ref, ov_ref):
    uq = uq_ref[...]
    dq = MLA_NOPE + MLA_ROPE
    zq = jnp.zeros((MLA_Q_RANK, HEAD_PAD - dq), F32)
    oq_ref[...] = jnp.concatenate(
        [p for h in range(MLA_HEADS) for p in (uq[:, h * dq:(h + 1) * dq], zq)], axis=1).astype(BF16)
    ukv = ukv_ref[...]
    lane = lax.broadcasted_iota(jnp.int32, ukv.shape, 1)
    ok_ref[...] = jnp.where((lane & (HEAD_PAD - 1)) < MLA_NOPE, ukv, 0.0).astype(BF16)
    ov_ref[...] = jnp.concatenate(
        [ukv[:, h * HEAD_PAD + MLA_NOPE:(h + 1) * HEAD_PAD] for h in range(MLA_HEADS)], axis=1).astype(BF16)


def _prep_mla_call(w_uq, w_ukv):
    lay = lambda n: pl.BlockSpec((None, n[0], n[1]), lambda l: (l, 0, 0))
    shapes = [(MLA_Q_RANK, MLA_QW), (MLA_KV_RANK, MLA_QW), (MLA_KV_RANK, MLA_W)]
    return pl.pallas_call(
        _prep_mla_kernel,
        grid=(DEPTH,),
        in_specs=[lay((MLA_Q_RANK, MLA_HEADS * (MLA_NOPE + MLA_ROPE))), lay((MLA_KV_RANK, MLA_QW))],
        out_specs=[lay(s) for s in shapes],
        out_shape=[jax.ShapeDtypeStruct((DEPTH,) + s, BF16) for s in shapes],
        compiler_params=_cparams(("arbitrary",)),
        name="prep_mla",
    )(w_uq, w_ukv)


def _prep_cast_kernel(a_ref, b_ref, c_ref, oa_ref, ob_ref, oc_ref):
    oa_ref[...] = a_ref[...].astype(BF16)
    ob_ref[...] = b_ref[...].astype(BF16)
    oc_ref[...] = c_ref[...].astype(BF16)


def _prep_cast_call(w_out, w_ffn_in, w_ffn_out):
    ws = (w_out, w_ffn_in, w_ffn_out)
    specs = [pl.BlockSpec((None, w.shape[1] // PREP_STEPS, w.shape[2]), lambda l, j: (l, j, 0)) for w in ws]
    return pl.pallas_call(
        _prep_cast_kernel,
        grid=(DEPTH, PREP_STEPS),
        in_specs=specs,
        out_specs=specs,
        out_shape=[jax.ShapeDtypeStruct(w.shape, BF16) for w in ws],
        compiler_params=_cparams(("arbitrary", "arbitrary")),
        name="prep_cast",
    )(*ws)


def _ada_kernel(c_ref, w_ref, b_ref, o_ref):
    s = _silu(c_ref[...]).astype(BF16)
    o_ref[...] = _dot(s, w_ref[...].astype(BF16)) + b_ref[pl.ds(pl.program_id(0), 1), :]


def _ada_call(cvec, w_ada, b_ada):
    tn = 1536
    return pl.pallas_call(
        _ada_kernel,
        grid=(DEPTH, N_ADA // tn),
        in_specs=[
            pl.BlockSpec((MOD_ROWS, D_MODEL), lambda l, j: (0, 0)),
            pl.BlockSpec((None, D_MODEL, tn), lambda l, j: (l, 0, j)),
            pl.BlockSpec((DEPTH, tn), lambda l, j: (0, j)),
        ],
        out_specs=pl.BlockSpec((None, MOD_ROWS, tn), lambda l, j: (l, 0, j)),
        out_shape=jax.ShapeDtypeStruct((DEPTH, MOD_ROWS, N_ADA), F32),
        compiler_params=_cparams(("arbitrary", "arbitrary")),
        name="ada_mod",
    )(cvec, w_ada, b_ada)


def _mod_row(mod_ref, dec, tiles_per_seq):
    if not dec:
        return mod_ref[0:1, :]
    return mod_ref[pl.ds(1 + pl.program_id(0) // tiles_per_seq, 1), :]


def _rope_block(x, c, sa, sb, half):
    return x * c + pltpu.roll(x, LANES - half, 1) * sa + pltpu.roll(x, half, 1) * sb


def _inproj_kernel(layer, rope, x_ref, mod_ref, w_in_ref, w_uq_ref, w_uk_ref, w_uv_ref, qn_ref, kvn_ref, gqn_ref,
                   gkn_ref, ones_ref, *rest):
    if rope:
        (mc_ref, msa_ref, msb_ref, gc_ref, gsa_ref, gsb_ref,
         zh_ref, qm_ref, km_ref, vm_ref, gq_ref, gk_ref, gv_ref) = rest
    else:
        rest = rest[-11:]
        (zh_ref, qm_ref, km_ref, vm_ref, gq_ref, gk_ref, gv_ref,
         ckv_ref, kpe_ref, gk32_ref, gv32_ref) = rest
    lrow = slice(layer, layer + 1)
    mod = _mod_row(mod_ref, rope, DEC_SEQ // TM_IN)
    sh1 = mod[:, 0:D_MODEL]
    sc1 = mod[:, D_MODEL:2 * D_MODEL]
    h = (x_ref[...] * (1.0 + sc1) + sh1).astype(BF16)
    z = _dot(h, w_in_ref[...])
    zh_ref[...] = z[:, 0:HG_COLS]
    mq = z[:, OFF_MQ:OFF_MKV]
    mkv = z[:, OFF_MKV:OFF_GQ]
    gq = z[:, OFF_GQ:OFF_GK]
    gk = z[:, OFF_GK:OFF_GV]
    gv = z[:, OFF_GV:OFF_KPE_PAD]
    kpe_pad = z[:, OFF_KPE_PAD:OFF_KPE]
    ones = ones_ref[...]
    gqn = _twice(gqn_ref[lrow, :])
    gkn_g = _twice(gkn_ref[lrow, :])

    q = _dot(_rms_rows(mq, qn_ref[lrow, :]).astype(BF16), w_uq_ref[...])
    ckv = _rms_rows(mkv, kvn_ref[lrow, :])
    ckv_b = ckv.astype(BF16)
    kn = _dot(ckv_b, w_uk_ref[...])
    vm_ref[...] = _dot(ckv_b, w_uv_ref[...]).astype(BF16)
    if rope:
        mc, msa, msb = mc_ref[...], msa_ref[...], msb_ref[...]
        kpe_pad = _rope_block(kpe_pad, mc, msa, msb, MLA_ROPE // 4)
    for hd in range(MLA_HEADS):
        sl = slice(hd * HEAD_PAD, (hd + 1) * HEAD_PAD)
        qh = q[:, sl]
        if rope:
            qh = _rope_block(qh, mc, msa, msb, MLA_ROPE // 4)
        qm_ref[:, sl] = (qh * MLA_SCALE).astype(BF16)
        km_ref[:, sl] = (kn[:, sl] + kpe_pad).astype(BF16)

    gkn = gk * lax.rsqrt(_group_mean(gk * gk, ones) + EPS) * gkn_g
    for p in range(GQA_W // LANES):
        sl = slice(p * LANES, (p + 1) * LANES)
        gqp = gq[:, sl]
        gqp = gqp * lax.rsqrt(_group_mean(gqp * gqp, ones) + EPS) * gqn
        if rope:
            gqp = _rope_block(gqp, gc_ref[...], gsa_ref[...], gsb_ref[...], GQA_HD // 4)
        gq_ref[:, sl] = (gqp * GQA_SCALE).astype(BF16)
    if rope:
        gk_ref[...] = _rope_block(gkn, gc_ref[...], gsa_ref[...], gsb_ref[...], GQA_HD // 4).astype(BF16)
    else:
        gk_ref[...] = gkn.astype(BF16)
        kpe = z[:, OFF_KPE:N_Z]
        for b in range(TM_IN // SEQ):
            rows = slice(b * SEQ, (b + 1) * SEQ)
            ckv_ref[b] = ckv[rows]
            kpe_ref[b] = kpe[rows]
            gk32_ref[b] = gkn[rows]
            gv32_ref[b] = gv[rows]
    gv_ref[...] = gv.astype(BF16)


def _inproj_call(x, mod, w, prm, cst, layer, rope, prev_states=None):
    t = x.shape[0]
    nt = t // TM_IN
    row = lambda i: (i, 0)
    in_specs = [
        pl.BlockSpec((TM_IN, D_MODEL), row),
        _layer_spec((MOD_ROWS, N_ADA), layer),
        _layer_spec((D_MODEL, N_Z), layer),
        _layer_spec((MLA_Q_RANK, MLA_QW), layer),
        _layer_spec((MLA_KV_RANK, MLA_QW), layer),
        _layer_spec((MLA_KV_RANK, MLA_W), layer),
        _const_spec((DEPTH, MLA_Q_RANK)),
        _const_spec((DEPTH, MLA_KV_RANK)),
        _const_spec((DEPTH, GQA_HD)),
        _const_spec((DEPTH, GQA_HD)),
        _const_spec((LANES, LANES)),
    ]
    args = [x, mod, w["w_in"], w["w_uq"], w["w_uk"], w["w_uv"], prm["mla_q_norm"], prm["mla_kv_norm"],
            prm["gqa_q_norm"], prm["gqa_k_norm"], cst["ones64"]]
    outs = [(HG_COLS, F32), (MLA_QW, BF16), (MLA_QW, BF16), (MLA_W, BF16), (GQA_W, BF16), (LANES, BF16),
            (LANES, BF16)]
    out_specs = [pl.BlockSpec((TM_IN, wd), row) for wd, _ in outs]
    out_shape = [jax.ShapeDtypeStruct((t, wd), dt) for wd, dt in outs]
    aliases = {}
    if rope:
        tab = lambda i: (i % (DEC_SEQ // TM_IN), 0)
        in_specs += [pl.BlockSpec((TM_IN, LANES), tab)] * 6
        args += [cst["mla_c"], cst["mla_sa"], cst["mla_sb"], cst["gqa_c"], cst["gqa_sa"], cst["gqa_sb"]]
    else:
        per = TM_IN // SEQ
        for wd in (MLA_KV_RANK, MLA_ROPE, LANES, LANES):
            out_specs.append(pl.BlockSpec((per, None, SEQ, wd), lambda i: (i, layer, 0, 0)))
            out_shape.append(jax.ShapeDtypeStruct((BATCH, DEPTH, SEQ, wd), F32))
        if prev_states is not None:
            for k, arr in enumerate(prev_states):
                aliases[len(args)] = len(outs) + k
                in_specs.append(_ANY)
                args.append(arr)
    return pl.pallas_call(
        functools.partial(_inproj_kernel, layer, rope),
        grid=(nt,),
        in_specs=in_specs,
        out_specs=out_specs,
        out_shape=out_shape,
        input_output_aliases=aliases,
        compiler_params=_cparams(("arbitrary",)),
        name="inproj_dec" if rope else "inproj_ctx",
    )(*args)


def _cachekv_kernel(ckv_ref, kpe_ref, w_uk_ref, w_uv_ref, place_ref, k_ref, v_ref):
    ckv_b = ckv_ref[...].astype(BF16)
    k_ref[...] = (_dot(ckv_b, w_uk_ref[...]) + _dot(kpe_ref[...].astype(BF16), place_ref[...])).astype(BF16)
    v_ref[...] = _dot(ckv_b, w_uv_ref[...]).astype(BF16)


def _cachekv_call(cache_ckv, cache_kpe, w, cst, layer):
    t = DEC_BATCH * PAST_LEN
    cache = lambda wd: pl.BlockSpec((None, None, PAST_LEN, wd), lambda b: (b, layer, 0, 0))
    return pl.pallas_call(
        _cachekv_kernel,
        grid=(DEC_BATCH,),
        in_specs=[cache(MLA_KV_RANK), cache(MLA_ROPE),
                  _layer_spec((MLA_KV_RANK, MLA_QW), layer), _layer_spec((MLA_KV_RANK, MLA_W), layer),
                  _const_spec((MLA_ROPE, MLA_QW))],
        out_specs=[pl.BlockSpec((PAST_LEN, MLA_QW), lambda b: (b, 0)),
                   pl.BlockSpec((PAST_LEN, MLA_W), lambda b: (b, 0))],
        out_shape=[jax.ShapeDtypeStruct((t, MLA_QW), BF16), jax.ShapeDtypeStruct((t, MLA_W), BF16)],
        compiler_params=_cparams(("arbitrary",)),
        name="cache_kv",
    )(cache_ckv, cache_kpe, w["w_uk"], w["w_uv"], cst["kpe_place"])


def _bcast_row(x, r, rows):
    return jnp.broadcast_to(x[r:r + 1, :], (rows, x.shape[1]))


def _cumsum_rows(tri, x):
    hi = x.astype(BF16)
    r1 = x - hi.astype(F32)
    mid = r1.astype(BF16)
    lo = (r1 - mid.astype(F32)).astype(BF16)
    return _dot(tri, hi) + _dot(tri, mid) + _dot(tri, lo)


def _hgrn_chunks(items, states, tri, lvl, stk, blk, lane_lo):
    C = HG_CHUNK
    zero_b = jnp.zeros((SUBLANES, LANES), F32)
    trow = lax.broadcasted_iota(jnp.int32, (SUBLANES, LANES), 0)
    for it in items:
        it["c"] = _cumsum_rows(tri[it["fwd"]], it["lf"])
    for it in items:
        it["q_a"] = jnp.where(lane_lo, it["q"], 0.0)
        it["q_b"] = jnp.where(lane_lo, 0.0, it["q"])
        it["ck"] = it["c"] - jnp.log(it["kk"])

    m = C // 2
    bit = m.bit_length() - 1
    while m >= SUBLANES:
        for it in items:
            fwd, c = it["fwd"], it["c"]
            qa_rows, qb_rows, k_rows = [], [], []
            for b in range(C // m):
                rows = slice(b * m, (b + 1) * m)
                anchor = (b // 2) * 2 * m + (m - 1 if fwd else m)
                r = _bcast_row(c, anchor, m)
                if (b % 2 == 1) == fwd:
                    e = jnp.exp(c[rows] - r)
                    qa_rows.append(it["q_a"][rows] * e)
                    qb_rows.append(it["q_b"][rows] * e)
                    k_rows += [zero_b] * (m // SUBLANES)
                else:
                    e = jnp.exp(r - c[rows])
                    qa_rows += [zero_b] * (m // SUBLANES)
                    qb_rows += [zero_b] * (m // SUBLANES)
                    k_rows.append(it["kk"][rows] * e)
            prod = _dot_nt(jnp.concatenate(qa_rows + qb_rows, axis=0).astype(BF16),
                           jnp.concatenate(k_rows, axis=0).astype(BF16))
            it["a"] = prod if "a" not in it else jnp.where(lvl[fwd] == bit, prod, it["a"])
        m //= 2
        bit -= 1

    for it in items:
        it["w"] = []
    for j in range(C // SUBLANES):
        sl = slice(j * SUBLANES, (j + 1) * SUBLANES)
        for it in items:
            cs, cks, qs = it["c"][sl], it["ck"][sl], it["q"][sl]
            w_s = []
            for s in range(SUBLANES):
                e = jnp.exp(cs - _bcast_row(cks, s, SUBLANES))
                keep = (trow >= s) if it["fwd"] else (trow <= s)
                w_s.append(jnp.where(keep, qs * e, 0.0))
            it["w"].append(jnp.concatenate(w_s, axis=1))
    for it in items:
        red = _dot(jnp.concatenate(it["w"], axis=0).astype(BF16), stk)
        it["a"] = jnp.where(lvl[it["fwd"]] == 0,
                            jnp.concatenate([red[:, 0:LANES], red[:, LANES:2 * LANES]], axis=0), it["a"])
    for it in items:
        off = _dot(it["a"].astype(BF16), it["iv"].astype(BF16))
        it["o"] = jnp.where(lane_lo, off[0:C], off[C:2 * C])
        it["edge"] = _bcast_row(it["c"], C - 1 if it["fwd"] else 0, C)
        it["q_dec"] = (it["q"] * jnp.exp(it["c"])).astype(BF16)
        k_dec = (it["kk"] * jnp.exp(it["edge"] - it["c"])).astype(BF16)
        it["upd"] = _dot_tn(it["iv"].astype(BF16), k_dec) * blk

    states = list(states)
    outs = []
    for it in items:
        st = states[it["sid"]]
        outs.append(it["o"] + _dot_nt(it["q_dec"], st.astype(BF16)))
        states[it["sid"]] = st * jnp.exp(it["edge"][0:1]) + it["upd"]
    return outs, states


def _hgrn_kernel(layer, nchunks, has_state, zq_ref, zff_ref, zfb_ref, zi_ref, zg_ref, lb_ref, g_ref,
                 trif_ref, trib_ref, lvlf_ref, lvlb_ref, stk_ref, ones_ref, blk_ref, *rest):
    if has_state:
        s0_ref, o_ref, of_scr, ob_scr = rest
    else:
        o_ref, st_ref, of_scr, ob_scr = rest[-4:]
    C = HG_CHUNK
    lane_lo = lax.broadcasted_iota(jnp.int32, (1, LANES), 1) < HG_DK

    lb_all = [lb_ref[j] for j in range(DEPTH)]
    mx = functools.reduce(jnp.maximum, lb_all)
    ex = [jnp.exp(v - mx) for v in lb_all]
    den = functools.reduce(lambda a, b: a + b, ex)
    sm = [v / den for v in ex]
    lbs = functools.reduce(lambda a, b: a + b, sm[:layer + 1]) - sm[0]
    lbf, lbb = lbs[0:1], lbs[1:2]

    def gates(zf, lbd):
        f = lbd + (1.0 - lbd) * jax.nn.sigmoid(zf)
        return jnp.log(jnp.maximum(f, GATE_FLOOR)), (1.0 - lbd) * jax.nn.sigmoid(-zf)

    def block_diag_t(s_ref_d):
        za = jnp.zeros((HG_DK, HG_DV), F32)
        top = jnp.concatenate([s_ref_d[0], za], axis=1)
        bot = jnp.concatenate([za, s_ref_d[1]], axis=1)
        return jnp.concatenate([top, bot], axis=0).T

    if has_state:
        st_f0 = block_diag_t(s0_ref[0])
        st_b0 = block_diag_t(s0_ref[1])
    else:
        st_f0 = jnp.zeros((LANES, LANES), F32)
        st_b0 = st_f0

    def body(j, carry):
        tri = {True: trif_ref[...], False: trib_ref[...]}
        lvl = {True: lvlf_ref[...], False: lvlb_ref[...]}
        items = []
        for u in range(HG_UNROLL):
            jf = j * HG_UNROLL + u
            for fwd in (True, False):
                rows = pl.ds(pl.multiple_of((jf if fwd else nchunks - 1 - jf) * C, C), C)
                lf, kk = gates((zff_ref if fwd else zfb_ref)[rows, :], lbf if fwd else lbb)
                items.append({"fwd": fwd, "sid": 0 if fwd else 1, "rows": rows, "lf": lf, "kk": kk,
                              "q": _silu(zq_ref[rows, :]) * (HG_DK ** -0.5), "iv": zi_ref[rows, :]})
        outs, states = _hgrn_chunks(items, carry, tri, lvl, stk_ref[...], blk_ref[...], lane_lo)
        for it, o in zip(items, outs):
            (of_scr if it["fwd"] else ob_scr)[it["rows"], :] = o
        return tuple(states)

    st_f, st_b = lax.fori_loop(0, nchunks // HG_UNROLL, body, (st_f0, st_b0))

    o = of_scr[...] + ob_scr[...]
    ms = _group_mean(o * o, ones_ref[...] * (1.0 / HG_DV))
    g = _twice(g_ref[layer:layer + 1, :])
    o_ref[...] = (o * lax.rsqrt(ms + EPS) * g * _silu(zg_ref[...])).astype(BF16)
    if not has_state:
        for d, st in enumerate((st_f, st_b)):
            s_kv = st.T
            st_ref[d, 0] = s_kv[0:HG_DK, 0:HG_DV]
            st_ref[d, 1] = s_kv[HG_DK:2 * HG_DK, HG_DV:2 * HG_DV]


def _hgrn_call(zh, lb, g, cst, layer, nb, s, state_in=None, prev_state_out=None):
    has_state = state_in is not None
    nchunks = s // HG_CHUNK
    npair = HG_HEADS // 2
    part = lambda k: pl.BlockSpec((s, LANES), lambda b, hp: (b, k * npair + hp))
    st_spec = pl.BlockSpec((None, None, 2, 2, HG_DK, HG_DV), lambda b, hp: (b, layer, 0, hp, 0, 0))
    in_specs = [part(0), part(1), part(2), part(3), part(4),
                pl.BlockSpec((DEPTH, 2, LANES), lambda b, hp: (0, 0, hp)),
                _const_spec((DEPTH, HG_DV)),
                _const_spec((HG_CHUNK, HG_CHUNK)), _const_spec((HG_CHUNK, HG_CHUNK)),
                _const_spec((2 * HG_CHUNK, HG_CHUNK)), _const_spec((2 * HG_CHUNK, HG_CHUNK)),
                _const_spec((SUBLANES * LANES, 2 * LANES)),
                _const_spec((LANES, LANES)), _const_spec((LANES, LANES))]
    args = [zh, zh, zh, zh, zh, lb, g, cst["tri_f"], cst["tri_b"], cst["lvl_f"], cst["lvl_b"], cst["stk"],
            cst["ones"], cst["blk"]]
    out_specs = [pl.BlockSpec((s, LANES), lambda b, hp: (b, hp))]
    out_shape = [jax.ShapeDtypeStruct((nb * s, HG_W), BF16)]
    aliases = {}
    if has_state:
        in_specs.append(st_spec)
        args.append(state_in)
    else:
        out_specs.append(st_spec)
        out_shape.append(jax.ShapeDtypeStruct((nb, DEPTH, 2, HG_HEADS, HG_DK, HG_DV), F32))
        if prev_state_out is not None:
            aliases[len(args)] = 1
            in_specs.append(_ANY)
            args.append(prev_state_out)
    return pl.pallas_call(
        functools.partial(_hgrn_kernel, layer, nchunks, has_state),
        grid=(nb, npair),
        in_specs=in_specs,
        out_specs=out_specs,
        out_shape=out_shape,
        input_output_aliases=aliases,
        scratch_shapes=[pltpu.VMEM((s, LANES), F32), pltpu.VMEM((s, LANES), F32)],
        compiler_params=_cparams(("arbitrary", "arbitrary")),
        name="hgrn_dec" if has_state else "hgrn_ctx",
    )(*args)


def _attend(q, ks, vs):
    ss = [_dot_nt(q, k) for k in ks]
    mx = functools.reduce(jnp.maximum, [jnp.max(s, axis=-1, keepdims=True) for s in ss])
    es = [jnp.exp(s - mx) for s in ss]
    den = functools.reduce(lambda a, b: a + b, [jnp.sum(e, axis=-1, keepdims=True) for e in es])
    o = functools.reduce(lambda a, b: a + b, [_dot(e.astype(BF16), v) for e, v in zip(es, vs)])
    return o / den


def _attn_kernel(has_cache, qm_ref, gq_ref, km_ref, vm_ref, gk_ref, gv_ref, *rest):
    if has_cache:
        kc_ref, vc_ref, gkc_ref, gvc_ref, o_ref = rest
    else:
        (o_ref,) = rest
    sq = qm_ref.shape[0]
    lane_lo = lax.broadcasted_iota(jnp.int32, (sq, LANES), 1) < MLA_V
    for p in range(MLA_HEADS // 2):
        vsl = slice(p * LANES, (p + 1) * LANES)
        vs = [vc_ref[:, vsl], vm_ref[:, vsl]] if has_cache else [vm_ref[:, vsl]]
        outs = []
        for hd in (2 * p, 2 * p + 1):
            sl = slice(hd * HEAD_PAD, (hd + 1) * HEAD_PAD)
            ks = [kc_ref[:, sl], km_ref[:, sl]] if has_cache else [km_ref[:, sl]]
            outs.append(_attend(qm_ref[:, sl], ks, vs))
        o_ref[:, vsl] = jnp.where(lane_lo, outs[0], outs[1]).astype(BF16)
    ks = [gkc_ref[...].astype(BF16), gk_ref[...]] if has_cache else [gk_ref[...]]
    vs = [gvc_ref[...].astype(BF16), gv_ref[...]] if has_cache else [gv_ref[...]]
    lo = lax.broadcasted_iota(jnp.int32, (GQA_ROWS, LANES), 1) < GQA_HD
    zero = jnp.zeros((GQA_ROWS, LANES), BF16)
    for r0 in range(0, sq, GQA_ROWS):
        res = []
        for p in range(GQA_W // LANES):
            qp = gq_ref[r0:r0 + GQA_ROWS, p * LANES:(p + 1) * LANES]
            o2 = _attend(jnp.concatenate([jnp.where(lo, qp, zero), jnp.where(lo, zero, qp)], axis=0), ks, vs)
            res.append(jnp.where(lo, o2[0:GQA_ROWS], o2[GQA_ROWS:2 * GQA_ROWS]))
        swapped = pltpu.roll(res[1], GQA_HD, 1)
        blocks = (jnp.where(lo, res[0], swapped), jnp.where(lo, res[2], res[0]), jnp.where(lo, swapped, res[2]))
        for p, blkv in enumerate(blocks):
            o_ref[r0:r0 + GQA_ROWS, MLA_W + p * LANES:MLA_W + (p + 1) * LANES] = blkv.astype(BF16)


def _attn_call(qm, gq, km, vm, gk, gv, cache, layer):
    has_cache = cache is not None
    t = qm.shape[0]
    widths = (MLA_QW, GQA_W, MLA_QW, MLA_W, LANES, LANES)
    if has_cache:
        nq = DEC_SEQ // TQ_DEC
        grid = (DEC_BATCH, nq)
        qmap = lambda b, i: (b * nq + i, 0)
        kmap = lambda b, i: (b, 0)
        once = pl.Buffered(1)
        in_specs = [pl.BlockSpec((TQ_DEC, widths[0]), qmap), pl.BlockSpec((TQ_DEC, widths[1]), qmap)]
        in_specs += [pl.BlockSpec((DEC_SEQ, wd), kmap, pipeline_mode=once) for wd in widths[2:]]
        in_specs += [pl.BlockSpec((PAST_LEN, wd), kmap, pipeline_mode=once) for wd in widths[2:4]]
        in_specs += [pl.BlockSpec((None, None, PAST_LEN, LANES), lambda b, i: (b, layer, 0, 0),
                                  pipeline_mode=once)] * 2
        args = [qm, gq, km, vm, gk, gv, *cache]
        out_spec = pl.BlockSpec((TQ_DEC, MLA_W + GQA_W), qmap)
        sem = ("arbitrary", "arbitrary")
    else:
        grid = (t // SEQ,)
        bmap = lambda b: (b, 0)
        in_specs = [pl.BlockSpec((SEQ, wd), bmap) for wd in widths]
        args = [qm, gq, km, vm, gk, gv]
        out_spec = pl.BlockSpec((SEQ, MLA_W + GQA_W), bmap)
        sem = ("arbitrary",)
    return pl.pallas_call(
        functools.partial(_attn_kernel, has_cache),
        grid=grid,
        in_specs=in_specs,
        out_specs=out_spec,
        out_shape=jax.ShapeDtypeStruct((t, MLA_W + GQA_W), BF16),
        compiler_params=_cparams(sem),
        name="attn_dec" if has_cache else "attn_ctx",
    )(*args)


def _ffn_kernel(layer, dec, hg_ref, at_ref, x_ref, mod_ref, w_out_ref, ln1g_ref, ln1b_ref, w_gu_ref, w_dn_ref,
                ln2g_ref, ln2b_ref, o_ref):
    lrow = slice(layer, layer + 1)
    mod = _mod_row(mod_ref, dec, DEC_SEQ // TM_FFN)
    g1 = mod[:, 2 * D_MODEL:3 * D_MODEL]
    sh2 = mod[:, 3 * D_MODEL:4 * D_MODEL]
    sc2 = mod[:, 4 * D_MODEL:5 * D_MODEL]
    g2 = mod[:, 5 * D_MODEL:6 * D_MODEL]
    y = _dot(hg_ref[...], w_out_ref[0:HG_W, :]) + _dot(at_ref[...], w_out_ref[HG_W:D_MODEL, :])
    x1 = _layernorm_rows(ALPHA * x_ref[...] + g1 * y, ln1g_ref[lrow, :], ln1b_ref[lrow, :])
    hf = (x1 * (1.0 + sc2) + sh2).astype(BF16)
    ch = D_FF // FF_CHUNKS
    f = None
    for c in range(FF_CHUNKS):
        gate = _dot(hf, w_gu_ref[:, c * ch:(c + 1) * ch])
        up = _dot(hf, w_gu_ref[:, D_FF + c * ch:D_FF + (c + 1) * ch])
        part = _dot((_silu(gate) * up).astype(BF16), w_dn_ref[c * ch:(c + 1) * ch, :])
        f = part if f is None else f + part
    o_ref[...] = _layernorm_rows(ALPHA * x1 + g2 * f, ln2g_ref[lrow, :], ln2b_ref[lrow, :])


def _ffn_call(hg, at, x, mod, w, prm, layer, dec):
    t = x.shape[0]
    row = lambda i: (i, 0)
    once = pl.Buffered(1)
    vec = pl.BlockSpec((DEPTH, D_MODEL), lambda i: (0, 0), pipeline_mode=once)
    return pl.pallas_call(
        functools.partial(_ffn_kernel, layer, dec),
        grid=(t // TM_FFN,),
        in_specs=[
            pl.BlockSpec((TM_FFN, HG_W), row),
            pl.BlockSpec((TM_FFN, MLA_W + GQA_W), row),
            pl.BlockSpec((TM_FFN, D_MODEL), row),
            _layer_spec((MOD_ROWS, N_ADA), layer, pipeline_mode=once),
            _layer_spec((D_MODEL, D_MODEL), layer, pipeline_mode=once),
            vec, vec,
            _layer_spec((D_MODEL, 2 * D_FF), layer, pipeline_mode=once),
            _layer_spec((D_FF, D_MODEL), layer, pipeline_mode=once),
            vec, vec,
        ],
        out_specs=pl.BlockSpec((TM_FFN, D_MODEL), row),
        out_shape=jax.ShapeDtypeStruct((t, D_MODEL), F32),
        compiler_params=_cparams(("arbitrary",)),
        name="ffn_dec" if dec else "ffn_ctx",
    )(hg, at, x, mod, w["w_out"], prm["ln1_g"], prm["ln1_b"], w["w_gu"], w["w_dn"], prm["ln2_g"], prm["ln2_b"])


def _rope_tables(width, rot_off, rot_dim):
    t = np.arange(DEC_SEQ)
    pos = (t // GRID_W, t % GRID_W)
    half = rot_dim // 4
    freqs = ROPE_THETA ** (-np.arange(half, dtype=np.float64) / half)
    c = np.ones((DEC_SEQ, LANES), np.float64)
    sa = np.zeros((DEC_SEQ, LANES), np.float64)
    sb = np.zeros((DEC_SEQ, LANES), np.float64)
    for blk in range(LANES // width):
        for axis in range(2):
            base = blk * width + rot_off + axis * 2 * half
            ang = pos[axis][:, None].astype(np.float64) * freqs[None, :]
            c[:, base:base + half] = np.cos(ang)
            c[:, base + half:base + 2 * half] = np.cos(ang)
            sa[:, base:base + half] = -np.sin(ang)
            sb[:, base + half:base + 2 * half] = np.sin(ang)
    return tuple(jnp.asarray(a, F32) for a in (c, sa, sb))


def _constants():
    C = HG_CHUNK
    t = np.arange(C)
    tri_f = (t[None, :] <= t[:, None]).astype(np.float32)
    tri_b = (t[None, :] >= t[:, None]).astype(np.float32)
    x = t[:, None] ^ t[None, :]
    top = np.where(x > 0, np.floor(np.log2(np.maximum(x, 1))).astype(np.int32), -1)
    same = (t[:, None] // SUBLANES) == (t[None, :] // SUBLANES)
    lvl_f = np.where(same, np.where(t[:, None] >= t[None, :], 0, -1),
                     np.where(t[:, None] > t[None, :], top, -1)).astype(np.int32)
    lvl_b = np.where(same, np.where(t[:, None] <= t[None, :], 0, -1),
                     np.where(t[:, None] < t[None, :], top, -1)).astype(np.int32)
    grp = np.arange(LANES) // HG_DK
    blk = (grp[:, None] == grp[None, :]).astype(np.float32)
    stk = np.zeros((SUBLANES, LANES, 2, C), np.float32)
    for s in range(SUBLANES):
        for hb in range(2):
            stk[s, hb * HG_DK:(hb + 1) * HG_DK, hb, s::SUBLANES] = 1.0
    stk = stk.reshape(SUBLANES * LANES, 2 * C)
    place = np.zeros((MLA_ROPE, MLA_QW), np.float32)
    for hd in range(MLA_HEADS):
        place[np.arange(MLA_ROPE), hd * HEAD_PAD + MLA_NOPE + np.arange(MLA_ROPE)] = 1.0
    mla_c, mla_sa, mla_sb = _rope_tables(HEAD_PAD, MLA_NOPE, MLA_ROPE)
    gqa_c, gqa_sa, gqa_sb = _rope_tables(GQA_HD, 0, GQA_HD)
    return {
        "tri_f": jnp.asarray(tri_f, BF16), "tri_b": jnp.asarray(tri_b, BF16),
        "lvl_f": jnp.asarray(np.concatenate([lvl_f, lvl_f], axis=0)),
        "lvl_b": jnp.asarray(np.concatenate([lvl_b, lvl_b], axis=0)),
        "stk": jnp.asarray(stk, BF16), "ones": jnp.asarray(blk, BF16), "ones64": jnp.asarray(blk / GQA_HD, BF16),
        "blk": jnp.asarray(blk, F32), "kpe_place": jnp.asarray(place, BF16),
        "mla_c": mla_c, "mla_sa": mla_sa, "mla_sb": mla_sb, "gqa_c": gqa_c, "gqa_sa": gqa_sa, "gqa_sb": gqa_sb,
    }


def kernel(x_prompt, x_sample, cache_mla_ckv, cache_mla_kpe, cache_gqa_k, cache_gqa_v, state_hgrn, c, c_ctx, w_ada,
           b_ada, w_in, hg_lb, hg_norm, mla_q_norm, mla_w_uq, mla_kv_norm, mla_w_ukv, gqa_q_norm, gqa_k_norm, w_out,
           ln1_g, ln1_b, w_ffn_in, w_ffn_out, ln2_g, ln2_b):
    cst = _constants()
    prm = {"mla_q_norm": mla_q_norm, "mla_kv_norm": mla_kv_norm, "gqa_q_norm": gqa_q_norm, "gqa_k_norm": gqa_k_norm,
           "ln1_g": ln1_g, "ln1_b": ln1_b, "ln2_g": ln2_g, "ln2_b": ln2_b}
    w = {"w_in": _prep_in_call(w_in)}
    w["w_uq"], w["w_uk"], w["w_uv"] = _prep_mla_call(mla_w_uq, mla_w_ukv)
    w["w_out"], w["w_gu"], w["w_dn"] = _prep_cast_call(w_out, w_ffn_in, w_ffn_out)

    cvec = jnp.concatenate([c_ctx[None, :], c, jnp.zeros((MOD_ROWS - 1 - DEC_BATCH, D_MODEL), F32)], axis=0)
    mod = _ada_call(cvec, w_ada, b_ada)
    cache_gk = cache_gqa_k.reshape(DEC_BATCH, DEPTH, PAST_LEN, LANES)
    cache_gv = cache_gqa_v.reshape(DEC_BATCH, DEPTH, PAST_LEN, LANES)

    x_ctx = x_prompt.reshape(BATCH * SEQ, D_MODEL)
    x_dec = x_sample.reshape(DEC_BATCH * DEC_SEQ, D_MODEL)
    states, hg_state = None, None
    for l in range(DEPTH):
        zh, qm, km, vm, gq, gk, gv, *states = _inproj_call(x_ctx, mod, w, prm, cst, l, rope=False,
                                                          prev_states=states)
        hg_out, hg_state = _hgrn_call(zh, hg_lb, hg_norm, cst, l, BATCH, SEQ, prev_state_out=hg_state)
        at = _attn_call(qm, gq, km, vm, gk, gv, None, l)
        x_ctx = _ffn_call(hg_out, at, x_ctx, mod, w, prm, l, dec=False)

        zh, qm, km, vm, gq, gk, gv = _inproj_call(x_dec, mod, w, prm, cst, l, rope=True)
        hg_out = _hgrn_call(zh, hg_lb, hg_norm, cst, l, DEC_BATCH, DEC_SEQ, state_in=state_hgrn)[0]
        kc, vc = _cachekv_call(cache_mla_ckv, cache_mla_kpe, w, cst, l)
        at = _attn_call(qm, gq, km, vm, gk, gv, (kc, vc, cache_gk, cache_gv), l)
        x_dec = _ffn_call(hg_out, at, x_dec, mod, w, prm, l, dec=True)

    ckv, kpe, gk32, gv32 = states
    kv_shape = (BATCH, DEPTH, SEQ, GQA_KV_HEADS, GQA_HD)
    return (x_ctx.reshape(BATCH, SEQ, D_MODEL), x_dec.reshape(DEC_BATCH, DEC_SEQ, D_MODEL),
            ckv, kpe, gk32.reshape(kv_shape), gv32.reshape(kv_shape), hg_state)
```

```python
import functools

import numpy as np
import jax
import jax.numpy as jnp
from jax import lax
from jax.experimental import pallas as pl
from jax.experimental.pallas import tpu as pltpu

F32 = jnp.float32
BF16 = jnp.bfloat16

D_MODEL = 1024
BATCH = 32
SEQ = 256
DEPTH = 2
DEC_BATCH = 2
DEC_SEQ = 2048
PAST_LEN = 256
GRID_W = 64
ROPE_THETA = 10000.0
EPS = 1e-6
GATE_FLOOR = 1e-30
HG_HEADS = 4
HG_DK = 64
HG_DV = 64
MLA_HEADS = 6
MLA_NOPE = 64
MLA_ROPE = 32
MLA_V = 64
MLA_Q_RANK = 256
MLA_KV_RANK = 128
GQA_HEADS = 6
GQA_KV_HEADS = 2
GQA_HD = 64
HG_W = HG_HEADS * HG_DV
MLA_W = MLA_HEADS * MLA_V
GQA_W = GQA_HEADS * GQA_HD
D_FF = 2816
MLA_SCALE = (MLA_NOPE + MLA_ROPE) ** -0.5
GQA_SCALE = GQA_HD ** -0.5
ALPHA = (2 * DEPTH) ** 0.25

LANES = 128
SUBLANES = 8
HEAD_PAD = 128
MLA_QW = MLA_HEADS * HEAD_PAD
N_ADA = 6 * D_MODEL
MOD_ROWS = 8

N_IN = 5 * HG_HEADS * HG_DK + MLA_Q_RANK + MLA_KV_RANK + MLA_ROPE + GQA_W + 2 * GQA_KV_HEADS * GQA_HD
SRC_KPE = 5 * HG_HEADS * HG_DK + MLA_Q_RANK + MLA_KV_RANK
SRC_GQ = SRC_KPE + MLA_ROPE
SRC_GK = SRC_GQ + GQA_W
SRC_GV = SRC_GK + GQA_KV_HEADS * GQA_HD
HG_COLS = 5 * HG_HEADS * HG_DK
OFF_MQ = HG_COLS
OFF_MKV = OFF_MQ + MLA_Q_RANK
OFF_GQ = OFF_MKV + MLA_KV_RANK
OFF_GK = OFF_GQ + GQA_W
OFF_GV = OFF_GK + GQA_KV_HEADS * GQA_HD
OFF_KPE_PAD = OFF_GV + GQA_KV_HEADS * GQA_HD
OFF_KPE = OFF_KPE_PAD + HEAD_PAD
N_Z = OFF_KPE + MLA_ROPE

TM_IN = 512
TM_FFN = 512
TQ_DEC = 512
GQA_ROWS = 256
ATT_GROUP_CTX = 9
ATT_GROUP_DEC = 2
HG_CHUNK = 128
HG_UNROLL = 2
FF_CHUNKS = 2
PREP_STEPS = 4
VMEM_LIMIT = 56 * 1024 * 1024


def _cparams(sem):
    return pltpu.CompilerParams(dimension_semantics=sem, vmem_limit_bytes=VMEM_LIMIT)


def _const_spec(shape):
    nd = len(shape)
    return pl.BlockSpec(shape, lambda *_: (0,) * nd)


def _layer_spec(shape, layer, **kw):
    nd = len(shape)
    return pl.BlockSpec((None,) + tuple(shape), lambda *_: (layer,) + (0,) * nd, **kw)


_ANY = pl.BlockSpec(memory_space=pl.ANY)


def _silu(x):
    return x * jax.nn.sigmoid(x)


def _dot(a, b):
    return jnp.dot(a, b, preferred_element_type=F32)


def _dot_nt(a, b):
    return lax.dot_general(a, b, (((1,), (1,)), ((), ())), preferred_element_type=F32)


def _dot_tn(a, b):
    return lax.dot_general(a, b, (((0,), (0,)), ((), ())), preferred_element_type=F32)


def _rms_rows(x, g):
    y = x * lax.rsqrt(jnp.mean(x * x, axis=-1, keepdims=True) + EPS)
    return y * g


def _layernorm_rows(x, g, b):
    xc = x - jnp.mean(x, axis=-1, keepdims=True)
    var = jnp.mean(xc * xc, axis=-1, keepdims=True)
    return xc * lax.rsqrt(var + EPS) * g + b


def _group_mean(x, ones_blk):
    return _dot(x.astype(BF16), ones_blk)


def _twice(g):
    return jnp.concatenate([g, g], axis=1)


_GQ_HEAD_ORDER = (0, 3, 1, 4, 2, 5)


def _prep_in_kernel(w_ref, o_ref):
    x = w_ref[...]
    rows = x.shape[0]
    kr = x[:, SRC_KPE:SRC_GQ]
    gq = [x[:, SRC_GQ + h * GQA_HD:SRC_GQ + (h + 1) * GQA_HD] for h in _GQ_HEAD_ORDER]
    pieces = [x[:, 0:SRC_KPE]] + gq + [
        x[:, SRC_GK:SRC_GV], x[:, SRC_GV:N_IN],
        jnp.zeros((rows, MLA_NOPE), F32), kr, jnp.zeros((rows, HEAD_PAD - MLA_NOPE - MLA_ROPE), F32), kr]
    o_ref[...] = jnp.concatenate(pieces, axis=1).astype(BF16)


def _prep_in_call(w_in):
    rb = 128
    return pl.pallas_call(
        _prep_in_kernel,
        grid=(DEPTH, D_MODEL // rb),
        in_specs=[pl.BlockSpec((None, rb, N_IN), lambda l, j: (l, j, 0))],
        out_specs=pl.BlockSpec((None, rb, N_Z), lambda l, j: (l, j, 0)),
        out_shape=jax.ShapeDtypeStruct((DEPTH, D_MODEL, N_Z), BF16),
        compiler_params=_cparams(("arbitrary", "arbitrary")),
        name="prep_in",
    )(w_in)


def _prep_mla_kernel(uq_ref, ukv_ref, oq_ref, ok_ref, ov_ref):
    uq = uq_ref[...]
    dq = MLA_NOPE + MLA_ROPE
    zq = jnp.zeros((MLA_Q_RANK, HEAD_PAD - dq), F32)
    oq_ref[...] = jnp.concatenate(
        [p for h in range(MLA_HEADS) for p in (uq[:, h * dq:(h + 1) * dq], zq)], axis=1).astype(BF16)
    ukv = ukv_ref[...]
    lane = lax.broadcasted_iota(jnp.int32, ukv.shape, 1)
    ok_ref[...] = jnp.where((lane & (HEAD_PAD - 1)) < MLA_NOPE, ukv, 0.0).astype(BF16)
    ov_ref[...] = jnp.concatenate(
        [ukv[:, h * HEAD_PAD + MLA_NOPE:(h + 1) * HEAD_PAD] for h in range(MLA_HEADS)], axis=1).astype(BF16)


def _prep_mla_call(w_uq, w_ukv):
    lay = lambda n: pl.BlockSpec((None, n[0], n[1]), lambda l: (l, 0, 0))
    shapes = [(MLA_Q_RANK, MLA_QW), (MLA_KV_RANK, MLA_QW), (MLA_KV_RANK, MLA_W)]
    return pl.pallas_call(
        _prep_mla_kernel,
        grid=(DEPTH,),
        in_specs=[lay((MLA_Q_RANK, MLA_HEADS * (MLA_NOPE + MLA_ROPE))), lay((MLA_KV_RANK, MLA_QW))],
        out_specs=[lay(s) for s in shapes],
        out_shape=[jax.ShapeDtypeStruct((DEPTH,) + s, BF16) for s in shapes],
        compiler_params=_cparams(("arbitrary",)),
        name="prep_mla",
    )(w_uq, w_ukv)


def _prep_cast_kernel(a_ref, b_ref, c_ref, oa_ref, ob_ref, oc_ref):
    oa_ref[...] = a_ref[...].astype(BF16)
    ob_ref[...] = b_ref[...].astype(BF16)
    oc_ref[...] = c_ref[...].astype(BF16)


def _prep_cast_call(w_out, w_ffn_in, w_ffn_out):
    ws = (w_out, w_ffn_in, w_ffn_out)
    specs = [pl.BlockSpec((None, w.shape[1] // PREP_STEPS, w.shape[2]), lambda l, j: (l, j, 0)) for w in ws]
    return pl.pallas_call(
        _prep_cast_kernel,
        grid=(DEPTH, PREP_STEPS),
        in_specs=specs,
        out_specs=specs,
        out_shape=[jax.ShapeDtypeStruct(w.shape, BF16) for w in ws],
        compiler_params=_cparams(("arbitrary", "arbitrary")),
        name="prep_cast",
    )(*ws)


def _ada_kernel(c_ref, w_ref, b_ref, o_ref):
    s = _silu(c_ref[...]).astype(BF16)
    o_ref[...] = _dot(s, w_ref[...].astype(BF16)) + b_ref[pl.ds(pl.program_id(0), 1), :]


def _ada_call(cvec, w_ada, b_ada):
    tn = 1536
    return pl.pallas_call(
        _ada_kernel,
        grid=(DEPTH, N_ADA // tn),
        in_specs=[
            pl.BlockSpec((MOD_ROWS, D_MODEL), lambda l, j: (0, 0)),
            pl.BlockSpec((None, D_MODEL, tn), lambda l, j: (l, 0, j)),
            pl.BlockSpec((DEPTH, tn), lambda l, j: (0, j)),
        ],
        out_specs=pl.BlockSpec((None, MOD_ROWS, tn), lambda l, j: (l, 0, j)),
        out_shape=jax.ShapeDtypeStruct((DEPTH, MOD_ROWS, N_ADA), F32),
        compiler_params=_cparams(("arbitrary", "arbitrary")),
        name="ada_mod",
    )(cvec, w_ada, b_ada)


def _mod_row(mod_ref, dec, tiles_per_seq):
    if not dec:
        return mod_ref[0:1, :]
    return mod_ref[pl.ds(1 + pl.program_id(0) // tiles_per_seq, 1), :]


def _rope_block(x, c, sa, sb, half):
    return x * c + pltpu.roll(x, LANES - half, 1) * sa + pltpu.roll(x, half, 1) * sb


def _inproj_kernel(layer, rope, x_ref, mod_ref, w_in_ref, w_uq_ref, w_uk_ref, w_uv_ref, qn_ref, kvn_ref, gqn_ref,
                   gkn_ref, ones_ref, *rest):
    if rope:
        (mc_ref, msa_ref, msb_ref, gc_ref, gsa_ref, gsb_ref,
         zhf_ref, zhb_ref, qm_ref, km_ref, vm_ref, gq_ref, gk_ref, gv_ref) = rest
    else:
        rest = rest[-12:]
        (zhf_ref, zhb_ref, qm_ref, km_ref, vm_ref, gq_ref, gk_ref, gv_ref,
         ckv_ref, kpe_ref, gk32_ref, gv32_ref) = rest
    lrow = slice(layer, layer + 1)
    mod = _mod_row(mod_ref, rope, DEC_SEQ // TM_IN)
    sh1 = mod[:, 0:D_MODEL]
    sc1 = mod[:, D_MODEL:2 * D_MODEL]
    h = (x_ref[...] * (1.0 + sc1) + sh1).astype(BF16)
    z = _dot(h, w_in_ref[...])
    gw = HG_HEADS * HG_DK
    zhf_ref[...] = z[:, gw:3 * gw]
    zhb_ref[:, 0:gw] = z[:, 0:gw].astype(BF16)
    zhb_ref[:, gw:3 * gw] = z[:, 3 * gw:HG_COLS].astype(BF16)
    mq = z[:, OFF_MQ:OFF_MKV]
    mkv = z[:, OFF_MKV:OFF_GQ]
    gq = z[:, OFF_GQ:OFF_GK]
    gk = z[:, OFF_GK:OFF_GV]
    gv = z[:, OFF_GV:OFF_KPE_PAD]
    kpe_pad = z[:, OFF_KPE_PAD:OFF_KPE]
    ones = ones_ref[...]
    gqn = _twice(gqn_ref[lrow, :])
    gkn_g = _twice(gkn_ref[lrow, :])

    q = _dot(_rms_rows(mq, qn_ref[lrow, :]).astype(BF16), w_uq_ref[...])
    ckv = _rms_rows(mkv, kvn_ref[lrow, :])
    ckv_b = ckv.astype(BF16)
    kn = _dot(ckv_b, w_uk_ref[...])
    vm_ref[...] = _dot(ckv_b, w_uv_ref[...]).astype(BF16)
    if rope:
        mc, msa, msb = mc_ref[...], msa_ref[...], msb_ref[...]
        kpe_pad = _rope_block(kpe_pad, mc, msa, msb, MLA_ROPE // 4)
    for hd in range(MLA_HEADS):
        sl = slice(hd * HEAD_PAD, (hd + 1) * HEAD_PAD)
        qh = q[:, sl]
        if rope:
            qh = _rope_block(qh, mc, msa, msb, MLA_ROPE // 4)
        qm_ref[:, sl] = (qh * MLA_SCALE).astype(BF16)
        km_ref[:, sl] = (kn[:, sl] + kpe_pad).astype(BF16)

    gkn = gk * lax.rsqrt(_group_mean(gk * gk, ones) + EPS) * gkn_g
    for p in range(GQA_W // LANES):
        sl = slice(p * LANES, (p + 1) * LANES)
        gqp = gq[:, sl]
        gqp = gqp * lax.rsqrt(_group_mean(gqp * gqp, ones) + EPS) * gqn
        if rope:
            gqp = _rope_block(gqp, gc_ref[...], gsa_ref[...], gsb_ref[...], GQA_HD // 4)
        gq_ref[:, sl] = (gqp * GQA_SCALE).astype(BF16)
    if rope:
        gk_ref[...] = _rope_block(gkn, gc_ref[...], gsa_ref[...], gsb_ref[...], GQA_HD // 4).astype(BF16)
    else:
        gk_ref[...] = gkn.astype(BF16)
        kpe = z[:, OFF_KPE:N_Z]
        for b in range(TM_IN // SEQ):
            rows = slice(b * SEQ, (b + 1) * SEQ)
            ckv_ref[b] = ckv[rows]
            kpe_ref[b] = kpe[rows]
            gk32_ref[b] = gkn[rows]
            gv32_ref[b] = gv[rows]
    gv_ref[...] = gv.astype(BF16)


def _inproj_call(x, mod, w, prm, cst, layer, rope, prev_states=None):
    t = x.shape[0]
    nt = t // TM_IN
    row = lambda i: (i, 0)
    in_specs = [
        pl.BlockSpec((TM_IN, D_MODEL), row),
        _layer_spec((MOD_ROWS, N_ADA), layer),
        _layer_spec((D_MODEL, N_Z), layer),
        _layer_spec((MLA_Q_RANK, MLA_QW), layer),
        _layer_spec((MLA_KV_RANK, MLA_QW), layer),
        _layer_spec((MLA_KV_RANK, MLA_W), layer),
        _const_spec((DEPTH, MLA_Q_RANK)),
        _const_spec((DEPTH, MLA_KV_RANK)),
        _const_spec((DEPTH, GQA_HD)),
        _const_spec((DEPTH, GQA_HD)),
        _const_spec((LANES, LANES)),
    ]
    args = [x, mod, w["w_in"], w["w_uq"], w["w_uk"], w["w_uv"], prm["mla_q_norm"], prm["mla_kv_norm"],
            prm["gqa_q_norm"], prm["gqa_k_norm"], cst["ones64"]]
    outs = [(2 * HG_HEADS * HG_DK, F32), (3 * HG_HEADS * HG_DK, BF16), (MLA_QW, BF16), (MLA_QW, BF16), (MLA_W, BF16),
            (GQA_W, BF16), (LANES, BF16), (LANES, BF16)]
    out_specs = [pl.BlockSpec((TM_IN, wd), row) for wd, _ in outs]
    out_shape = [jax.ShapeDtypeStruct((t, wd), dt) for wd, dt in outs]
    aliases = {}
    if rope:
        tab = lambda i: (i % (DEC_SEQ // TM_IN), 0)
        in_specs += [pl.BlockSpec((TM_IN, LANES), tab)] * 6
        args += [cst["mla_c"], cst["mla_sa"], cst["mla_sb"], cst["gqa_c"], cst["gqa_sa"], cst["gqa_sb"]]
    else:
        per = TM_IN // SEQ
        for wd in (MLA_KV_RANK, MLA_ROPE, LANES, LANES):
            out_specs.append(pl.BlockSpec((per, None, SEQ, wd), lambda i: (i, layer, 0, 0)))
            out_shape.append(jax.ShapeDtypeStruct((BATCH, DEPTH, SEQ, wd), F32))
        if prev_states is not None:
            for k, arr in enumerate(prev_states):
                aliases[len(args)] = len(outs) + k
                in_specs.append(_ANY)
                args.append(arr)
    return pl.pallas_call(
        functools.partial(_inproj_kernel, layer, rope),
        grid=(nt,),
        in_specs=in_specs,
        out_specs=out_specs,
        out_shape=out_shape,
        input_output_aliases=aliases,
        compiler_params=_cparams(("arbitrary",)),
        name="inproj_dec" if rope else "inproj_ctx",
    )(*args)


def _cachekv_kernel(ckv_ref, kpe_ref, w_uk_ref, w_uv_ref, place_ref, k_ref, v_ref):
    ckv_b = ckv_ref[...].astype(BF16)
    k_ref[...] = (_dot(ckv_b, w_uk_ref[...]) + _dot(kpe_ref[...].astype(BF16), place_ref[...])).astype(BF16)
    v_ref[...] = _dot(ckv_b, w_uv_ref[...]).astype(BF16)


def _cachekv_call(cache_ckv, cache_kpe, w, cst, layer):
    t = DEC_BATCH * PAST_LEN
    cache = lambda wd: pl.BlockSpec((None, None, PAST_LEN, wd), lambda b: (b, layer, 0, 0))
    return pl.pallas_call(
        _cachekv_kernel,
        grid=(DEC_BATCH,),
        in_specs=[cache(MLA_KV_RANK), cache(MLA_ROPE),
                  _layer_spec((MLA_KV_RANK, MLA_QW), layer), _layer_spec((MLA_KV_RANK, MLA_W), layer),
                  _const_spec((MLA_ROPE, MLA_QW))],
        out_specs=[pl.BlockSpec((PAST_LEN, MLA_QW), lambda b: (b, 0)),
                   pl.BlockSpec((PAST_LEN, MLA_W), lambda b: (b, 0))],
        out_shape=[jax.ShapeDtypeStruct((t, MLA_QW), BF16), jax.ShapeDtypeStruct((t, MLA_W), BF16)],
        compiler_params=_cparams(("arbitrary",)),
        name="cache_kv",
    )(cache_ckv, cache_kpe, w["w_uk"], w["w_uv"], cst["kpe_place"])


def _bcast_row(x, r, rows):
    return jnp.broadcast_to(x[r:r + 1, :], (rows, x.shape[1]))


def _cumsum_rows(tri, x):
    hi = x.astype(BF16)
    r1 = x - hi.astype(F32)
    mid = r1.astype(BF16)
    lo = (r1 - mid.astype(F32)).astype(BF16)
    return _dot(tri, hi) + _dot(tri, mid) + _dot(tri, lo)


def _hgrn_chunks(items, states, tri, lvl, stk, blk, lane_lo):
    C = HG_CHUNK
    zero_b = jnp.zeros((SUBLANES, LANES), F32)
    trow = lax.broadcasted_iota(jnp.int32, (SUBLANES, LANES), 0)
    for it in items:
        it["c"] = _cumsum_rows(tri[it["fwd"]], it["lf"])
    for it in items:
        it["q_a"] = jnp.where(lane_lo, it["q"], 0.0)
        it["q_b"] = jnp.where(lane_lo, 0.0, it["q"])
        it["ck"] = it["c"] - jnp.log(it["kk"])

    m = C // 2
    bit = m.bit_length() - 1
    while m >= SUBLANES:
        for it in items:
            fwd, c = it["fwd"], it["c"]
            qa_rows, qb_rows, k_rows = [], [], []
            for b in range(C // m):
                rows = slice(b * m, (b + 1) * m)
                anchor = (b // 2) * 2 * m + (m - 1 if fwd else m)
                r = _bcast_row(c, anchor, m)
                if (b % 2 == 1) == fwd:
                    e = jnp.exp(c[rows] - r)
                    qa_rows.append(it["q_a"][rows] * e)
                    qb_rows.append(it["q_b"][rows] * e)
                    k_rows += [zero_b] * (m // SUBLANES)
                else:
                    e = jnp.exp(r - c[rows])
                    qa_rows += [zero_b] * (m // SUBLANES)
                    qb_rows += [zero_b] * (m // SUBLANES)
                    k_rows.append(it["kk"][rows] * e)
            prod = _dot_nt(jnp.concatenate(qa_rows + qb_rows, axis=0).astype(BF16),
                           jnp.concatenate(k_rows, axis=0).astype(BF16))
            it["a"] = prod if "a" not in it else jnp.where(lvl[fwd] == bit, prod, it["a"])
        m //= 2
        bit -= 1

    for it in items:
        it["w"] = []
    for j in range(C // SUBLANES):
        sl = slice(j * SUBLANES, (j + 1) * SUBLANES)
        for it in items:
            cs, cks, qs = it["c"][sl], it["ck"][sl], it["q"][sl]
            w_s = []
            for s in range(SUBLANES):
                e = jnp.exp(cs - _bcast_row(cks, s, SUBLANES))
                keep = (trow >= s) if it["fwd"] else (trow <= s)
                w_s.append(jnp.where(keep, qs * e, 0.0))
            it["w"].append(jnp.concatenate(w_s, axis=1))
    for it in items:
        red = _dot(jnp.concatenate(it["w"], axis=0).astype(BF16), stk)
        it["a"] = jnp.where(lvl[it["fwd"]] == 0,
                            jnp.concatenate([red[:, 0:LANES], red[:, LANES:2 * LANES]], axis=0), it["a"])
    for it in items:
        off = _dot(it["a"].astype(BF16), it["iv"].astype(BF16))
        it["o"] = jnp.where(lane_lo, off[0:C], off[C:2 * C])
        it["edge"] = _bcast_row(it["c"], C - 1 if it["fwd"] else 0, C)
        it["q_dec"] = (it["q"] * jnp.exp(it["c"])).astype(BF16)
        k_dec = (it["kk"] * jnp.exp(it["edge"] - it["c"])).astype(BF16)
        it["upd"] = _dot_tn(it["iv"].astype(BF16), k_dec) * blk

    states = list(states)
    outs = []
    for it in items:
        st = states[it["sid"]]
        outs.append(it["o"] + _dot_nt(it["q_dec"], st.astype(BF16)))
        states[it["sid"]] = st * jnp.exp(it["edge"][0:1]) + it["upd"]
    return outs, states


def _hgrn_kernel(layer, nchunks, has_state, zq_ref, zff_ref, zfb_ref, zi_ref, zg_ref, lb_ref, g_ref,
                 trif_ref, trib_ref, lvlf_ref, lvlb_ref, stk_ref, ones_ref, blk_ref, *rest):
    if has_state:
        s0_ref, o_ref, of_scr, ob_scr = rest
    else:
        o_ref, st_ref, of_scr, ob_scr = rest[-4:]
    C = HG_CHUNK
    lane_lo = lax.broadcasted_iota(jnp.int32, (1, LANES), 1) < HG_DK

    lb_all = [lb_ref[j] for j in range(DEPTH)]
    mx = functools.reduce(jnp.maximum, lb_all)
    ex = [jnp.exp(v - mx) for v in lb_all]
    den = functools.reduce(lambda a, b: a + b, ex)
    sm = [v / den for v in ex]
    lbs = functools.reduce(lambda a, b: a + b, sm[:layer + 1]) - sm[0]
    lbf, lbb = lbs[0:1], lbs[1:2]

    def gates(zf, lbd):
        f = lbd + (1.0 - lbd) * jax.nn.sigmoid(zf)
        return jnp.log(jnp.maximum(f, GATE_FLOOR)), (1.0 - lbd) * jax.nn.sigmoid(-zf)

    def block_diag_t(s_ref_d):
        za = jnp.zeros((HG_DK, HG_DV), F32)
        top = jnp.concatenate([s_ref_d[0], za], axis=1)
        bot = jnp.concatenate([za, s_ref_d[1]], axis=1)
        return jnp.concatenate([top, bot], axis=0).T

    if has_state:
        st_f0 = block_diag_t(s0_ref[0])
        st_b0 = block_diag_t(s0_ref[1])
    else:
        st_f0 = jnp.zeros((LANES, LANES), F32)
        st_b0 = st_f0

    def body(j, carry):
        tri = {True: trif_ref[...], False: trib_ref[...]}
        lvl = {True: lvlf_ref[...], False: lvlb_ref[...]}
        items = []
        for u in range(HG_UNROLL):
            jf = j * HG_UNROLL + u
            for fwd in (True, False):
                rows = pl.ds(pl.multiple_of((jf if fwd else nchunks - 1 - jf) * C, C), C)
                lf, kk = gates((zff_ref if fwd else zfb_ref)[rows, :], lbf if fwd else lbb)
                items.append({"fwd": fwd, "sid": 0 if fwd else 1, "rows": rows, "lf": lf, "kk": kk,
                              "q": _silu(zq_ref[rows, :].astype(F32)) * (HG_DK ** -0.5), "iv": zi_ref[rows, :]})
        outs, states = _hgrn_chunks(items, carry, tri, lvl, stk_ref[...], blk_ref[...], lane_lo)
        for it, o in zip(items, outs):
            (of_scr if it["fwd"] else ob_scr)[it["rows"], :] = o
        return tuple(states)

    st_f, st_b = lax.fori_loop(0, nchunks // HG_UNROLL, body, (st_f0, st_b0))

    o = of_scr[...] + ob_scr[...]
    ms = _group_mean(o * o, ones_ref[...] * (1.0 / HG_DV))
    g = _twice(g_ref[layer:layer + 1, :])
    o_ref[...] = (o * lax.rsqrt(ms + EPS) * g * _silu(zg_ref[...].astype(F32))).astype(BF16)
    if not has_state:
        for d, st in enumerate((st_f, st_b)):
            s_kv = st.T
            st_ref[d, 0] = s_kv[0:HG_DK, 0:HG_DV]
            st_ref[d, 1] = s_kv[HG_DK:2 * HG_DK, HG_DV:2 * HG_DV]


def _hgrn_call(zhf, zhb, lb, g, cst, layer, nb, s, state_in=None, prev_state_out=None):
    has_state = state_in is not None
    nchunks = s // HG_CHUNK
    npair = HG_HEADS // 2
    part = lambda k: pl.BlockSpec((s, LANES), lambda b, hp: (b, k * npair + hp))
    st_spec = pl.BlockSpec((None, None, 2, 2, HG_DK, HG_DV), lambda b, hp: (b, layer, 0, hp, 0, 0))
    in_specs = [part(0), part(0), part(1), part(1), part(2),
                pl.BlockSpec((DEPTH, 2, LANES), lambda b, hp: (0, 0, hp)),
                _const_spec((DEPTH, HG_DV)),
                _const_spec((HG_CHUNK, HG_CHUNK)), _const_spec((HG_CHUNK, HG_CHUNK)),
                _const_spec((2 * HG_CHUNK, HG_CHUNK)), _const_spec((2 * HG_CHUNK, HG_CHUNK)),
                _const_spec((SUBLANES * LANES, 2 * LANES)),
                _const_spec((LANES, LANES)), _const_spec((LANES, LANES))]
    args = [zhb, zhf, zhf, zhb, zhb, lb, g, cst["tri_f"], cst["tri_b"], cst["lvl_f"], cst["lvl_b"], cst["stk"],
            cst["ones"], cst["blk"]]
    out_specs = [pl.BlockSpec((s, LANES), lambda b, hp: (b, hp))]
    out_shape = [jax.ShapeDtypeStruct((nb * s, HG_W), BF16)]
    aliases = {}
    if has_state:
        in_specs.append(st_spec)
        args.append(state_in)
    else:
        out_specs.append(st_spec)
        out_shape.append(jax.ShapeDtypeStruct((nb, DEPTH, 2, HG_HEADS, HG_DK, HG_DV), F32))
        if prev_state_out is not None:
            aliases[len(args)] = 1
            in_specs.append(_ANY)
            args.append(prev_state_out)
    return pl.pallas_call(
        functools.partial(_hgrn_kernel, layer, nchunks, has_state),
        grid=(nb, npair),
        in_specs=in_specs,
        out_specs=out_specs,
        out_shape=out_shape,
        input_output_aliases=aliases,
        scratch_shapes=[pltpu.VMEM((s, LANES), F32), pltpu.VMEM((s, LANES), F32)],
        compiler_params=_cparams(("arbitrary", "arbitrary")),
        name="hgrn_dec" if has_state else "hgrn_ctx",
    )(*args)


def _attend(jobs):
    add = lambda a, b: a + b
    ss = [[_dot_nt(q, k) for k in ks] for q, ks, _ in jobs]
    mxs = [functools.reduce(jnp.maximum, [jnp.max(s, axis=-1, keepdims=True) for s in sj]) for sj in ss]
    es = [[jnp.exp(s - mx) for s in sj] for sj, mx in zip(ss, mxs)]
    dens = [functools.reduce(add, [jnp.sum(e, axis=-1, keepdims=True) for e in ej]) for ej in es]
    outs = [functools.reduce(add, [_dot(e.astype(BF16), v) for e, v in zip(ej, vs)])
            for ej, (_, _, vs) in zip(es, jobs)]
    return [o / d for o, d in zip(outs, dens)]


def _attn_kernel(has_cache, qm_ref, gq_ref, km_ref, vm_ref, gk_ref, gv_ref, *rest):
    if has_cache:
        kc_ref, vc_ref, gkc_ref, gvc_ref, o_ref = rest
    else:
        (o_ref,) = rest
    sq = qm_ref.shape[0]
    group = ATT_GROUP_DEC if has_cache else ATT_GROUP_CTX
    lo_q = lax.broadcasted_iota(jnp.int32, (sq, LANES), 1) < MLA_V
    lo = lax.broadcasted_iota(jnp.int32, (GQA_ROWS, LANES), 1) < GQA_HD
    zero = jnp.zeros((GQA_ROWS, LANES), BF16)
    gks = [gkc_ref[...].astype(BF16), gk_ref[...]] if has_cache else [gk_ref[...]]
    gvs = [gvc_ref[...].astype(BF16), gv_ref[...]] if has_cache else [gv_ref[...]]
    jobs = []
    for hd in range(MLA_HEADS):
        sl = slice(hd * HEAD_PAD, (hd + 1) * HEAD_PAD)
        vsl = slice((hd // 2) * LANES, (hd // 2 + 1) * LANES)
        ks = [kc_ref[:, sl], km_ref[:, sl]] if has_cache else [km_ref[:, sl]]
        vs = [vc_ref[:, vsl], vm_ref[:, vsl]] if has_cache else [vm_ref[:, vsl]]
        jobs.append((qm_ref[:, sl], ks, vs))
    for r0 in range(0, sq, GQA_ROWS):
        for p in range(GQA_W // LANES):
            qp = gq_ref[r0:r0 + GQA_ROWS, p * LANES:(p + 1) * LANES]
            jobs.append((jnp.concatenate([jnp.where(lo, qp, zero), jnp.where(lo, zero, qp)], axis=0), gks, gvs))
    res = []
    for j0 in range(0, len(jobs), group):
        res += _attend(jobs[j0:j0 + group])
    for p in range(MLA_HEADS // 2):
        o_ref[:, p * LANES:(p + 1) * LANES] = jnp.where(lo_q, res[2 * p], res[2 * p + 1]).astype(BF16)
    for i, r0 in enumerate(range(0, sq, GQA_ROWS)):
        chunk = res[MLA_HEADS + 3 * i:MLA_HEADS + 3 * i + 3]
        gres = [jnp.where(lo, o2[0:GQA_ROWS], o2[GQA_ROWS:2 * GQA_ROWS]) for o2 in chunk]
        swapped = pltpu.roll(gres[1], GQA_HD, 1)
        blocks = (jnp.where(lo, gres[0], swapped), jnp.where(lo, gres[2], gres[0]), jnp.where(lo, swapped, gres[2]))
        for p, blkv in enumerate(blocks):
            o_ref[r0:r0 + GQA_ROWS, MLA_W + p * LANES:MLA_W + (p + 1) * LANES] = blkv.astype(BF16)


def _attn_call(qm, gq, km, vm, gk, gv, cache, layer):
    has_cache = cache is not None
    t = qm.shape[0]
    widths = (MLA_QW, GQA_W, MLA_QW, MLA_W, LANES, LANES)
    if has_cache:
        nq = DEC_SEQ // TQ_DEC
        grid = (DEC_BATCH, nq)
        qmap = lambda b, i: (b * nq + i, 0)
        kmap = lambda b, i: (b, 0)
        once = pl.Buffered(1)
        in_specs = [pl.BlockSpec((TQ_DEC, widths[0]), qmap), pl.BlockSpec((TQ_DEC, widths[1]), qmap)]
        in_specs += [pl.BlockSpec((DEC_SEQ, wd), kmap, pipeline_mode=once) for wd in widths[2:]]
        in_specs += [pl.BlockSpec((PAST_LEN, wd), kmap, pipeline_mode=once) for wd in widths[2:4]]
        in_specs += [pl.BlockSpec((None, None, PAST_LEN, LANES), lambda b, i: (b, layer, 0, 0),
                                  pipeline_mode=once)] * 2
        args = [qm, gq, km, vm, gk, gv, *cache]
        out_spec = pl.BlockSpec((TQ_DEC, MLA_W + GQA_W), qmap)
        sem = ("arbitrary", "arbitrary")
    else:
        grid = (t // SEQ,)
        bmap = lambda b: (b, 0)
        in_specs = [pl.BlockSpec((SEQ, wd), bmap) for wd in widths]
        args = [qm, gq, km, vm, gk, gv]
        out_spec = pl.BlockSpec((SEQ, MLA_W + GQA_W), bmap)
        sem = ("arbitrary",)
    return pl.pallas_call(
        functools.partial(_attn_kernel, has_cache),
        grid=grid,
        in_specs=in_specs,
        out_specs=out_spec,
        out_shape=jax.ShapeDtypeStruct((t, MLA_W + GQA_W), BF16),
        compiler_params=_cparams(sem),
        name="attn_dec" if has_cache else "attn_ctx",
    )(*args)


def _ffn_kernel(layer, dec, hg_ref, at_ref, x_ref, mod_ref, w_out_ref, ln1g_ref, ln1b_ref, w_gu_ref, w_dn_ref,
                ln2g_ref, ln2b_ref, o_ref):
    lrow = slice(layer, layer + 1)
    mod = _mod_row(mod_ref, dec, DEC_SEQ // TM_FFN)
    g1 = mod[:, 2 * D_MODEL:3 * D_MODEL]
    sh2 = mod[:, 3 * D_MODEL:4 * D_MODEL]
    sc2 = mod[:, 4 * D_MODEL:5 * D_MODEL]
    g2 = mod[:, 5 * D_MODEL:6 * D_MODEL]
    y = _dot(hg_ref[...], w_out_ref[0:HG_W, :]) + _dot(at_ref[...], w_out_ref[HG_W:D_MODEL, :])
    x1 = _layernorm_rows(ALPHA * x_ref[...] + g1 * y, ln1g_ref[lrow, :], ln1b_ref[lrow, :])
    hf = (x1 * (1.0 + sc2) + sh2).astype(BF16)
    ch = D_FF // FF_CHUNKS
    f = None
    for c in range(FF_CHUNKS):
        gate = _dot(hf, w_gu_ref[:, c * ch:(c + 1) * ch])
        up = _dot(hf, w_gu_ref[:, D_FF + c * ch:D_FF + (c + 1) * ch])
        part = _dot((_silu(gate) * up).astype(BF16), w_dn_ref[c * ch:(c + 1) * ch, :])
        f = part if f is None else f + part
    o_ref[...] = _layernorm_rows(ALPHA * x1 + g2 * f, ln2g_ref[lrow, :], ln2b_ref[lrow, :])


def _ffn_call(hg, at, x, mod, w, prm, layer, dec):
    t = x.shape[0]
    row = lambda i: (i, 0)
    once = pl.Buffered(1)
    vec = pl.BlockSpec((DEPTH, D_MODEL), lambda i: (0, 0), pipeline_mode=once)
    return pl.pallas_call(
        functools.partial(_ffn_kernel, layer, dec),
        grid=(t // TM_FFN,),
        in_specs=[
            pl.BlockSpec((TM_FFN, HG_W), row),
            pl.BlockSpec((TM_FFN, MLA_W + GQA_W), row),
            pl.BlockSpec((TM_FFN, D_MODEL), row),
            _layer_spec((MOD_ROWS, N_ADA), layer, pipeline_mode=once),
            _layer_spec((D_MODEL, D_MODEL), layer, pipeline_mode=once),
            vec, vec,
            _layer_spec((D_MODEL, 2 * D_FF), layer, pipeline_mode=once),
            _layer_spec((D_FF, D_MODEL), layer, pipeline_mode=once),
            vec, vec,
        ],
        out_specs=pl.BlockSpec((TM_FFN, D_MODEL), row),
        out_shape=jax.ShapeDtypeStruct((t, D_MODEL), F32),
        compiler_params=_cparams(("arbitrary",)),
        name="ffn_dec" if dec else "ffn_ctx",
    )(hg, at, x, mod, w["w_out"], prm["ln1_g"], prm["ln1_b"], w["w_gu"], w["w_dn"], prm["ln2_g"], prm["ln2_b"])


def _rope_tables(width, rot_off, rot_dim):
    t = np.arange(DEC_SEQ)
    pos = (t // GRID_W, t % GRID_W)
    half = rot_dim // 4
    freqs = ROPE_THETA ** (-np.arange(half, dtype=np.float64) / half)
    c = np.ones((DEC_SEQ, LANES), np.float64)
    sa = np.zeros((DEC_SEQ, LANES), np.float64)
    sb = np.zeros((DEC_SEQ, LANES), np.float64)
    for blk in range(LANES // width):
        for axis in range(2):
            base = blk * width + rot_off + axis * 2 * half
            ang = pos[axis][:, None].astype(np.float64) * freqs[None, :]
            c[:, base:base + half] = np.cos(ang)
            c[:, base + half:base + 2 * half] = np.cos(ang)
            sa[:, base:base + half] = -np.sin(ang)
            sb[:, base + half:base + 2 * half] = np.sin(ang)
    return tuple(jnp.asarray(a, F32) for a in (c, sa, sb))


def _constants():
    C = HG_CHUNK
    t = np.arange(C)
    tri_f = (t[None, :] <= t[:, None]).astype(np.float32)
    tri_b = (t[None, :] >= t[:, None]).astype(np.float32)
    x = t[:, None] ^ t[None, :]
    top = np.where(x > 0, np.floor(np.log2(np.maximum(x, 1))).astype(np.int32), -1)
    same = (t[:, None] // SUBLANES) == (t[None, :] // SUBLANES)
    lvl_f = np.where(same, np.where(t[:, None] >= t[None, :], 0, -1),
                     np.where(t[:, None] > t[None, :], top, -1)).astype(np.int32)
    lvl_b = np.where(same, np.where(t[:, None] <= t[None, :], 0, -1),
                     np.where(t[:, None] < t[None, :], top, -1)).astype(np.int32)
    grp = np.arange(LANES) // HG_DK
    blk = (grp[:, None] == grp[None, :]).astype(np.float32)
    stk = np.zeros((SUBLANES, LANES, 2, C), np.float32)
    for s in range(SUBLANES):
        for hb in range(2):
            stk[s, hb * HG_DK:(hb + 1) * HG_DK, hb, s::SUBLANES] = 1.0
    stk = stk.reshape(SUBLANES * LANES, 2 * C)
    place = np.zeros((MLA_ROPE, MLA_QW), np.float32)
    for hd in range(MLA_HEADS):
        place[np.arange(MLA_ROPE), hd * HEAD_PAD + MLA_NOPE + np.arange(MLA_ROPE)] = 1.0
    mla_c, mla_sa, mla_sb = _rope_tables(HEAD_PAD, MLA_NOPE, MLA_ROPE)
    gqa_c, gqa_sa, gqa_sb = _rope_tables(GQA_HD, 0, GQA_HD)
    return {
        "tri_f": jnp.asarray(tri_f, BF16), "tri_b": jnp.asarray(tri_b, BF16),
        "lvl_f": jnp.asarray(np.concatenate([lvl_f, lvl_f], axis=0)),
        "lvl_b": jnp.asarray(np.concatenate([lvl_b, lvl_b], axis=0)),
        "stk": jnp.asarray(stk, BF16), "ones": jnp.asarray(blk, BF16), "ones64": jnp.asarray(blk / GQA_HD, BF16),
        "blk": jnp.asarray(blk, F32), "kpe_place": jnp.asarray(place, BF16),
        "mla_c": mla_c, "mla_sa": mla_sa, "mla_sb": mla_sb, "gqa_c": gqa_c, "gqa_sa": gqa_sa, "gqa_sb": gqa_sb,
    }


def kernel(x_prompt, x_sample, cache_mla_ckv, cache_mla_kpe, cache_gqa_k, cache_gqa_v, state_hgrn, c, c_ctx, w_ada,
           b_ada, w_in, hg_lb, hg_norm, mla_q_norm, mla_w_uq, mla_kv_norm, mla_w_ukv, gqa_q_norm, gqa_k_norm, w_out,
           ln1_g, ln1_b, w_ffn_in, w_ffn_out, ln2_g, ln2_b):
    cst = _constants()
    prm = {"mla_q_norm": mla_q_norm, "mla_kv_norm": mla_kv_norm, "gqa_q_norm": gqa_q_norm, "gqa_k_norm": gqa_k_norm,
           "ln1_g": ln1_g, "ln1_b": ln1_b, "ln2_g": ln2_g, "ln2_b": ln2_b}
    w = {"w_in": _prep_in_call(w_in)}
    w["w_uq"], w["w_uk"], w["w_uv"] = _prep_mla_call(mla_w_uq, mla_w_ukv)
    w["w_out"], w["w_gu"], w["w_dn"] = _prep_cast_call(w_out, w_ffn_in, w_ffn_out)

    cvec = jnp.concatenate([c_ctx[None, :], c, jnp.zeros((MOD_ROWS - 1 - DEC_BATCH, D_MODEL), F32)], axis=0)
    mod = _ada_call(cvec, w_ada, b_ada)
    cache_gk = cache_gqa_k.reshape(DEC_BATCH, DEPTH, PAST_LEN, LANES)
    cache_gv = cache_gqa_v.reshape(DEC_BATCH, DEPTH, PAST_LEN, LANES)

    x_ctx = x_prompt.reshape(BATCH * SEQ, D_MODEL)
    x_dec = x_sample.reshape(DEC_BATCH * DEC_SEQ, D_MODEL)
    states, hg_state = None, None
    for l in range(DEPTH):
        zhf, zhb, qm, km, vm, gq, gk, gv, *states = _inproj_call(x_ctx, mod, w, prm, cst, l, rope=False,
                                                          prev_states=states)
        hg_out, hg_state = _hgrn_call(zhf, zhb, hg_lb, hg_norm, cst, l, BATCH, SEQ, prev_state_out=hg_state)
        at = _attn_call(qm, gq, km, vm, gk, gv, None, l)
        x_ctx = _ffn_call(hg_out, at, x_ctx, mod, w, prm, l, dec=False)

        zhf, zhb, qm, km, vm, gq, gk, gv = _inproj_call(x_dec, mod, w, prm, cst, l, rope=True)
        hg_out = _hgrn_call(zhf, zhb, hg_lb, hg_norm, cst, l, DEC_BATCH, DEC_SEQ, state_in=state_hgrn)[0]
        kc, vc = _cachekv_call(cache_mla_ckv, cache_mla_kpe, w, cst, l)
        at = _attn_call(qm, gq, km, vm, gk, gv, (kc, vc, cache_gk, cache_gv), l)
        x_dec = _ffn_call(hg_out, at, x_dec, mod, w, prm, l, dec=True)

    ckv, kpe, gk32, gv32 = states
    kv_shape = (BATCH, DEPTH, SEQ, GQA_KV_HEADS, GQA_HD)
    return (x_ctx.reshape(BATCH, SEQ, D_MODEL), x_dec.reshape(DEC_BATCH, DEC_SEQ, D_MODEL),
            ckv, kpe, gk32.reshape(kv_shape), gv32.reshape(kv_shape), hg_state)
```

```python
import functools

import numpy as np
import jax
import jax.numpy as jnp
from jax import lax
from jax.experimental import pallas as pl
from jax.experimental.pallas import tpu as pltpu

F32 = jnp.float32
BF16 = jnp.bfloat16

D_MODEL = 1024
BATCH = 32
SEQ = 256
DEPTH = 2
DEC_BATCH = 2
DEC_SEQ = 2048
PAST_LEN = 256
GRID_W = 64
ROPE_THETA = 10000.0
EPS = 1e-6
GATE_FLOOR = 1e-30
HG_HEADS = 4
HG_DK = 64
HG_DV = 64
MLA_HEADS = 6
MLA_NOPE = 64
MLA_ROPE = 32
MLA_V = 64
MLA_Q_RANK = 256
MLA_KV_RANK = 128
GQA_HEADS = 6
GQA_KV_HEADS = 2
GQA_HD = 64
HG_W = HG_HEADS * HG_DV
MLA_W = MLA_HEADS * MLA_V
GQA_W = GQA_HEADS * GQA_HD
D_FF = 2816
MLA_SCALE = (MLA_NOPE + MLA_ROPE) ** -0.5
GQA_SCALE = GQA_HD ** -0.5
ALPHA = (2 * DEPTH) ** 0.25
LOG2E = 1.4426950408889634

LANES = 128
SUBLANES = 8
HEAD_PAD = 128
MLA_QW = MLA_HEADS * HEAD_PAD
N_ADA = 6 * D_MODEL
MOD_ROWS = 8

N_IN = 5 * HG_HEADS * HG_DK + MLA_Q_RANK + MLA_KV_RANK + MLA_ROPE + GQA_W + 2 * GQA_KV_HEADS * GQA_HD
SRC_KPE = 5 * HG_HEADS * HG_DK + MLA_Q_RANK + MLA_KV_RANK
SRC_GQ = SRC_KPE + MLA_ROPE
SRC_GK = SRC_GQ + GQA_W
SRC_GV = SRC_GK + GQA_KV_HEADS * GQA_HD
HG_COLS = 5 * HG_HEADS * HG_DK
OFF_MQ = HG_COLS
OFF_MKV = OFF_MQ + MLA_Q_RANK
OFF_GQ = OFF_MKV + MLA_KV_RANK
OFF_GK = OFF_GQ + GQA_W
OFF_GV = OFF_GK + GQA_KV_HEADS * GQA_HD
OFF_KPE_PAD = OFF_GV + GQA_KV_HEADS * GQA_HD
OFF_KPE = OFF_KPE_PAD + HEAD_PAD
N_Z = OFF_KPE + MLA_ROPE

TM_IN = 512
TM_FFN = 512
TQ_DEC = 512
GQA_ROWS = 256
ATT_GROUP_CTX = 9
ATT_GROUP_DEC = 1
VT_ROWS = 80
HG_CHUNK = 128
HG_UNROLL = 2
FF_CHUNKS = 2
PREP_STEPS = 4
VMEM_LIMIT = 56 * 1024 * 1024


def _cparams(sem):
    return pltpu.CompilerParams(dimension_semantics=sem, vmem_limit_bytes=VMEM_LIMIT)


def _const_spec(shape):
    nd = len(shape)
    return pl.BlockSpec(shape, lambda *_: (0,) * nd)


def _layer_spec(shape, layer, **kw):
    nd = len(shape)
    return pl.BlockSpec((None,) + tuple(shape), lambda *_: (layer,) + (0,) * nd, **kw)


_ANY = pl.BlockSpec(memory_space=pl.ANY)


def _silu(x):
    return x * jax.nn.sigmoid(x)


def _dot(a, b):
    return jnp.dot(a, b, preferred_element_type=F32)


def _dot_nt(a, b):
    return lax.dot_general(a, b, (((1,), (1,)), ((), ())), preferred_element_type=F32)


def _dot_tn(a, b):
    return lax.dot_general(a, b, (((0,), (0,)), ((), ())), preferred_element_type=F32)


def _rms_rows(x, g):
    y = x * lax.rsqrt(jnp.mean(x * x, axis=-1, keepdims=True) + EPS)
    return y * g


def _layernorm_rows(x, g, b):
    xc = x - jnp.mean(x, axis=-1, keepdims=True)
    var = jnp.mean(xc * xc, axis=-1, keepdims=True)
    return xc * lax.rsqrt(var + EPS) * g + b


def _group_mean(x, ones_blk):
    return _dot(x.astype(BF16), ones_blk)


def _twice(g):
    return jnp.concatenate([g, g], axis=1)


_GQ_HEAD_ORDER = (0, 3, 1, 4, 2, 5)


def _prep_in_kernel(w_ref, o_ref):
    x = w_ref[...]
    rows = x.shape[0]
    kr = x[:, SRC_KPE:SRC_GQ]
    gq = [x[:, SRC_GQ + h * GQA_HD:SRC_GQ + (h + 1) * GQA_HD] for h in _GQ_HEAD_ORDER]
    pieces = [x[:, 0:SRC_KPE]] + gq + [
        x[:, SRC_GK:SRC_GV], x[:, SRC_GV:N_IN],
        jnp.zeros((rows, MLA_NOPE), F32), kr, jnp.zeros((rows, HEAD_PAD - MLA_NOPE - MLA_ROPE), F32), kr]
    o_ref[...] = jnp.concatenate(pieces, axis=1).astype(BF16)


def _prep_in_call(w_in):
    rb = 128
    return pl.pallas_call(
        _prep_in_kernel,
        grid=(DEPTH, D_MODEL // rb),
        in_specs=[pl.BlockSpec((None, rb, N_IN), lambda l, j: (l, j, 0))],
        out_specs=pl.BlockSpec((None, rb, N_Z), lambda l, j: (l, j, 0)),
        out_shape=jax.ShapeDtypeStruct((DEPTH, D_MODEL, N_Z), BF16),
        compiler_params=_cparams(("arbitrary", "arbitrary")),
        name="prep_in",
    )(w_in)


def _prep_mla_kernel(uq_ref, ukv_ref, oq_ref, ok_ref, ov_ref):
    uq = uq_ref[...]
    dq = MLA_NOPE + MLA_ROPE
    zq = jnp.zeros((MLA_Q_RANK, HEAD_PAD - dq), F32)
    oq_ref[...] = jnp.concatenate(
        [p for h in range(MLA_HEADS) for p in (uq[:, h * dq:(h + 1) * dq], zq)], axis=1).astype(BF16)
    ukv = ukv_ref[...]
    lane = lax.broadcasted_iota(jnp.int32, ukv.shape, 1)
    ok_ref[...] = jnp.where((lane & (HEAD_PAD - 1)) < MLA_NOPE, ukv, 0.0).astype(BF16)
    ov_ref[...] = jnp.concatenate(
        [ukv[:, h * HEAD_PAD + MLA_NOPE:(h + 1) * HEAD_PAD] for h in range(MLA_HEADS)], axis=1).astype(BF16)


def _prep_mla_call(w_uq, w_ukv):
    lay = lambda n: pl.BlockSpec((None, n[0], n[1]), lambda l: (l, 0, 0))
    shapes = [(MLA_Q_RANK, MLA_QW), (MLA_KV_RANK, MLA_QW), (MLA_KV_RANK, MLA_W)]
    return pl.pallas_call(
        _prep_mla_kernel,
        grid=(DEPTH,),
        in_specs=[lay((MLA_Q_RANK, MLA_HEADS * (MLA_NOPE + MLA_ROPE))), lay((MLA_KV_RANK, MLA_QW))],
        out_specs=[lay(s) for s in shapes],
        out_shape=[jax.ShapeDtypeStruct((DEPTH,) + s, BF16) for s in shapes],
        compiler_params=_cparams(("arbitrary",)),
        name="prep_mla",
    )(w_uq, w_ukv)


def _prep_cast_kernel(a_ref, b_ref, c_ref, oa_ref, ob_ref, oc_ref):
    oa_ref[...] = a_ref[...].astype(BF16)
    ob_ref[...] = b_ref[...].astype(BF16)
    oc_ref[...] = c_ref[...].astype(BF16)


def _prep_cast_call(w_out, w_ffn_in, w_ffn_out):
    ws = (w_out, w_ffn_in, w_ffn_out)
    specs = [pl.BlockSpec((None, w.shape[1] // PREP_STEPS, w.shape[2]), lambda l, j: (l, j, 0)) for w in ws]
    return pl.pallas_call(
        _prep_cast_kernel,
        grid=(DEPTH, PREP_STEPS),
        in_specs=specs,
        out_specs=specs,
        out_shape=[jax.ShapeDtypeStruct(w.shape, BF16) for w in ws],
        compiler_params=_cparams(("arbitrary", "arbitrary")),
        name="prep_cast",
    )(*ws)


def _ada_kernel(c_ref, w_ref, b_ref, o_ref):
    s = _silu(c_ref[...]).astype(BF16)
    o_ref[...] = _dot(s, w_ref[...].astype(BF16)) + b_ref[pl.ds(pl.program_id(0), 1), :]


def _ada_call(cvec, w_ada, b_ada):
    tn = 1536
    return pl.pallas_call(
        _ada_kernel,
        grid=(DEPTH, N_ADA // tn),
        in_specs=[
            pl.BlockSpec((MOD_ROWS, D_MODEL), lambda l, j: (0, 0)),
            pl.BlockSpec((None, D_MODEL, tn), lambda l, j: (l, 0, j)),
            pl.BlockSpec((DEPTH, tn), lambda l, j: (0, j)),
        ],
        out_specs=pl.BlockSpec((None, MOD_ROWS, tn), lambda l, j: (l, 0, j)),
        out_shape=jax.ShapeDtypeStruct((DEPTH, MOD_ROWS, N_ADA), F32),
        compiler_params=_cparams(("arbitrary", "arbitrary")),
        name="ada_mod",
    )(cvec, w_ada, b_ada)


def _mod_row(mod_ref, dec, tiles_per_seq):
    if not dec:
        return mod_ref[0:1, :]
    return mod_ref[pl.ds(1 + pl.program_id(0) // tiles_per_seq, 1), :]


def _rope_block(x, c, sa, sb, half):
    return x * c + pltpu.roll(x, LANES - half, 1) * sa + pltpu.roll(x, half, 1) * sb


def _inproj_kernel(layer, rope, x_ref, mod_ref, w_in_ref, w_uq_ref, w_uk_ref, w_uv_ref, qn_ref, kvn_ref, gqn_ref,
                   gkn_ref, ones_ref, *rest):
    if rope:
        (mc_ref, msa_ref, msb_ref, gc_ref, gsa_ref, gsb_ref,
         zhf_ref, zhb_ref, qm_ref, km_ref, vm_ref, gq_ref, gk_ref, gv_ref) = rest
    else:
        rest = rest[-12:]
        (zhf_ref, zhb_ref, qm_ref, km_ref, vm_ref, gq_ref, gk_ref, gv_ref,
         ckv_ref, kpe_ref, gk32_ref, gv32_ref) = rest
    lrow = slice(layer, layer + 1)
    qk_base = LOG2E if rope else 1.0
    mod = _mod_row(mod_ref, rope, DEC_SEQ // TM_IN)
    sh1 = mod[:, 0:D_MODEL]
    sc1 = mod[:, D_MODEL:2 * D_MODEL]
    h = (x_ref[...] * (1.0 + sc1) + sh1).astype(BF16)
    z = _dot(h, w_in_ref[...])
    gw = HG_HEADS * HG_DK
    zhf_ref[...] = z[:, gw:3 * gw]
    zhb_ref[:, 0:gw] = z[:, 0:gw].astype(BF16)
    zhb_ref[:, gw:3 * gw] = z[:, 3 * gw:HG_COLS].astype(BF16)
    mq = z[:, OFF_MQ:OFF_MKV]
    mkv = z[:, OFF_MKV:OFF_GQ]
    gq = z[:, OFF_GQ:OFF_GK]
    gk = z[:, OFF_GK:OFF_GV]
    gv = z[:, OFF_GV:OFF_KPE_PAD]
    kpe_pad = z[:, OFF_KPE_PAD:OFF_KPE]
    ones = ones_ref[...]
    gqn = _twice(gqn_ref[lrow, :])
    gkn_g = _twice(gkn_ref[lrow, :])

    q = _dot(_rms_rows(mq, qn_ref[lrow, :]).astype(BF16), w_uq_ref[...])
    ckv = _rms_rows(mkv, kvn_ref[lrow, :])
    ckv_b = ckv.astype(BF16)
    kn = _dot(ckv_b, w_uk_ref[...])
    vm_ref[...] = _dot(ckv_b, w_uv_ref[...]).astype(BF16)
    if rope:
        mc, msa, msb = mc_ref[...], msa_ref[...], msb_ref[...]
        kpe_pad = _rope_block(kpe_pad, mc, msa, msb, MLA_ROPE // 4)
    for hd in range(MLA_HEADS):
        sl = slice(hd * HEAD_PAD, (hd + 1) * HEAD_PAD)
        qh = q[:, sl]
        if rope:
            qh = _rope_block(qh, mc, msa, msb, MLA_ROPE // 4)
        qm_ref[:, sl] = (qh * (MLA_SCALE * qk_base)).astype(BF16)
        km_ref[:, sl] = (kn[:, sl] + kpe_pad).astype(BF16)

    gkn = gk * lax.rsqrt(_group_mean(gk * gk, ones) + EPS) * gkn_g
    for p in range(GQA_W // LANES):
        sl = slice(p * LANES, (p + 1) * LANES)
        gqp = gq[:, sl]
        gqp = gqp * lax.rsqrt(_group_mean(gqp * gqp, ones) + EPS) * gqn
        if rope:
            gqp = _rope_block(gqp, gc_ref[...], gsa_ref[...], gsb_ref[...], GQA_HD // 4)
        gq_ref[:, sl] = (gqp * (GQA_SCALE * qk_base)).astype(BF16)
    if rope:
        gk_ref[...] = _rope_block(gkn, gc_ref[...], gsa_ref[...], gsb_ref[...], GQA_HD // 4).astype(BF16)
    else:
        gk_ref[...] = gkn.astype(BF16)
        kpe = z[:, OFF_KPE:N_Z]
        for b in range(TM_IN // SEQ):
            rows = slice(b * SEQ, (b + 1) * SEQ)
            ckv_ref[b] = ckv[rows]
            kpe_ref[b] = kpe[rows]
            gk32_ref[b] = gkn[rows]
            gv32_ref[b] = gv[rows]
    gv_ref[...] = gv.astype(BF16)


def _inproj_call(x, mod, w, prm, cst, layer, rope, prev_states=None):
    t = x.shape[0]
    nt = t // TM_IN
    row = lambda i: (i, 0)
    in_specs = [
        pl.BlockSpec((TM_IN, D_MODEL), row),
        _layer_spec((MOD_ROWS, N_ADA), layer),
        _layer_spec((D_MODEL, N_Z), layer),
        _layer_spec((MLA_Q_RANK, MLA_QW), layer),
        _layer_spec((MLA_KV_RANK, MLA_QW), layer),
        _layer_spec((MLA_KV_RANK, MLA_W), layer),
        _const_spec((DEPTH, MLA_Q_RANK)),
        _const_spec((DEPTH, MLA_KV_RANK)),
        _const_spec((DEPTH, GQA_HD)),
        _const_spec((DEPTH, GQA_HD)),
        _const_spec((LANES, LANES)),
    ]
    args = [x, mod, w["w_in"], w["w_uq"], w["w_uk"], w["w_uv"], prm["mla_q_norm"], prm["mla_kv_norm"],
            prm["gqa_q_norm"], prm["gqa_k_norm"], cst["ones64"]]
    outs = [(2 * HG_HEADS * HG_DK, F32), (3 * HG_HEADS * HG_DK, BF16), (MLA_QW, BF16), (MLA_QW, BF16), (MLA_W, BF16),
            (GQA_W, BF16), (LANES, BF16), (LANES, BF16)]
    out_specs = [pl.BlockSpec((TM_IN, wd), row) for wd, _ in outs]
    out_shape = [jax.ShapeDtypeStruct((t, wd), dt) for wd, dt in outs]
    aliases = {}
    if rope:
        tab = lambda i: (i % (DEC_SEQ // TM_IN), 0)
        in_specs += [pl.BlockSpec((TM_IN, LANES), tab)] * 6
        args += [cst["mla_c"], cst["mla_sa"], cst["mla_sb"], cst["gqa_c"], cst["gqa_sa"], cst["gqa_sb"]]
    else:
        per = TM_IN // SEQ
        for wd in (MLA_KV_RANK, MLA_ROPE, LANES, LANES):
            out_specs.append(pl.BlockSpec((per, None, SEQ, wd), lambda i: (i, layer, 0, 0)))
            out_shape.append(jax.ShapeDtypeStruct((BATCH, DEPTH, SEQ, wd), F32))
        if prev_states is not None:
            for k, arr in enumerate(prev_states):
                aliases[len(args)] = len(outs) + k
                in_specs.append(_ANY)
                args.append(arr)
    return pl.pallas_call(
        functools.partial(_inproj_kernel, layer, rope),
        grid=(nt,),
        in_specs=in_specs,
        out_specs=out_specs,
        out_shape=out_shape,
        input_output_aliases=aliases,
        compiler_params=_cparams(("arbitrary",)),
        name="inproj_dec" if rope else "inproj_ctx",
    )(*args)


def _cachekv_kernel(ckv_ref, kpe_ref, w_uk_ref, w_uv_ref, place_ref, k_ref, v_ref):
    ckv_b = ckv_ref[...].astype(BF16)
    k_ref[...] = (_dot(ckv_b, w_uk_ref[...]) + _dot(kpe_ref[...].astype(BF16), place_ref[...])).astype(BF16)
    v_ref[...] = _dot(ckv_b, w_uv_ref[...]).astype(BF16)


def _cachekv_call(cache_ckv, cache_kpe, w, cst, layer):
    t = DEC_BATCH * PAST_LEN
    cache = lambda wd: pl.BlockSpec((None, None, PAST_LEN, wd), lambda b: (b, layer, 0, 0))
    return pl.pallas_call(
        _cachekv_kernel,
        grid=(DEC_BATCH,),
        in_specs=[cache(MLA_KV_RANK), cache(MLA_ROPE),
                  _layer_spec((MLA_KV_RANK, MLA_QW), layer), _layer_spec((MLA_KV_RANK, MLA_W), layer),
                  _const_spec((MLA_ROPE, MLA_QW))],
        out_specs=[pl.BlockSpec((PAST_LEN, MLA_QW), lambda b: (b, 0)),
                   pl.BlockSpec((PAST_LEN, MLA_W), lambda b: (b, 0))],
        out_shape=[jax.ShapeDtypeStruct((t, MLA_QW), BF16), jax.ShapeDtypeStruct((t, MLA_W), BF16)],
        compiler_params=_cparams(("arbitrary",)),
        name="cache_kv",
    )(cache_ckv, cache_kpe, w["w_uk"], w["w_uv"], cst["kpe_place"])


def _bcast_row(x, r, rows):
    return jnp.broadcast_to(x[r:r + 1, :], (rows, x.shape[1]))


def _cumsum_rows(tri, x):
    hi = x.astype(BF16)
    r1 = x - hi.astype(F32)
    mid = r1.astype(BF16)
    lo = (r1 - mid.astype(F32)).astype(BF16)
    return _dot(tri, hi) + _dot(tri, mid) + _dot(tri, lo)


def _hgrn_chunks(items, states, tri, lvl, stk, blk, lane_lo):
    C = HG_CHUNK
    zero_b = jnp.zeros((SUBLANES, LANES), F32)
    trow = lax.broadcasted_iota(jnp.int32, (SUBLANES, LANES), 0)
    for it in items:
        it["c"] = _cumsum_rows(tri[it["fwd"]], it["lf"])
    for it in items:
        it["q_a"] = jnp.where(lane_lo, it["q"], 0.0)
        it["q_b"] = jnp.where(lane_lo, 0.0, it["q"])
        it["ck"] = it["c"] - jnp.log(it["kk"])

    m = C // 2
    bit = m.bit_length() - 1
    while m >= SUBLANES:
        for it in items:
            fwd, c = it["fwd"], it["c"]
            qa_rows, qb_rows, k_rows = [], [], []
            for b in range(C // m):
                rows = slice(b * m, (b + 1) * m)
                anchor = (b // 2) * 2 * m + (m - 1 if fwd else m)
                r = _bcast_row(c, anchor, m)
                if (b % 2 == 1) == fwd:
                    e = jnp.exp(c[rows] - r)
                    qa_rows.append(it["q_a"][rows] * e)
                    qb_rows.append(it["q_b"][rows] * e)
                    k_rows += [zero_b] * (m // SUBLANES)
                else:
                    e = jnp.exp(r - c[rows])
                    qa_rows += [zero_b] * (m // SUBLANES)
                    qb_rows += [zero_b] * (m // SUBLANES)
                    k_rows.append(it["kk"][rows] * e)
            prod = _dot_nt(jnp.concatenate(qa_rows + qb_rows, axis=0).astype(BF16),
                           jnp.concatenate(k_rows, axis=0).astype(BF16))
            it["a"] = prod if "a" not in it else jnp.where(lvl[fwd] == bit, prod, it["a"])
        m //= 2
        bit -= 1

    for it in items:
        it["w"] = []
    for j in range(C // SUBLANES):
        sl = slice(j * SUBLANES, (j + 1) * SUBLANES)
        for it in items:
            cs, cks, qs = it["c"][sl], it["ck"][sl], it["q"][sl]
            w_s = []
            for s in range(SUBLANES):
                e = jnp.exp(cs - _bcast_row(cks, s, SUBLANES))
                keep = (trow >= s) if it["fwd"] else (trow <= s)
                w_s.append(jnp.where(keep, qs * e, 0.0))
            it["w"].append(jnp.concatenate(w_s, axis=1))
    for it in items:
        red = _dot(jnp.concatenate(it["w"], axis=0).astype(BF16), stk)
        it["a"] = jnp.where(lvl[it["fwd"]] == 0,
                            jnp.concatenate([red[:, 0:LANES], red[:, LANES:2 * LANES]], axis=0), it["a"])
    for it in items:
        off = _dot(it["a"].astype(BF16), it["iv"].astype(BF16))
        it["o"] = jnp.where(lane_lo, off[0:C], off[C:2 * C])
        it["edge"] = _bcast_row(it["c"], C - 1 if it["fwd"] else 0, C)
        it["q_dec"] = (it["q"] * jnp.exp(it["c"])).astype(BF16)
        k_dec = (it["kk"] * jnp.exp(it["edge"] - it["c"])).astype(BF16)
        it["upd"] = _dot_tn(it["iv"].astype(BF16), k_dec) * blk

    states = list(states)
    outs = []
    for it in items:
        st = states[it["sid"]]
        outs.append(it["o"] + _dot_nt(it["q_dec"], st.astype(BF16)))
        states[it["sid"]] = st * jnp.exp(it["edge"][0:1]) + it["upd"]
    return outs, states


def _hgrn_kernel(layer, nchunks, has_state, zq_ref, zff_ref, zfb_ref, zi_ref, zg_ref, lb_ref, g_ref,
                 trif_ref, trib_ref, lvlf_ref, lvlb_ref, stk_ref, ones_ref, blk_ref, *rest):
    if has_state:
        s0_ref, o_ref, of_scr, ob_scr = rest
    else:
        o_ref, st_ref, of_scr, ob_scr = rest[-4:]
    C = HG_CHUNK
    lane_lo = lax.broadcasted_iota(jnp.int32, (1, LANES), 1) < HG_DK

    lb_all = [lb_ref[j] for j in range(DEPTH)]
    mx = functools.reduce(jnp.maximum, lb_all)
    ex = [jnp.exp(v - mx) for v in lb_all]
    den = functools.reduce(lambda a, b: a + b, ex)
    sm = [v / den for v in ex]
    lbs = functools.reduce(lambda a, b: a + b, sm[:layer + 1]) - sm[0]
    lbf, lbb = lbs[0:1], lbs[1:2]

    def gates(zf, lbd):
        f = lbd + (1.0 - lbd) * jax.nn.sigmoid(zf)
        return jnp.log(jnp.maximum(f, GATE_FLOOR)), (1.0 - lbd) * jax.nn.sigmoid(-zf)

    def block_diag_t(s_ref_d):
        za = jnp.zeros((HG_DK, HG_DV), F32)
        top = jnp.concatenate([s_ref_d[0], za], axis=1)
        bot = jnp.concatenate([za, s_ref_d[1]], axis=1)
        return jnp.concatenate([top, bot], axis=0).T

    if has_state:
        st_f0 = block_diag_t(s0_ref[0])
        st_b0 = block_diag_t(s0_ref[1])
    else:
        st_f0 = jnp.zeros((LANES, LANES), F32)
        st_b0 = st_f0

    def body(j, carry):
        tri = {True: trif_ref[...], False: trib_ref[...]}
        lvl = {True: lvlf_ref[...], False: lvlb_ref[...]}
        items = []
        for u in range(HG_UNROLL):
            jf = j * HG_UNROLL + u
            for fwd in (True, False):
                rows = pl.ds(pl.multiple_of((jf if fwd else nchunks - 1 - jf) * C, C), C)
                lf, kk = gates((zff_ref if fwd else zfb_ref)[rows, :], lbf if fwd else lbb)
                items.append({"fwd": fwd, "sid": 0 if fwd else 1, "rows": rows, "lf": lf, "kk": kk,
                              "q": _silu(zq_ref[rows, :].astype(F32)) * (HG_DK ** -0.5), "iv": zi_ref[rows, :]})
        outs, states = _hgrn_chunks(items, carry, tri, lvl, stk_ref[...], blk_ref[...], lane_lo)
        for it, o in zip(items, outs):
            (of_scr if it["fwd"] else ob_scr)[it["rows"], :] = o
        return tuple(states)

    st_f, st_b = lax.fori_loop(0, nchunks // HG_UNROLL, body, (st_f0, st_b0))

    o = of_scr[...] + ob_scr[...]
    ms = _group_mean(o * o, ones_ref[...] * (1.0 / HG_DV))
    g = _twice(g_ref[layer:layer + 1, :])
    o_ref[...] = (o * lax.rsqrt(ms + EPS) * g * _silu(zg_ref[...].astype(F32))).astype(BF16)
    if not has_state:
        for d, st in enumerate((st_f, st_b)):
            s_kv = st.T
            st_ref[d, 0] = s_kv[0:HG_DK, 0:HG_DV]
            st_ref[d, 1] = s_kv[HG_DK:2 * HG_DK, HG_DV:2 * HG_DV]


def _hgrn_call(zhf, zhb, lb, g, cst, layer, nb, s, state_in=None, prev_state_out=None):
    has_state = state_in is not None
    nchunks = s // HG_CHUNK
    npair = HG_HEADS // 2
    part = lambda k: pl.BlockSpec((s, LANES), lambda b, hp: (b, k * npair + hp))
    st_spec = pl.BlockSpec((None, None, 2, 2, HG_DK, HG_DV), lambda b, hp: (b, layer, 0, hp, 0, 0))
    in_specs = [part(0), part(0), part(1), part(1), part(2),
                pl.BlockSpec((DEPTH, 2, LANES), lambda b, hp: (0, 0, hp)),
                _const_spec((DEPTH, HG_DV)),
                _const_spec((HG_CHUNK, HG_CHUNK)), _const_spec((HG_CHUNK, HG_CHUNK)),
                _const_spec((2 * HG_CHUNK, HG_CHUNK)), _const_spec((2 * HG_CHUNK, HG_CHUNK)),
                _const_spec((SUBLANES * LANES, 2 * LANES)),
                _const_spec((LANES, LANES)), _const_spec((LANES, LANES))]
    args = [zhb, zhf, zhf, zhb, zhb, lb, g, cst["tri_f"], cst["tri_b"], cst["lvl_f"], cst["lvl_b"], cst["stk"],
            cst["ones"], cst["blk"]]
    out_specs = [pl.BlockSpec((s, LANES), lambda b, hp: (b, hp))]
    out_shape = [jax.ShapeDtypeStruct((nb * s, HG_W), BF16)]
    aliases = {}
    if has_state:
        in_specs.append(st_spec)
        args.append(state_in)
    else:
        out_specs.append(st_spec)
        out_shape.append(jax.ShapeDtypeStruct((nb, DEPTH, 2, HG_HEADS, HG_DK, HG_DV), F32))
        if prev_state_out is not None:
            aliases[len(args)] = 1
            in_specs.append(_ANY)
            args.append(prev_state_out)
    return pl.pallas_call(
        functools.partial(_hgrn_kernel, layer, nchunks, has_state),
        grid=(nb, npair),
        in_specs=in_specs,
        out_specs=out_specs,
        out_shape=out_shape,
        input_output_aliases=aliases,
        scratch_shapes=[pltpu.VMEM((s, LANES), F32), pltpu.VMEM((s, LANES), F32)],
        compiler_params=_cparams(("arbitrary", "arbitrary")),
        name="hgrn_dec" if has_state else "hgrn_ctx",
    )(*args)


def _attend(jobs):
    add = lambda a, b: a + b
    ss = [[_dot_nt(q, k) for k in ks] for q, ks, _ in jobs]
    mxs = [functools.reduce(jnp.maximum, [jnp.max(s, axis=-1, keepdims=True) for s in sj]) for sj in ss]
    es = [[jnp.exp(s - mx) for s in sj] for sj, mx in zip(ss, mxs)]
    dens = [functools.reduce(add, [jnp.sum(e, axis=-1, keepdims=True) for e in ej]) for ej in es]
    outs = [functools.reduce(add, [_dot(e.astype(BF16), v) for e, v in zip(ej, vs)])
            for ej, (_, _, vs) in zip(es, jobs)]
    return [o / d for o, d in zip(outs, dens)]


def _attn_kernel(has_cache, qm_ref, gq_ref, km_ref, vm_ref, gk_ref, gv_ref, *rest):
    if has_cache:
        kc_ref, vc_ref, gkc_ref, gvc_ref, o_ref = rest
    else:
        (o_ref,) = rest
    sq = qm_ref.shape[0]
    group = ATT_GROUP_DEC if has_cache else ATT_GROUP_CTX
    lo_q = lax.broadcasted_iota(jnp.int32, (sq, LANES), 1) < MLA_V
    lo = lax.broadcasted_iota(jnp.int32, (GQA_ROWS, LANES), 1) < GQA_HD
    zero = jnp.zeros((GQA_ROWS, LANES), BF16)
    gks = [gkc_ref[...].astype(BF16), gk_ref[...]] if has_cache else [gk_ref[...]]
    gvs = [gvc_ref[...].astype(BF16), gv_ref[...]] if has_cache else [gv_ref[...]]
    jobs = []
    for hd in range(MLA_HEADS):
        sl = slice(hd * HEAD_PAD, (hd + 1) * HEAD_PAD)
        vsl = slice((hd // 2) * LANES, (hd // 2 + 1) * LANES)
        ks = [kc_ref[:, sl], km_ref[:, sl]] if has_cache else [km_ref[:, sl]]
        vs = [vc_ref[:, vsl], vm_ref[:, vsl]] if has_cache else [vm_ref[:, vsl]]
        jobs.append((qm_ref[:, sl], ks, vs))
    for r0 in range(0, sq, GQA_ROWS):
        for p in range(GQA_W // LANES):
            qp = gq_ref[r0:r0 + GQA_ROWS, p * LANES:(p + 1) * LANES]
            jobs.append((jnp.concatenate([jnp.where(lo, qp, zero), jnp.where(lo, zero, qp)], axis=0), gks, gvs))
    res = []
    for j0 in range(0, len(jobs), group):
        res += _attend(jobs[j0:j0 + group])
    for p in range(MLA_HEADS // 2):
        o_ref[:, p * LANES:(p + 1) * LANES] = jnp.where(lo_q, res[2 * p], res[2 * p + 1]).astype(BF16)
    for i, r0 in enumerate(range(0, sq, GQA_ROWS)):
        chunk = res[MLA_HEADS + 3 * i:MLA_HEADS + 3 * i + 3]
        gres = [jnp.where(lo, o2[0:GQA_ROWS], o2[GQA_ROWS:2 * GQA_ROWS]) for o2 in chunk]
        swapped = pltpu.roll(gres[1], GQA_HD, 1)
        blocks = (jnp.where(lo, gres[0], swapped), jnp.where(lo, gres[2], gres[0]), jnp.where(lo, swapped, gres[2]))
        for p, blkv in enumerate(blocks):
            o_ref[r0:r0 + GQA_ROWS, MLA_W + p * LANES:MLA_W + (p + 1) * LANES] = blkv.astype(BF16)


def _attend_t(jobs, ss):
    add = lambda a, b: a + b
    mxs = [functools.reduce(jnp.maximum, [jnp.max(s, axis=-1, keepdims=True) for s in sj]) for sj in ss]
    ebs = [[jnp.exp2(s - mx).astype(BF16) for s in sj] for sj, mx in zip(ss, mxs)]
    prods = [[functools.reduce(add, [_dot_nt(vt, e[rows]) for vt, e in zip(vts, ej)]) for rows, vts in parts]
             for ej, (_, _, parts) in zip(ebs, jobs)]
    return [[pr[0:MLA_V] * (1.0 / pr[MLA_V:MLA_V + 1]) for pr in pj] for pj in prods]


def _scores(jobs):
    return [[_dot_nt(q, k) for k in ks] for q, ks, _ in jobs]


def _attn_dec_kernel(qm_ref, gq_ref, km_ref, vm_ref, gk_ref, gv_ref, kc_ref, vc_ref, gkc_ref, gvc_ref, o_ref, vt_scr):
    sq = qm_ref.shape[0]
    sk = vt_scr.shape[1]
    nblk = MLA_HEADS + GQA_KV_HEADS

    @pl.when(pl.program_id(1) == 0)
    def _():
        tr = lambda v: v.astype(F32).T.astype(BF16)
        pieces = ((slice(0, PAST_LEN), tr(vc_ref[...]), tr(gvc_ref[...])),
                  (slice(PAST_LEN, sk), tr(vm_ref[...]), tr(gv_ref[...])))
        for cols, mla_t, gqa_t in pieces:
            width = cols.stop - cols.start
            tail = (lax.broadcasted_iota(jnp.int32, (VT_ROWS - MLA_V, width), 0) == 0).astype(BF16)
            for b in range(nblk):
                src = mla_t[b * MLA_V:(b + 1) * MLA_V] if b < MLA_HEADS else \
                    gqa_t[(b - MLA_HEADS) * GQA_HD:(b - MLA_HEADS + 1) * GQA_HD]
                vt_scr[b * VT_ROWS:b * VT_ROWS + MLA_V, cols] = src
                vt_scr[b * VT_ROWS + MLA_V:(b + 1) * VT_ROWS, cols] = tail

    def vt(b):
        return [vt_scr[b * VT_ROWS:(b + 1) * VT_ROWS, 0:PAST_LEN], vt_scr[b * VT_ROWS:(b + 1) * VT_ROWS, PAST_LEN:sk]]

    lo = lax.broadcasted_iota(jnp.int32, (GQA_ROWS, LANES), 1) < GQA_HD
    zero = jnp.zeros((GQA_ROWS, LANES), BF16)
    gks = [gkc_ref[...].astype(BF16), gk_ref[...]]
    jobs = []
    for hd in range(MLA_HEADS):
        sl = slice(hd * HEAD_PAD, (hd + 1) * HEAD_PAD)
        jobs.append((qm_ref[:, sl], [kc_ref[:, sl], km_ref[:, sl]], [(slice(0, sq), vt(hd))]))
    halves = [(slice(0, GQA_ROWS), vt(MLA_HEADS)), (slice(GQA_ROWS, 2 * GQA_ROWS), vt(MLA_HEADS + 1))]
    for r0 in range(0, sq, GQA_ROWS):
        for p in range(GQA_W // LANES):
            qp = gq_ref[r0:r0 + GQA_ROWS, p * LANES:(p + 1) * LANES]
            jobs.append((jnp.concatenate([jnp.where(lo, qp, zero), jnp.where(lo, zero, qp)], axis=0), gks, halves))
    groups = [jobs[j0:j0 + ATT_GROUP_DEC] for j0 in range(0, len(jobs), ATT_GROUP_DEC)]
    outs = []
    ss_next = _scores(groups[0])
    for g, grp in enumerate(groups):
        ss = ss_next
        if g + 1 < len(groups):
            ss_next = _scores(groups[g + 1])
        outs += _attend_t(grp, ss)
    for p in range(MLA_HEADS // 2):
        pair = jnp.concatenate([outs[2 * p][0], outs[2 * p + 1][0]], axis=0)
        o_ref[:, p * LANES:(p + 1) * LANES] = pair.T.astype(BF16)
    for i, r0 in enumerate(range(0, sq, GQA_ROWS)):
        gres = [jnp.concatenate(outs[MLA_HEADS + 3 * i + p], axis=0).T for p in range(GQA_W // LANES)]
        swapped = pltpu.roll(gres[1], GQA_HD, 1)
        blocks = (jnp.where(lo, gres[0], swapped), jnp.where(lo, gres[2], gres[0]), jnp.where(lo, swapped, gres[2]))
        for p, blkv in enumerate(blocks):
            o_ref[r0:r0 + GQA_ROWS, MLA_W + p * LANES:MLA_W + (p + 1) * LANES] = blkv.astype(BF16)


def _attn_call(qm, gq, km, vm, gk, gv, cache, layer):
    has_cache = cache is not None
    t = qm.shape[0]
    widths = (MLA_QW, GQA_W, MLA_QW, MLA_W, LANES, LANES)
    if has_cache:
        nq = DEC_SEQ // TQ_DEC
        grid = (DEC_BATCH, nq)
        qmap = lambda b, i: (b * nq + i, 0)
        kmap = lambda b, i: (b, 0)
        once = pl.Buffered(1)
        in_specs = [pl.BlockSpec((TQ_DEC, widths[0]), qmap), pl.BlockSpec((TQ_DEC, widths[1]), qmap)]
        in_specs += [pl.BlockSpec((DEC_SEQ, wd), kmap, pipeline_mode=once) for wd in widths[2:]]
        in_specs += [pl.BlockSpec((PAST_LEN, wd), kmap, pipeline_mode=once) for wd in widths[2:4]]
        in_specs += [pl.BlockSpec((None, None, PAST_LEN, LANES), lambda b, i: (b, layer, 0, 0),
                                  pipeline_mode=once)] * 2
        args = [qm, gq, km, vm, gk, gv, *cache]
        out_spec = pl.BlockSpec((TQ_DEC, MLA_W + GQA_W), qmap)
        body = _attn_dec_kernel
        scratch = [pltpu.VMEM(((MLA_HEADS + GQA_KV_HEADS) * VT_ROWS, PAST_LEN + DEC_SEQ), BF16)]
        sem = ("arbitrary", "arbitrary")
    else:
        grid = (t // SEQ,)
        bmap = lambda b: (b, 0)
        in_specs = [pl.BlockSpec((SEQ, wd), bmap) for wd in widths]
        args = [qm, gq, km, vm, gk, gv]
        out_spec = pl.BlockSpec((SEQ, MLA_W + GQA_W), bmap)
        body = functools.partial(_attn_kernel, False)
        scratch = []
        sem = ("arbitrary",)
    return pl.pallas_call(
        body,
        grid=grid,
        in_specs=in_specs,
        out_specs=out_spec,
        out_shape=jax.ShapeDtypeStruct((t, MLA_W + GQA_W), BF16),
        scratch_shapes=scratch,
        compiler_params=_cparams(sem),
        name="attn_dec" if has_cache else "attn_ctx",
    )(*args)


def _ffn_kernel(layer, dec, hg_ref, at_ref, x_ref, mod_ref, w_out_ref, ln1g_ref, ln1b_ref, w_gu_ref, w_dn_ref,
                ln2g_ref, ln2b_ref, o_ref):
    lrow = slice(layer, layer + 1)
    mod = _mod_row(mod_ref, dec, DEC_SEQ // TM_FFN)
    g1 = mod[:, 2 * D_MODEL:3 * D_MODEL]
    sh2 = mod[:, 3 * D_MODEL:4 * D_MODEL]
    sc2 = mod[:, 4 * D_MODEL:5 * D_MODEL]
    g2 = mod[:, 5 * D_MODEL:6 * D_MODEL]
    y = _dot(hg_ref[...], w_out_ref[0:HG_W, :]) + _dot(at_ref[...], w_out_ref[HG_W:D_MODEL, :])
    x1 = _layernorm_rows(ALPHA * x_ref[...] + g1 * y, ln1g_ref[lrow, :], ln1b_ref[lrow, :])
    hf = (x1 * (1.0 + sc2) + sh2).astype(BF16)
    ch = D_FF // FF_CHUNKS
    f = None
    for c in range(FF_CHUNKS):
        gate = _dot(hf, w_gu_ref[:, c * ch:(c + 1) * ch])
        up = _dot(hf, w_gu_ref[:, D_FF + c * ch:D_FF + (c + 1) * ch])
        part = _dot((_silu(gate) * up).astype(BF16), w_dn_ref[c * ch:(c + 1) * ch, :])
        f = part if f is None else f + part
    o_ref[...] = _layernorm_rows(ALPHA * x1 + g2 * f, ln2g_ref[lrow, :], ln2b_ref[lrow, :])


def _ffn_call(hg, at, x, mod, w, prm, layer, dec):
    t = x.shape[0]
    row = lambda i: (i, 0)
    once = pl.Buffered(1)
    vec = pl.BlockSpec((DEPTH, D_MODEL), lambda i: (0, 0), pipeline_mode=once)
    return pl.pallas_call(
        functools.partial(_ffn_kernel, layer, dec),
        grid=(t // TM_FFN,),
        in_specs=[
            pl.BlockSpec((TM_FFN, HG_W), row),
            pl.BlockSpec((TM_FFN, MLA_W + GQA_W), row),
            pl.BlockSpec((TM_FFN, D_MODEL), row),
            _layer_spec((MOD_ROWS, N_ADA), layer, pipeline_mode=once),
            _layer_spec((D_MODEL, D_MODEL), layer, pipeline_mode=once),
            vec, vec,
            _layer_spec((D_MODEL, 2 * D_FF), layer, pipeline_mode=once),
            _layer_spec((D_FF, D_MODEL), layer, pipeline_mode=once),
            vec, vec,
        ],
        out_specs=pl.BlockSpec((TM_FFN, D_MODEL), row),
        out_shape=jax.ShapeDtypeStruct((t, D_MODEL), F32),
        compiler_params=_cparams(("arbitrary",)),
        name="ffn_dec" if dec else "ffn_ctx",
    )(hg, at, x, mod, w["w_out"], prm["ln1_g"], prm["ln1_b"], w["w_gu"], w["w_dn"], prm["ln2_g"], prm["ln2_b"])


def _rope_tables(width, rot_off, rot_dim):
    t = np.arange(DEC_SEQ)
    pos = (t // GRID_W, t % GRID_W)
    half = rot_dim // 4
    freqs = ROPE_THETA ** (-np.arange(half, dtype=np.float64) / half)
    c = np.ones((DEC_SEQ, LANES), np.float64)
    sa = np.zeros((DEC_SEQ, LANES), np.float64)
    sb = np.zeros((DEC_SEQ, LANES), np.float64)
    for blk in range(LANES // width):
        for axis in range(2):
            base = blk * width + rot_off + axis * 2 * half
            ang = pos[axis][:, None].astype(np.float64) * freqs[None, :]
            c[:, base:base + half] = np.cos(ang)
            c[:, base + half:base + 2 * half] = np.cos(ang)
            sa[:, base:base + half] = -np.sin(ang)
            sb[:, base + half:base + 2 * half] = np.sin(ang)
    return tuple(jnp.asarray(a, F32) for a in (c, sa, sb))


def _constants():
    C = HG_CHUNK
    t = np.arange(C)
    tri_f = (t[None, :] <= t[:, None]).astype(np.float32)
    tri_b = (t[None, :] >= t[:, None]).astype(np.float32)
    x = t[:, None] ^ t[None, :]
    top = np.where(x > 0, np.floor(np.log2(np.maximum(x, 1))).astype(np.int32), -1)
    same = (t[:, None] // SUBLANES) == (t[None, :] // SUBLANES)
    lvl_f = np.where(same, np.where(t[:, None] >= t[None, :], 0, -1),
                     np.where(t[:, None] > t[None, :], top, -1)).astype(np.int32)
    lvl_b = np.where(same, np.where(t[:, None] <= t[None, :], 0, -1),
                     np.where(t[:, None] < t[None, :], top, -1)).astype(np.int32)
    grp = np.arange(LANES) // HG_DK
    blk = (grp[:, None] == grp[None, :]).astype(np.float32)
    stk = np.zeros((SUBLANES, LANES, 2, C), np.float32)
    for s in range(SUBLANES):
        for hb in range(2):
            stk[s, hb * HG_DK:(hb + 1) * HG_DK, hb, s::SUBLANES] = 1.0
    stk = stk.reshape(SUBLANES * LANES, 2 * C)
    place = np.zeros((MLA_ROPE, MLA_QW), np.float32)
    for hd in range(MLA_HEADS):
        place[np.arange(MLA_ROPE), hd * HEAD_PAD + MLA_NOPE + np.arange(MLA_ROPE)] = 1.0
    mla_c, mla_sa, mla_sb = _rope_tables(HEAD_PAD, MLA_NOPE, MLA_ROPE)
    gqa_c, gqa_sa, gqa_sb = _rope_tables(GQA_HD, 0, GQA_HD)
    return {
        "tri_f": jnp.asarray(tri_f, BF16), "tri_b": jnp.asarray(tri_b, BF16),
        "lvl_f": jnp.asarray(np.concatenate([lvl_f, lvl_f], axis=0)),
        "lvl_b": jnp.asarray(np.concatenate([lvl_b, lvl_b], axis=0)),
        "stk": jnp.asarray(stk, BF16), "ones": jnp.asarray(blk, BF16), "ones64": jnp.asarray(blk / GQA_HD, BF16),
        "blk": jnp.asarray(blk, F32), "kpe_place": jnp.asarray(place, BF16),
        "mla_c": mla_c, "mla_sa": mla_sa, "mla_sb": mla_sb, "gqa_c": gqa_c, "gqa_sa": gqa_sa, "gqa_sb": gqa_sb,
    }


def kernel(x_prompt, x_sample, cache_mla_ckv, cache_mla_kpe, cache_gqa_k, cache_gqa_v, state_hgrn, c, c_ctx, w_ada,
           b_ada, w_in, hg_lb, hg_norm, mla_q_norm, mla_w_uq, mla_kv_norm, mla_w_ukv, gqa_q_norm, gqa_k_norm, w_out,
           ln1_g, ln1_b, w_ffn_in, w_ffn_out, ln2_g, ln2_b):
    cst = _constants()
    prm = {"mla_q_norm": mla_q_norm, "mla_kv_norm": mla_kv_norm, "gqa_q_norm": gqa_q_norm, "gqa_k_norm": gqa_k_norm,
           "ln1_g": ln1_g, "ln1_b": ln1_b, "ln2_g": ln2_g, "ln2_b": ln2_b}
    w = {"w_in": _prep_in_call(w_in)}
    w["w_uq"], w["w_uk"], w["w_uv"] = _prep_mla_call(mla_w_uq, mla_w_ukv)
    w["w_out"], w["w_gu"], w["w_dn"] = _prep_cast_call(w_out, w_ffn_in, w_ffn_out)

    cvec = jnp.concatenate([c_ctx[None, :], c, jnp.zeros((MOD_ROWS - 1 - DEC_BATCH, D_MODEL), F32)], axis=0)
    mod = _ada_call(cvec, w_ada, b_ada)
    cache_gk = cache_gqa_k.reshape(DEC_BATCH, DEPTH, PAST_LEN, LANES)
    cache_gv = cache_gqa_v.reshape(DEC_BATCH, DEPTH, PAST_LEN, LANES)

    x_ctx = x_prompt.reshape(BATCH * SEQ, D_MODEL)
    x_dec = x_sample.reshape(DEC_BATCH * DEC_SEQ, D_MODEL)
    states, hg_state = None, None
    for l in range(DEPTH):
        zhf, zhb, qm, km, vm, gq, gk, gv, *states = _inproj_call(x_ctx, mod, w, prm, cst, l, rope=False,
                                                          prev_states=states)
        hg_out, hg_state = _hgrn_call(zhf, zhb, hg_lb, hg_norm, cst, l, BATCH, SEQ, prev_state_out=hg_state)
        at = _attn_call(qm, gq, km, vm, gk, gv, None, l)
        x_ctx = _ffn_call(hg_out, at, x_ctx, mod, w, prm, l, dec=False)

        zhf, zhb, qm, km, vm, gq, gk, gv = _inproj_call(x_dec, mod, w, prm, cst, l, rope=True)
        hg_out = _hgrn_call(zhf, zhb, hg_lb, hg_norm, cst, l, DEC_BATCH, DEC_SEQ, state_in=state_hgrn)[0]
        kc, vc = _cachekv_call(cache_mla_ckv, cache_mla_kpe, w, cst, l)
        at = _attn_call(qm, gq, km, vm, gk, gv, (kc, vc, cache_gk, cache_gv), l)
        x_dec = _ffn_call(hg_out, at, x_dec, mod, w, prm, l, dec=True)

    ckv, kpe, gk32, gv32 = states
    kv_shape = (BATCH, DEPTH, SEQ, GQA_KV_HEADS, GQA_HD)
    return (x_ctx.reshape(BATCH, SEQ, D_MODEL), x_dec.reshape(DEC_BATCH, DEC_SEQ, D_MODEL),
            ckv, kpe, gk32.reshape(kv_shape), gv32.reshape(kv_shape), hg_state)
```

```python
import functools

import numpy as np
import jax
import jax.numpy as jnp
from jax import lax
from jax.experimental import pallas as pl
from jax.experimental.pallas import tpu as pltpu

F32 = jnp.float32
BF16 = jnp.bfloat16

D_MODEL = 1024
BATCH = 32
SEQ = 256
DEPTH = 2
DEC_BATCH = 2
DEC_SEQ = 2048
PAST_LEN = 256
GRID_W = 64
ROPE_THETA = 10000.0
EPS = 1e-6
GATE_FLOOR = 1e-30
HG_HEADS = 4
HG_DK = 64
HG_DV = 64
MLA_HEADS = 6
MLA_NOPE = 64
MLA_ROPE = 32
MLA_V = 64
MLA_Q_RANK = 256
MLA_KV_RANK = 128
GQA_HEADS = 6
GQA_KV_HEADS = 2
GQA_HD = 64
HG_W = HG_HEADS * HG_DV
MLA_W = MLA_HEADS * MLA_V
GQA_W = GQA_HEADS * GQA_HD
D_FF = 2816
MLA_SCALE = (MLA_NOPE + MLA_ROPE) ** -0.5
GQA_SCALE = GQA_HD ** -0.5
ALPHA = (2 * DEPTH) ** 0.25
LOG2E = 1.4426950408889634

LANES = 128
SUBLANES = 8
HEAD_PAD = 128
MLA_QW = MLA_HEADS * HEAD_PAD
N_ADA = 6 * D_MODEL
MOD_ROWS = 8

N_IN = 5 * HG_HEADS * HG_DK + MLA_Q_RANK + MLA_KV_RANK + MLA_ROPE + GQA_W + 2 * GQA_KV_HEADS * GQA_HD
SRC_KPE = 5 * HG_HEADS * HG_DK + MLA_Q_RANK + MLA_KV_RANK
SRC_GQ = SRC_KPE + MLA_ROPE
SRC_GK = SRC_GQ + GQA_W
SRC_GV = SRC_GK + GQA_KV_HEADS * GQA_HD
HG_COLS = 5 * HG_HEADS * HG_DK
OFF_MQ = HG_COLS
OFF_MKV = OFF_MQ + MLA_Q_RANK
OFF_GQ = OFF_MKV + MLA_KV_RANK
OFF_GK = OFF_GQ + GQA_W
OFF_GV = OFF_GK + GQA_KV_HEADS * GQA_HD
OFF_KPE_PAD = OFF_GV + GQA_KV_HEADS * GQA_HD
OFF_KPE = OFF_KPE_PAD + HEAD_PAD
N_Z = OFF_KPE + MLA_ROPE

TM_IN = 512
TM_FFN = 512
TQ_DEC = 512
GQA_ROWS = 256
ATT_GROUP_CTX = 9
ATT_GROUP_DEC = 1
VT_ROWS = 80
HG_CHUNK = 128
HG_UNROLL = 2
FF_CHUNKS = 2
PREP_STEPS = 4
VMEM_LIMIT = 56 * 1024 * 1024


def _cparams(sem):
    return pltpu.CompilerParams(dimension_semantics=sem, vmem_limit_bytes=VMEM_LIMIT)


def _const_spec(shape):
    nd = len(shape)
    return pl.BlockSpec(shape, lambda *_: (0,) * nd)


def _layer_spec(shape, layer, **kw):
    nd = len(shape)
    return pl.BlockSpec((None,) + tuple(shape), lambda *_: (layer,) + (0,) * nd, **kw)


_ANY = pl.BlockSpec(memory_space=pl.ANY)


def _silu(x):
    return x * jax.nn.sigmoid(x)


def _dot(a, b):
    return jnp.dot(a, b, preferred_element_type=F32)


def _dot_nt(a, b):
    return lax.dot_general(a, b, (((1,), (1,)), ((), ())), preferred_element_type=F32)


def _dot_tn(a, b):
    return lax.dot_general(a, b, (((0,), (0,)), ((), ())), preferred_element_type=F32)


def _rms_rows(x, g):
    y = x * lax.rsqrt(jnp.mean(x * x, axis=-1, keepdims=True) + EPS)
    return y * g


def _layernorm_rows(x, g, b):
    xc = x - jnp.mean(x, axis=-1, keepdims=True)
    var = jnp.mean(xc * xc, axis=-1, keepdims=True)
    return xc * lax.rsqrt(var + EPS) * g + b


def _group_mean(x, ones_blk):
    return _dot(x.astype(BF16), ones_blk)


def _twice(g):
    return jnp.concatenate([g, g], axis=1)


_GQ_HEAD_ORDER = (0, 3, 1, 4, 2, 5)


def _prep_in_kernel(w_ref, o_ref):
    x = w_ref[...]
    rows = x.shape[0]
    kr = x[:, SRC_KPE:SRC_GQ]
    gq = [x[:, SRC_GQ + h * GQA_HD:SRC_GQ + (h + 1) * GQA_HD] for h in _GQ_HEAD_ORDER]
    pieces = [x[:, 0:SRC_KPE]] + gq + [
        x[:, SRC_GK:SRC_GV], x[:, SRC_GV:N_IN],
        jnp.zeros((rows, MLA_NOPE), F32), kr, jnp.zeros((rows, HEAD_PAD - MLA_NOPE - MLA_ROPE), F32), kr]
    o_ref[...] = jnp.concatenate(pieces, axis=1).astype(BF16)


def _prep_in_call(w_in):
    rb = 128
    return pl.pallas_call(
        _prep_in_kernel,
        grid=(DEPTH, D_MODEL // rb),
        in_specs=[pl.BlockSpec((None, rb, N_IN), lambda l, j: (l, j, 0))],
        out_specs=pl.BlockSpec((None, rb, N_Z), lambda l, j: (l, j, 0)),
        out_shape=jax.ShapeDtypeStruct((DEPTH, D_MODEL, N_Z), BF16),
        compiler_params=_cparams(("arbitrary", "arbitrary")),
        name="prep_in",
    )(w_in)


def _prep_mla_kernel(uq_ref, ukv_ref, oq_ref, ok_ref, ov_ref):
    uq = uq_ref[...]
    dq = MLA_NOPE + MLA_ROPE
    zq = jnp.zeros((MLA_Q_RANK, HEAD_PAD - dq), F32)
    oq_ref[...] = jnp.concatenate(
        [p for h in range(MLA_HEADS) for p in (uq[:, h * dq:(h + 1) * dq], zq)], axis=1).astype(BF16)
    ukv = ukv_ref[...]
    lane = lax.broadcasted_iota(jnp.int32, ukv.shape, 1)
    ok_ref[...] = jnp.where((lane & (HEAD_PAD - 1)) < MLA_NOPE, ukv, 0.0).astype(BF16)
    ov_ref[...] = jnp.concatenate(
        [ukv[:, h * HEAD_PAD + MLA_NOPE:(h + 1) * HEAD_PAD] for h in range(MLA_HEADS)], axis=1).astype(BF16)


def _prep_mla_call(w_uq, w_ukv):
    lay = lambda n: pl.BlockSpec((None, n[0], n[1]), lambda l: (l, 0, 0))
    shapes = [(MLA_Q_RANK, MLA_QW), (MLA_KV_RANK, MLA_QW), (MLA_KV_RANK, MLA_W)]
    return pl.pallas_call(
        _prep_mla_kernel,
        grid=(DEPTH,),
        in_specs=[lay((MLA_Q_RANK, MLA_HEADS * (MLA_NOPE + MLA_ROPE))), lay((MLA_KV_RANK, MLA_QW))],
        out_specs=[lay(s) for s in shapes],
        out_shape=[jax.ShapeDtypeStruct((DEPTH,) + s, BF16) for s in shapes],
        compiler_params=_cparams(("arbitrary",)),
        name="prep_mla",
    )(w_uq, w_ukv)


def _prep_cast_kernel(a_ref, b_ref, c_ref, oa_ref, ob_ref, oc_ref):
    oa_ref[...] = a_ref[...].astype(BF16)
    ob_ref[...] = b_ref[...].astype(BF16)
    oc_ref[...] = c_ref[...].astype(BF16)


def _prep_cast_call(w_out, w_ffn_in, w_ffn_out):
    ws = (w_out, w_ffn_in, w_ffn_out)
    specs = [pl.BlockSpec((None, w.shape[1] // PREP_STEPS, w.shape[2]), lambda l, j: (l, j, 0)) for w in ws]
    return pl.pallas_call(
        _prep_cast_kernel,
        grid=(DEPTH, PREP_STEPS),
        in_specs=specs,
        out_specs=specs,
        out_shape=[jax.ShapeDtypeStruct(w.shape, BF16) for w in ws],
        compiler_params=_cparams(("arbitrary", "arbitrary")),
        name="prep_cast",
    )(*ws)


def _ada_kernel(c_ref, w_ref, b_ref, o_ref):
    s = _silu(c_ref[...]).astype(BF16)
    o_ref[...] = _dot(s, w_ref[...].astype(BF16)) + b_ref[pl.ds(pl.program_id(0), 1), :]


def _ada_call(cvec, w_ada, b_ada):
    tn = 1536
    return pl.pallas_call(
        _ada_kernel,
        grid=(DEPTH, N_ADA // tn),
        in_specs=[
            pl.BlockSpec((MOD_ROWS, D_MODEL), lambda l, j: (0, 0)),
            pl.BlockSpec((None, D_MODEL, tn), lambda l, j: (l, 0, j)),
            pl.BlockSpec((DEPTH, tn), lambda l, j: (0, j)),
        ],
        out_specs=pl.BlockSpec((None, MOD_ROWS, tn), lambda l, j: (l, 0, j)),
        out_shape=jax.ShapeDtypeStruct((DEPTH, MOD_ROWS, N_ADA), F32),
        compiler_params=_cparams(("arbitrary", "arbitrary")),
        name="ada_mod",
    )(cvec, w_ada, b_ada)


def _mod_row(mod_ref, dec, tiles_per_seq):
    if not dec:
        return mod_ref[0:1, :]
    return mod_ref[pl.ds(1 + pl.program_id(0) // tiles_per_seq, 1), :]


def _rope_block(x, c, sa, sb, half):
    return x * c + pltpu.roll(x, LANES - half, 1) * sa + pltpu.roll(x, half, 1) * sb


def _inproj_kernel(layer, rope, x_ref, mod_ref, w_in_ref, w_uq_ref, w_uk_ref, w_uv_ref, qn_ref, kvn_ref, gqn_ref,
                   gkn_ref, ones_ref, *rest):
    if rope:
        (mc_ref, msa_ref, msb_ref, gc_ref, gsa_ref, gsb_ref,
         zhf_ref, zhb_ref, qm_ref, km_ref, vm_ref, gq_ref, gk_ref, gv_ref) = rest
    else:
        rest = rest[-12:]
        (zhf_ref, zhb_ref, qm_ref, km_ref, vm_ref, gq_ref, gk_ref, gv_ref,
         ckv_ref, kpe_ref, gk32_ref, gv32_ref) = rest
    lrow = slice(layer, layer + 1)
    qk_base = LOG2E
    mod = _mod_row(mod_ref, rope, DEC_SEQ // TM_IN)
    sh1 = mod[:, 0:D_MODEL]
    sc1 = mod[:, D_MODEL:2 * D_MODEL]
    h = (x_ref[...] * (1.0 + sc1) + sh1).astype(BF16)
    z = _dot(h, w_in_ref[...])
    gw = HG_HEADS * HG_DK
    zhf_ref[...] = z[:, gw:3 * gw]
    zhb_ref[:, 0:gw] = z[:, 0:gw].astype(BF16)
    zhb_ref[:, gw:3 * gw] = z[:, 3 * gw:HG_COLS].astype(BF16)
    mq = z[:, OFF_MQ:OFF_MKV]
    mkv = z[:, OFF_MKV:OFF_GQ]
    gq = z[:, OFF_GQ:OFF_GK]
    gk = z[:, OFF_GK:OFF_GV]
    gv = z[:, OFF_GV:OFF_KPE_PAD]
    kpe_pad = z[:, OFF_KPE_PAD:OFF_KPE]
    ones = ones_ref[...]
    gqn = _twice(gqn_ref[lrow, :])
    gkn_g = _twice(gkn_ref[lrow, :])

    q = _dot(_rms_rows(mq, qn_ref[lrow, :]).astype(BF16), w_uq_ref[...])
    ckv = _rms_rows(mkv, kvn_ref[lrow, :])
    ckv_b = ckv.astype(BF16)
    kn = _dot(ckv_b, w_uk_ref[...])
    vm_ref[...] = _dot(ckv_b, w_uv_ref[...]).astype(BF16)
    if rope:
        mc, msa, msb = mc_ref[...], msa_ref[...], msb_ref[...]
        kpe_pad = _rope_block(kpe_pad, mc, msa, msb, MLA_ROPE // 4)
    for hd in range(MLA_HEADS):
        sl = slice(hd * HEAD_PAD, (hd + 1) * HEAD_PAD)
        qh = q[:, sl]
        if rope:
            qh = _rope_block(qh, mc, msa, msb, MLA_ROPE // 4)
        qm_ref[:, sl] = (qh * (MLA_SCALE * qk_base)).astype(BF16)
        km_ref[:, sl] = (kn[:, sl] + kpe_pad).astype(BF16)

    gkn = gk * lax.rsqrt(_group_mean(gk * gk, ones) + EPS) * gkn_g
    for p in range(GQA_W // LANES):
        sl = slice(p * LANES, (p + 1) * LANES)
        gqp = gq[:, sl]
        gqp = gqp * lax.rsqrt(_group_mean(gqp * gqp, ones) + EPS) * gqn
        if rope:
            gqp = _rope_block(gqp, gc_ref[...], gsa_ref[...], gsb_ref[...], GQA_HD // 4)
        gq_ref[:, sl] = (gqp * (GQA_SCALE * qk_base)).astype(BF16)
    if rope:
        gk_ref[...] = _rope_block(gkn, gc_ref[...], gsa_ref[...], gsb_ref[...], GQA_HD // 4).astype(BF16)
    else:
        gk_ref[...] = gkn.astype(BF16)
        kpe = z[:, OFF_KPE:N_Z]
        for b in range(TM_IN // SEQ):
            rows = slice(b * SEQ, (b + 1) * SEQ)
            ckv_ref[b] = ckv[rows]
            kpe_ref[b] = kpe[rows]
            gk32_ref[b] = gkn[rows]
            gv32_ref[b] = gv[rows]
    gv_ref[...] = gv.astype(BF16)


def _inproj_call(x, mod, w, prm, cst, layer, rope, prev_states=None):
    t = x.shape[0]
    nt = t // TM_IN
    row = lambda i: (i, 0)
    in_specs = [
        pl.BlockSpec((TM_IN, D_MODEL), row),
        _layer_spec((MOD_ROWS, N_ADA), layer),
        _layer_spec((D_MODEL, N_Z), layer),
        _layer_spec((MLA_Q_RANK, MLA_QW), layer),
        _layer_spec((MLA_KV_RANK, MLA_QW), layer),
        _layer_spec((MLA_KV_RANK, MLA_W), layer),
        _const_spec((DEPTH, MLA_Q_RANK)),
        _const_spec((DEPTH, MLA_KV_RANK)),
        _const_spec((DEPTH, GQA_HD)),
        _const_spec((DEPTH, GQA_HD)),
        _const_spec((LANES, LANES)),
    ]
    args = [x, mod, w["w_in"], w["w_uq"], w["w_uk"], w["w_uv"], prm["mla_q_norm"], prm["mla_kv_norm"],
            prm["gqa_q_norm"], prm["gqa_k_norm"], cst["ones64"]]
    outs = [(2 * HG_HEADS * HG_DK, F32), (3 * HG_HEADS * HG_DK, BF16), (MLA_QW, BF16), (MLA_QW, BF16), (MLA_W, BF16),
            (GQA_W, BF16), (LANES, BF16), (LANES, BF16)]
    out_specs = [pl.BlockSpec((TM_IN, wd), row) for wd, _ in outs]
    out_shape = [jax.ShapeDtypeStruct((t, wd), dt) for wd, dt in outs]
    aliases = {}
    if rope:
        tab = lambda i: (i % (DEC_SEQ // TM_IN), 0)
        in_specs += [pl.BlockSpec((TM_IN, LANES), tab)] * 6
        args += [cst["mla_c"], cst["mla_sa"], cst["mla_sb"], cst["gqa_c"], cst["gqa_sa"], cst["gqa_sb"]]
    else:
        per = TM_IN // SEQ
        for wd in (MLA_KV_RANK, MLA_ROPE, LANES, LANES):
            out_specs.append(pl.BlockSpec((per, None, SEQ, wd), lambda i: (i, layer, 0, 0)))
            out_shape.append(jax.ShapeDtypeStruct((BATCH, DEPTH, SEQ, wd), F32))
        if prev_states is not None:
            for k, arr in enumerate(prev_states):
                aliases[len(args)] = len(outs) + k
                in_specs.append(_ANY)
                args.append(arr)
    return pl.pallas_call(
        functools.partial(_inproj_kernel, layer, rope),
        grid=(nt,),
        in_specs=in_specs,
        out_specs=out_specs,
        out_shape=out_shape,
        input_output_aliases=aliases,
        compiler_params=_cparams(("arbitrary",)),
        name="inproj_dec" if rope else "inproj_ctx",
    )(*args)


def _cachekv_kernel(ckv_ref, kpe_ref, w_uk_ref, w_uv_ref, place_ref, k_ref, v_ref):
    ckv_b = ckv_ref[...].astype(BF16)
    k_ref[...] = (_dot(ckv_b, w_uk_ref[...]) + _dot(kpe_ref[...].astype(BF16), place_ref[...])).astype(BF16)
    v_ref[...] = _dot(ckv_b, w_uv_ref[...]).astype(BF16)


def _cachekv_call(cache_ckv, cache_kpe, w, cst, layer):
    t = DEC_BATCH * PAST_LEN
    cache = lambda wd: pl.BlockSpec((None, None, PAST_LEN, wd), lambda b: (b, layer, 0, 0))
    return pl.pallas_call(
        _cachekv_kernel,
        grid=(DEC_BATCH,),
        in_specs=[cache(MLA_KV_RANK), cache(MLA_ROPE),
                  _layer_spec((MLA_KV_RANK, MLA_QW), layer), _layer_spec((MLA_KV_RANK, MLA_W), layer),
                  _const_spec((MLA_ROPE, MLA_QW))],
        out_specs=[pl.BlockSpec((PAST_LEN, MLA_QW), lambda b: (b, 0)),
                   pl.BlockSpec((PAST_LEN, MLA_W), lambda b: (b, 0))],
        out_shape=[jax.ShapeDtypeStruct((t, MLA_QW), BF16), jax.ShapeDtypeStruct((t, MLA_W), BF16)],
        compiler_params=_cparams(("arbitrary",)),
        name="cache_kv",
    )(cache_ckv, cache_kpe, w["w_uk"], w["w_uv"], cst["kpe_place"])


def _bcast_row(x, r, rows):
    return jnp.broadcast_to(x[r:r + 1, :], (rows, x.shape[1]))


def _cumsum_rows(tri, x):
    hi = x.astype(BF16)
    r1 = x - hi.astype(F32)
    mid = r1.astype(BF16)
    lo = (r1 - mid.astype(F32)).astype(BF16)
    return _dot(tri, hi) + _dot(tri, mid) + _dot(tri, lo)


def _hgrn_chunks(items, states, tri, lvl, stk, blk, lane_lo):
    C = HG_CHUNK
    zero_b = jnp.zeros((SUBLANES, LANES), F32)
    trow = lax.broadcasted_iota(jnp.int32, (SUBLANES, LANES), 0)
    for it in items:
        it["c"] = _cumsum_rows(tri[it["fwd"]], it["lf"])
    for it in items:
        it["q_a"] = jnp.where(lane_lo, it["q"], 0.0)
        it["q_b"] = jnp.where(lane_lo, 0.0, it["q"])
        it["ck"] = it["c"] - jnp.log(it["kk"])

    m = C // 2
    bit = m.bit_length() - 1
    while m >= SUBLANES:
        for it in items:
            fwd, c = it["fwd"], it["c"]
            qa_rows, qb_rows, k_rows = [], [], []
            for b in range(C // m):
                rows = slice(b * m, (b + 1) * m)
                anchor = (b // 2) * 2 * m + (m - 1 if fwd else m)
                r = _bcast_row(c, anchor, m)
                if (b % 2 == 1) == fwd:
                    e = jnp.exp(c[rows] - r)
                    qa_rows.append(it["q_a"][rows] * e)
                    qb_rows.append(it["q_b"][rows] * e)
                    k_rows += [zero_b] * (m // SUBLANES)
                else:
                    e = jnp.exp(r - c[rows])
                    qa_rows += [zero_b] * (m // SUBLANES)
                    qb_rows += [zero_b] * (m // SUBLANES)
                    k_rows.append(it["kk"][rows] * e)
            prod = _dot_nt(jnp.concatenate(qa_rows + qb_rows, axis=0).astype(BF16),
                           jnp.concatenate(k_rows, axis=0).astype(BF16))
            it["a"] = prod if "a" not in it else jnp.where(lvl[fwd] == bit, prod, it["a"])
        m //= 2
        bit -= 1

    for it in items:
        it["w"] = []
    for j in range(C // SUBLANES):
        sl = slice(j * SUBLANES, (j + 1) * SUBLANES)
        for it in items:
            cs, cks, qs = it["c"][sl], it["ck"][sl], it["q"][sl]
            w_s = []
            for s in range(SUBLANES):
                e = jnp.exp(cs - _bcast_row(cks, s, SUBLANES))
                keep = (trow >= s) if it["fwd"] else (trow <= s)
                w_s.append(jnp.where(keep, qs * e, 0.0))
            it["w"].append(jnp.concatenate(w_s, axis=1))
    for it in items:
        red = _dot(jnp.concatenate(it["w"], axis=0).astype(BF16), stk)
        it["a"] = jnp.where(lvl[it["fwd"]] == 0,
                            jnp.concatenate([red[:, 0:LANES], red[:, LANES:2 * LANES]], axis=0), it["a"])
    for it in items:
        off = _dot(it["a"].astype(BF16), it["iv"].astype(BF16))
        it["o"] = jnp.where(lane_lo, off[0:C], off[C:2 * C])
        it["edge"] = _bcast_row(it["c"], C - 1 if it["fwd"] else 0, C)
        it["q_dec"] = (it["q"] * jnp.exp(it["c"])).astype(BF16)
        k_dec = (it["kk"] * jnp.exp(it["edge"] - it["c"])).astype(BF16)
        it["upd"] = _dot_tn(it["iv"].astype(BF16), k_dec) * blk

    states = list(states)
    outs = []
    for it in items:
        st = states[it["sid"]]
        outs.append(it["o"] + _dot_nt(it["q_dec"], st.astype(BF16)))
        states[it["sid"]] = st * jnp.exp(it["edge"][0:1]) + it["upd"]
    return outs, states


def _hgrn_kernel(layer, nchunks, has_state, zq_ref, zff_ref, zfb_ref, zi_ref, zg_ref, lb_ref, g_ref,
                 trif_ref, trib_ref, lvlf_ref, lvlb_ref, stk_ref, ones_ref, blk_ref, *rest):
    if has_state:
        s0_ref, o_ref, of_scr, ob_scr = rest
    else:
        o_ref, st_ref, of_scr, ob_scr = rest[-4:]
    C = HG_CHUNK
    lane_lo = lax.broadcasted_iota(jnp.int32, (1, LANES), 1) < HG_DK

    lb_all = [lb_ref[j] for j in range(DEPTH)]
    mx = functools.reduce(jnp.maximum, lb_all)
    ex = [jnp.exp(v - mx) for v in lb_all]
    den = functools.reduce(lambda a, b: a + b, ex)
    sm = [v / den for v in ex]
    lbs = functools.reduce(lambda a, b: a + b, sm[:layer + 1]) - sm[0]
    lbf, lbb = lbs[0:1], lbs[1:2]

    def gates(zf, lbd):
        f = lbd + (1.0 - lbd) * jax.nn.sigmoid(zf)
        return jnp.log(jnp.maximum(f, GATE_FLOOR)), jnp.maximum(1.0 - f, 0.0)

    def block_diag_t(s_ref_d):
        za = jnp.zeros((HG_DK, HG_DV), F32)
        top = jnp.concatenate([s_ref_d[0], za], axis=1)
        bot = jnp.concatenate([za, s_ref_d[1]], axis=1)
        return jnp.concatenate([top, bot], axis=0).T

    if has_state:
        st_f0 = block_diag_t(s0_ref[0])
        st_b0 = block_diag_t(s0_ref[1])
    else:
        st_f0 = jnp.zeros((LANES, LANES), F32)
        st_b0 = st_f0

    def body(j, carry):
        tri = {True: trif_ref[...], False: trib_ref[...]}
        lvl = {True: lvlf_ref[...], False: lvlb_ref[...]}
        items = []
        for u in range(HG_UNROLL):
            jf = j * HG_UNROLL + u
            for fwd in (True, False):
                rows = pl.ds(pl.multiple_of((jf if fwd else nchunks - 1 - jf) * C, C), C)
                lf, kk = gates((zff_ref if fwd else zfb_ref)[rows, :], lbf if fwd else lbb)
                items.append({"fwd": fwd, "sid": 0 if fwd else 1, "rows": rows, "lf": lf, "kk": kk,
                              "q": _silu(zq_ref[rows, :].astype(F32)) * (HG_DK ** -0.5), "iv": zi_ref[rows, :]})
        outs, states = _hgrn_chunks(items, carry, tri, lvl, stk_ref[...], blk_ref[...], lane_lo)
        for it, o in zip(items, outs):
            (of_scr if it["fwd"] else ob_scr)[it["rows"], :] = o
        return tuple(states)

    st_f, st_b = lax.fori_loop(0, nchunks // HG_UNROLL, body, (st_f0, st_b0))

    o = of_scr[...] + ob_scr[...]
    ms = _group_mean(o * o, ones_ref[...] * (1.0 / HG_DV))
    g = _twice(g_ref[layer:layer + 1, :])
    o_ref[...] = (o * lax.rsqrt(ms + EPS) * g * _silu(zg_ref[...].astype(F32))).astype(BF16)
    if not has_state:
        for d, st in enumerate((st_f, st_b)):
            s_kv = st.T
            st_ref[d, 0] = s_kv[0:HG_DK, 0:HG_DV]
            st_ref[d, 1] = s_kv[HG_DK:2 * HG_DK, HG_DV:2 * HG_DV]


def _hgrn_call(zhf, zhb, lb, g, cst, layer, nb, s, state_in=None, prev_state_out=None):
    has_state = state_in is not None
    nchunks = s // HG_CHUNK
    npair = HG_HEADS // 2
    part = lambda k: pl.BlockSpec((s, LANES), lambda b, hp: (b, k * npair + hp))
    st_spec = pl.BlockSpec((None, None, 2, 2, HG_DK, HG_DV), lambda b, hp: (b, layer, 0, hp, 0, 0))
    in_specs = [part(0), part(0), part(1), part(1), part(2),
                pl.BlockSpec((DEPTH, 2, LANES), lambda b, hp: (0, 0, hp)),
                _const_spec((DEPTH, HG_DV)),
                _const_spec((HG_CHUNK, HG_CHUNK)), _const_spec((HG_CHUNK, HG_CHUNK)),
                _const_spec((2 * HG_CHUNK, HG_CHUNK)), _const_spec((2 * HG_CHUNK, HG_CHUNK)),
                _const_spec((SUBLANES * LANES, 2 * LANES)),
                _const_spec((LANES, LANES)), _const_spec((LANES, LANES))]
    args = [zhb, zhf, zhf, zhb, zhb, lb, g, cst["tri_f"], cst["tri_b"], cst["lvl_f"], cst["lvl_b"], cst["stk"],
            cst["ones"], cst["blk"]]
    out_specs = [pl.BlockSpec((s, LANES), lambda b, hp: (b, hp))]
    out_shape = [jax.ShapeDtypeStruct((nb * s, HG_W), BF16)]
    aliases = {}
    if has_state:
        in_specs.append(st_spec)
        args.append(state_in)
    else:
        out_specs.append(st_spec)
        out_shape.append(jax.ShapeDtypeStruct((nb, DEPTH, 2, HG_HEADS, HG_DK, HG_DV), F32))
        if prev_state_out is not None:
            aliases[len(args)] = 1
            in_specs.append(_ANY)
            args.append(prev_state_out)
    return pl.pallas_call(
        functools.partial(_hgrn_kernel, layer, nchunks, has_state),
        grid=(nb, npair),
        in_specs=in_specs,
        out_specs=out_specs,
        out_shape=out_shape,
        input_output_aliases=aliases,
        scratch_shapes=[pltpu.VMEM((s, LANES), F32), pltpu.VMEM((s, LANES), F32)],
        compiler_params=_cparams(("arbitrary", "arbitrary")),
        name="hgrn_dec" if has_state else "hgrn_ctx",
    )(*args)


def _attend_t(jobs, ss):
    add = lambda a, b: a + b
    mxs = [functools.reduce(jnp.maximum, [jnp.max(s, axis=-1, keepdims=True) for s in sj]) for sj in ss]
    ebs = [[jnp.exp2(s - mx).astype(BF16) for s in sj] for sj, mx in zip(ss, mxs)]
    prods = [[functools.reduce(add, [_dot_nt(vt, e[rows]) for vt, e in zip(vts, ej)]) for rows, vts in parts]
             for ej, (_, _, parts) in zip(ebs, jobs)]
    return [[pr[0:MLA_V] * (1.0 / pr[MLA_V:MLA_V + 1]) for pr in pj] for pj in prods]


def _scores(jobs):
    return [[_dot_nt(q, k) for k in ks] for q, ks, _ in jobs]


def _attn_kernel(has_cache, qm_ref, gq_ref, km_ref, vm_ref, gk_ref, gv_ref, *rest):
    if has_cache:
        kc_ref, vc_ref, gkc_ref, gvc_ref, o_ref, vt_scr = rest
    else:
        o_ref, vt_scr = rest
    sq = qm_ref.shape[0]
    sk = vt_scr.shape[1]
    nblk = MLA_HEADS + GQA_KV_HEADS
    cols = [slice(0, PAST_LEN), slice(PAST_LEN, sk)] if has_cache else [slice(0, sk)]

    def fill():
        tr = lambda v: v.astype(F32).T.astype(BF16)
        srcs = [(vc_ref, gvc_ref), (vm_ref, gv_ref)] if has_cache else [(vm_ref, gv_ref)]
        for cl, (mla_ref, gqa_ref) in zip(cols, srcs):
            mla_t, gqa_t = tr(mla_ref[...]), tr(gqa_ref[...])
            tail = (lax.broadcasted_iota(jnp.int32, (VT_ROWS - MLA_V, cl.stop - cl.start), 0) == 0).astype(BF16)
            for b in range(nblk):
                src = mla_t[b * MLA_V:(b + 1) * MLA_V] if b < MLA_HEADS else \
                    gqa_t[(b - MLA_HEADS) * GQA_HD:(b - MLA_HEADS + 1) * GQA_HD]
                vt_scr[b * VT_ROWS:b * VT_ROWS + MLA_V, cl] = src
                vt_scr[b * VT_ROWS + MLA_V:(b + 1) * VT_ROWS, cl] = tail

    if has_cache:
        pl.when(pl.program_id(1) == 0)(fill)
    else:
        fill()

    def vt(b):
        return [vt_scr[b * VT_ROWS:(b + 1) * VT_ROWS, cl] for cl in cols]

    lo = lax.broadcasted_iota(jnp.int32, (GQA_ROWS, LANES), 1) < GQA_HD
    zero = jnp.zeros((GQA_ROWS, LANES), BF16)
    gks = [gkc_ref[...].astype(BF16), gk_ref[...]] if has_cache else [gk_ref[...]]
    jobs = []
    for hd in range(MLA_HEADS):
        sl = slice(hd * HEAD_PAD, (hd + 1) * HEAD_PAD)
        ks = [kc_ref[:, sl], km_ref[:, sl]] if has_cache else [km_ref[:, sl]]
        jobs.append((qm_ref[:, sl], ks, [(slice(0, sq), vt(hd))]))
    halves = [(slice(0, GQA_ROWS), vt(MLA_HEADS)), (slice(GQA_ROWS, 2 * GQA_ROWS), vt(MLA_HEADS + 1))]
    for r0 in range(0, sq, GQA_ROWS):
        for p in range(GQA_W // LANES):
            qp = gq_ref[r0:r0 + GQA_ROWS, p * LANES:(p + 1) * LANES]
            jobs.append((jnp.concatenate([jnp.where(lo, qp, zero), jnp.where(lo, zero, qp)], axis=0), gks, halves))
    group = ATT_GROUP_DEC if has_cache else ATT_GROUP_CTX
    groups = [jobs[j0:j0 + group] for j0 in range(0, len(jobs), group)]
    outs = []
    ss_next = _scores(groups[0])
    for g, grp in enumerate(groups):
        ss = ss_next
        if g + 1 < len(groups):
            ss_next = _scores(groups[g + 1])
        outs += _attend_t(grp, ss)
    for p in range(MLA_HEADS // 2):
        pair = jnp.concatenate([outs[2 * p][0], outs[2 * p + 1][0]], axis=0)
        o_ref[:, p * LANES:(p + 1) * LANES] = pair.T.astype(BF16)
    for i, r0 in enumerate(range(0, sq, GQA_ROWS)):
        gres = [jnp.concatenate(outs[MLA_HEADS + 3 * i + p], axis=0).T for p in range(GQA_W // LANES)]
        swapped = pltpu.roll(gres[1], GQA_HD, 1)
        blocks = (jnp.where(lo, gres[0], swapped), jnp.where(lo, gres[2], gres[0]), jnp.where(lo, swapped, gres[2]))
        for p, blkv in enumerate(blocks):
            o_ref[r0:r0 + GQA_ROWS, MLA_W + p * LANES:MLA_W + (p + 1) * LANES] = blkv.astype(BF16)


def _attn_call(qm, gq, km, vm, gk, gv, cache, layer):
    has_cache = cache is not None
    t = qm.shape[0]
    widths = (MLA_QW, GQA_W, MLA_QW, MLA_W, LANES, LANES)
    if has_cache:
        nq = DEC_SEQ // TQ_DEC
        grid = (DEC_BATCH, nq)
        qmap = lambda b, i: (b * nq + i, 0)
        kmap = lambda b, i: (b, 0)
        once = pl.Buffered(1)
        in_specs = [pl.BlockSpec((TQ_DEC, widths[0]), qmap), pl.BlockSpec((TQ_DEC, widths[1]), qmap)]
        in_specs += [pl.BlockSpec((DEC_SEQ, wd), kmap, pipeline_mode=once) for wd in widths[2:]]
        in_specs += [pl.BlockSpec((PAST_LEN, wd), kmap, pipeline_mode=once) for wd in widths[2:4]]
        in_specs += [pl.BlockSpec((None, None, PAST_LEN, LANES), lambda b, i: (b, layer, 0, 0),
                                  pipeline_mode=once)] * 2
        args = [qm, gq, km, vm, gk, gv, *cache]
        out_spec = pl.BlockSpec((TQ_DEC, MLA_W + GQA_W), qmap)
        sk = PAST_LEN + DEC_SEQ
        sem = ("arbitrary", "arbitrary")
    else:
        grid = (t // SEQ,)
        bmap = lambda b: (b, 0)
        in_specs = [pl.BlockSpec((SEQ, wd), bmap) for wd in widths]
        args = [qm, gq, km, vm, gk, gv]
        out_spec = pl.BlockSpec((SEQ, MLA_W + GQA_W), bmap)
        sk = SEQ
        sem = ("arbitrary",)
    return pl.pallas_call(
        functools.partial(_attn_kernel, has_cache),
        grid=grid,
        in_specs=in_specs,
        out_specs=out_spec,
        out_shape=jax.ShapeDtypeStruct((t, MLA_W + GQA_W), BF16),
        scratch_shapes=[pltpu.VMEM(((MLA_HEADS + GQA_KV_HEADS) * VT_ROWS, sk), BF16)],
        compiler_params=_cparams(sem),
        name="attn_dec" if has_cache else "attn_ctx",
    )(*args)


def _ffn_kernel(layer, dec, hg_ref, at_ref, x_ref, mod_ref, w_out_ref, ln1g_ref, ln1b_ref, w_gu_ref, w_dn_ref,
                ln2g_ref, ln2b_ref, o_ref):
    lrow = slice(layer, layer + 1)
    mod = _mod_row(mod_ref, dec, DEC_SEQ // TM_FFN)
    g1 = mod[:, 2 * D_MODEL:3 * D_MODEL]
    sh2 = mod[:, 3 * D_MODEL:4 * D_MODEL]
    sc2 = mod[:, 4 * D_MODEL:5 * D_MODEL]
    g2 = mod[:, 5 * D_MODEL:6 * D_MODEL]
    y = _dot(hg_ref[...], w_out_ref[0:HG_W, :]) + _dot(at_ref[...], w_out_ref[HG_W:D_MODEL, :])
    x1 = _layernorm_rows(ALPHA * x_ref[...] + g1 * y, ln1g_ref[lrow, :], ln1b_ref[lrow, :])
    hf = (x1 * (1.0 + sc2) + sh2).astype(BF16)
    ch = D_FF // FF_CHUNKS
    f = None
    for c in range(FF_CHUNKS):
        gate = _dot(hf, w_gu_ref[:, c * ch:(c + 1) * ch])
        up = _dot(hf, w_gu_ref[:, D_FF + c * ch:D_FF + (c + 1) * ch])
        part = _dot((_silu(gate) * up).astype(BF16), w_dn_ref[c * ch:(c + 1) * ch, :])
        f = part if f is None else f + part
    o_ref[...] = _layernorm_rows(ALPHA * x1 + g2 * f, ln2g_ref[lrow, :], ln2b_ref[lrow, :])


def _ffn_call(hg, at, x, mod, w, prm, layer, dec):
    t = x.shape[0]
    row = lambda i: (i, 0)
    once = pl.Buffered(1)
    vec = pl.BlockSpec((DEPTH, D_MODEL), lambda i: (0, 0), pipeline_mode=once)
    return pl.pallas_call(
        functools.partial(_ffn_kernel, layer, dec),
        grid=(t // TM_FFN,),
        in_specs=[
            pl.BlockSpec((TM_FFN, HG_W), row),
            pl.BlockSpec((TM_FFN, MLA_W + GQA_W), row),
            pl.BlockSpec((TM_FFN, D_MODEL), row),
            _layer_spec((MOD_ROWS, N_ADA), layer, pipeline_mode=once),
            _layer_spec((D_MODEL, D_MODEL), layer, pipeline_mode=once),
            vec, vec,
            _layer_spec((D_MODEL, 2 * D_FF), layer, pipeline_mode=once),
            _layer_spec((D_FF, D_MODEL), layer, pipeline_mode=once),
            vec, vec,
        ],
        out_specs=pl.BlockSpec((TM_FFN, D_MODEL), row),
        out_shape=jax.ShapeDtypeStruct((t, D_MODEL), F32),
        compiler_params=_cparams(("arbitrary",)),
        name="ffn_dec" if dec else "ffn_ctx",
    )(hg, at, x, mod, w["w_out"], prm["ln1_g"], prm["ln1_b"], w["w_gu"], w["w_dn"], prm["ln2_g"], prm["ln2_b"])


def _rope_tables(width, rot_off, rot_dim):
    t = np.arange(DEC_SEQ)
    pos = (t // GRID_W, t % GRID_W)
    half = rot_dim // 4
    freqs = ROPE_THETA ** (-np.arange(half, dtype=np.float64) / half)
    c = np.ones((DEC_SEQ, LANES), np.float64)
    sa = np.zeros((DEC_SEQ, LANES), np.float64)
    sb = np.zeros((DEC_SEQ, LANES), np.float64)
    for blk in range(LANES // width):
        for axis in range(2):
            base = blk * width + rot_off + axis * 2 * half
            ang = pos[axis][:, None].astype(np.float64) * freqs[None, :]
            c[:, base:base + half] = np.cos(ang)
            c[:, base + half:base + 2 * half] = np.cos(ang)
            sa[:, base:base + half] = -np.sin(ang)
            sb[:, base + half:base + 2 * half] = np.sin(ang)
    return tuple(jnp.asarray(a, F32) for a in (c, sa, sb))


def _constants():
    C = HG_CHUNK
    t = np.arange(C)
    tri_f = (t[None, :] <= t[:, None]).astype(np.float32)
    tri_b = (t[None, :] >= t[:, None]).astype(np.float32)
    x = t[:, None] ^ t[None, :]
    top = np.where(x > 0, np.floor(np.log2(np.maximum(x, 1))).astype(np.int32), -1)
    same = (t[:, None] // SUBLANES) == (t[None, :] // SUBLANES)
    lvl_f = np.where(same, np.where(t[:, None] >= t[None, :], 0, -1),
                     np.where(t[:, None] > t[None, :], top, -1)).astype(np.int32)
    lvl_b = np.where(same, np.where(t[:, None] <= t[None, :], 0, -1),
                     np.where(t[:, None] < t[None, :], top, -1)).astype(np.int32)
    grp = np.arange(LANES) // HG_DK
    blk = (grp[:, None] == grp[None, :]).astype(np.float32)
    stk = np.zeros((SUBLANES, LANES, 2, C), np.float32)
    for s in range(SUBLANES):
        for hb in range(2):
            stk[s, hb * HG_DK:(hb + 1) * HG_DK, hb, s::SUBLANES] = 1.0
    stk = stk.reshape(SUBLANES * LANES, 2 * C)
    place = np.zeros((MLA_ROPE, MLA_QW), np.float32)
    for hd in range(MLA_HEADS):
        place[np.arange(MLA_ROPE), hd * HEAD_PAD + MLA_NOPE + np.arange(MLA_ROPE)] = 1.0
    mla_c, mla_sa, mla_sb = _rope_tables(HEAD_PAD, MLA_NOPE, MLA_ROPE)
    gqa_c, gqa_sa, gqa_sb = _rope_tables(GQA_HD, 0, GQA_HD)
    return {
        "tri_f": jnp.asarray(tri_f, BF16), "tri_b": jnp.asarray(tri_b, BF16),
        "lvl_f": jnp.asarray(np.concatenate([lvl_f, lvl_f], axis=0)),
        "lvl_b": jnp.asarray(np.concatenate([lvl_b, lvl_b], axis=0)),
        "stk": jnp.asarray(stk, BF16), "ones": jnp.asarray(blk, BF16), "ones64": jnp.asarray(blk / GQA_HD, BF16),
        "blk": jnp.asarray(blk, F32), "kpe_place": jnp.asarray(place, BF16),
        "mla_c": mla_c, "mla_sa": mla_sa, "mla_sb": mla_sb, "gqa_c": gqa_c, "gqa_sa": gqa_sa, "gqa_sb": gqa_sb,
    }


def kernel(x_prompt, x_sample, cache_mla_ckv, cache_mla_kpe, cache_gqa_k, cache_gqa_v, state_hgrn, c, c_ctx, w_ada,
           b_ada, w_in, hg_lb, hg_norm, mla_q_norm, mla_w_uq, mla_kv_norm, mla_w_ukv, gqa_q_norm, gqa_k_norm, w_out,
           ln1_g, ln1_b, w_ffn_in, w_ffn_out, ln2_g, ln2_b):
    cst = _constants()
    prm = {"mla_q_norm": mla_q_norm, "mla_kv_norm": mla_kv_norm, "gqa_q_norm": gqa_q_norm, "gqa_k_norm": gqa_k_norm,
           "ln1_g": ln1_g, "ln1_b": ln1_b, "ln2_g": ln2_g, "ln2_b": ln2_b}
    w = {"w_in": _prep_in_call(w_in)}
    w["w_uq"], w["w_uk"], w["w_uv"] = _prep_mla_call(mla_w_uq, mla_w_ukv)
    w["w_out"], w["w_gu"], w["w_dn"] = _prep_cast_call(w_out, w_ffn_in, w_ffn_out)

    cvec = jnp.concatenate([c_ctx[None, :], c, jnp.zeros((MOD_ROWS - 1 - DEC_BATCH, D_MODEL), F32)], axis=0)
    mod = _ada_call(cvec, w_ada, b_ada)
    cache_gk = cache_gqa_k.reshape(DEC_BATCH, DEPTH, PAST_LEN, LANES)
    cache_gv = cache_gqa_v.reshape(DEC_BATCH, DEPTH, PAST_LEN, LANES)

    x_ctx = x_prompt.reshape(BATCH * SEQ, D_MODEL)
    x_dec = x_sample.reshape(DEC_BATCH * DEC_SEQ, D_MODEL)
    states, hg_state = None, None
    for l in range(DEPTH):
        zhf, zhb, qm, km, vm, gq, gk, gv, *states = _inproj_call(x_ctx, mod, w, prm, cst, l, rope=False,
                                                          prev_states=states)
        hg_out, hg_state = _hgrn_call(zhf, zhb, hg_lb, hg_norm, cst, l, BATCH, SEQ, prev_state_out=hg_state)
        at = _attn_call(qm, gq, km, vm, gk, gv, None, l)
        x_ctx = _ffn_call(hg_out, at, x_ctx, mod, w, prm, l, dec=False)

        zhf, zhb, qm, km, vm, gq, gk, gv = _inproj_call(x_dec, mod, w, prm, cst, l, rope=True)
        hg_out = _hgrn_call(zhf, zhb, hg_lb, hg_norm, cst, l, DEC_BATCH, DEC_SEQ, state_in=state_hgrn)[0]
        kc, vc = _cachekv_call(cache_mla_ckv, cache_mla_kpe, w, cst, l)
        at = _attn_call(qm, gq, km, vm, gk, gv, (kc, vc, cache_gk, cache_gv), l)
        x_dec = _ffn_call(hg_out, at, x_dec, mod, w, prm, l, dec=True)

    ckv, kpe, gk32, gv32 = states
    kv_shape = (BATCH, DEPTH, SEQ, GQA_KV_HEADS, GQA_HD)
    return (x_ctx.reshape(BATCH, SEQ, D_MODEL), x_dec.reshape(DEC_BATCH, DEC_SEQ, D_MODEL),
            ckv, kpe, gk32.reshape(kv_shape), gv32.reshape(kv_shape), hg_state)
```

```python
import functools

import numpy as np
import jax
import jax.numpy as jnp
from jax import lax
from jax.experimental import pallas as pl
from jax.experimental.pallas import tpu as pltpu

F32 = jnp.float32
BF16 = jnp.bfloat16

D_MODEL = 1024
BATCH = 32
SEQ = 256
DEPTH = 2
DEC_BATCH = 2
DEC_SEQ = 2048
PAST_LEN = 256
GRID_W = 64
ROPE_THETA = 10000.0
EPS = 1e-6
GATE_FLOOR = 1e-30
HG_HEADS = 4
HG_DK = 64
HG_DV = 64
MLA_HEADS = 6
MLA_NOPE = 64
MLA_ROPE = 32
MLA_V = 64
MLA_Q_RANK = 256
MLA_KV_RANK = 128
GQA_HEADS = 6
GQA_KV_HEADS = 2
GQA_HD = 64
HG_W = HG_HEADS * HG_DV
MLA_W = MLA_HEADS * MLA_V
GQA_W = GQA_HEADS * GQA_HD
D_FF = 2816
MLA_SCALE = (MLA_NOPE + MLA_ROPE) ** -0.5
GQA_SCALE = GQA_HD ** -0.5
ALPHA = (2 * DEPTH) ** 0.25
LOG2E = 1.4426950408889634

LANES = 128
SUBLANES = 8
HEAD_PAD = 128
MLA_QW = MLA_HEADS * HEAD_PAD
N_ADA = 6 * D_MODEL
MOD_ROWS = 8

N_IN = 5 * HG_HEADS * HG_DK + MLA_Q_RANK + MLA_KV_RANK + MLA_ROPE + GQA_W + 2 * GQA_KV_HEADS * GQA_HD
SRC_KPE = 5 * HG_HEADS * HG_DK + MLA_Q_RANK + MLA_KV_RANK
SRC_GQ = SRC_KPE + MLA_ROPE
SRC_GK = SRC_GQ + GQA_W
SRC_GV = SRC_GK + GQA_KV_HEADS * GQA_HD
HG_COLS = 5 * HG_HEADS * HG_DK
OFF_MQ = HG_COLS
OFF_MKV = OFF_MQ + MLA_Q_RANK
OFF_GQ = OFF_MKV + MLA_KV_RANK
OFF_GK = OFF_GQ + GQA_W
OFF_GV = OFF_GK + GQA_KV_HEADS * GQA_HD
OFF_KPE_PAD = OFF_GV + GQA_KV_HEADS * GQA_HD
OFF_KPE = OFF_KPE_PAD + HEAD_PAD
N_Z = OFF_KPE + MLA_ROPE

TM_IN = 512
TM_FFN = 512
TQ_DEC = 512
GQA_ROWS = 256
ATT_GROUP_CTX = 9
ATT_GROUP_DEC = 1
VT_ROWS = 80
HG_CHUNK = 128
HG_UNROLL = 2
FF_CHUNKS = 2
PREP_STEPS = 4
VMEM_LIMIT = 56 * 1024 * 1024


def _cparams(sem):
    return pltpu.CompilerParams(dimension_semantics=sem, vmem_limit_bytes=VMEM_LIMIT)


def _const_spec(shape):
    nd = len(shape)
    return pl.BlockSpec(shape, lambda *_: (0,) * nd)


def _layer_spec(shape, layer, **kw):
    nd = len(shape)
    return pl.BlockSpec((None,) + tuple(shape), lambda *_: (layer,) + (0,) * nd, **kw)


_ANY = pl.BlockSpec(memory_space=pl.ANY)


def _silu(x):
    return x * jax.nn.sigmoid(x)


def _dot(a, b):
    return jnp.dot(a, b, preferred_element_type=F32)


def _dot_nt(a, b):
    return lax.dot_general(a, b, (((1,), (1,)), ((), ())), preferred_element_type=F32)


def _dot_tn(a, b):
    return lax.dot_general(a, b, (((0,), (0,)), ((), ())), preferred_element_type=F32)


def _rms_rows(x, g):
    y = x * lax.rsqrt(jnp.mean(x * x, axis=-1, keepdims=True) + EPS)
    return y * g


def _layernorm_rows(x, g, b):
    xc = x - jnp.mean(x, axis=-1, keepdims=True)
    var = jnp.mean(xc * xc, axis=-1, keepdims=True)
    return xc * lax.rsqrt(var + EPS) * g + b


def _group_mean(x, ones_blk):
    return _dot(x.astype(BF16), ones_blk)


def _twice(g):
    return jnp.concatenate([g, g], axis=1)


_GQ_HEAD_ORDER = (0, 3, 1, 4, 2, 5)


def _prep_in_kernel(w_ref, o_ref):
    x = w_ref[...]
    rows = x.shape[0]
    kr = x[:, SRC_KPE:SRC_GQ]
    gq = [x[:, SRC_GQ + h * GQA_HD:SRC_GQ + (h + 1) * GQA_HD] for h in _GQ_HEAD_ORDER]
    pieces = [x[:, 0:SRC_KPE]] + gq + [
        x[:, SRC_GK:SRC_GV], x[:, SRC_GV:N_IN],
        jnp.zeros((rows, MLA_NOPE), F32), kr, jnp.zeros((rows, HEAD_PAD - MLA_NOPE - MLA_ROPE), F32), kr]
    o_ref[...] = jnp.concatenate(pieces, axis=1).astype(BF16)


def _prep_in_call(w_in):
    rb = 128
    return pl.pallas_call(
        _prep_in_kernel,
        grid=(DEPTH, D_MODEL // rb),
        in_specs=[pl.BlockSpec((None, rb, N_IN), lambda l, j: (l, j, 0))],
        out_specs=pl.BlockSpec((None, rb, N_Z), lambda l, j: (l, j, 0)),
        out_shape=jax.ShapeDtypeStruct((DEPTH, D_MODEL, N_Z), BF16),
        compiler_params=_cparams(("arbitrary", "arbitrary")),
        name="prep_in",
    )(w_in)


def _prep_mla_kernel(uq_ref, ukv_ref, oq_ref, ok_ref, ov_ref):
    uq = uq_ref[...]
    dq = MLA_NOPE + MLA_ROPE
    zq = jnp.zeros((MLA_Q_RANK, HEAD_PAD - dq), F32)
    oq_ref[...] = jnp.concatenate(
        [p for h in range(MLA_HEADS) for p in (uq[:, h * dq:(h + 1) * dq], zq)], axis=1).astype(BF16)
    ukv = ukv_ref[...]
    lane = lax.broadcasted_iota(jnp.int32, ukv.shape, 1)
    ok_ref[...] = jnp.where((lane & (HEAD_PAD - 1)) < MLA_NOPE, ukv, 0.0).astype(BF16)
    ov_ref[...] = jnp.concatenate(
        [ukv[:, h * HEAD_PAD + MLA_NOPE:(h + 1) * HEAD_PAD] for h in range(MLA_HEADS)], axis=1).astype(BF16)


def _prep_mla_call(w_uq, w_ukv):
    lay = lambda n: pl.BlockSpec((None, n[0], n[1]), lambda l: (l, 0, 0))
    shapes = [(MLA_Q_RANK, MLA_QW), (MLA_KV_RANK, MLA_QW), (MLA_KV_RANK, MLA_W)]
    return pl.pallas_call(
        _prep_mla_kernel,
        grid=(DEPTH,),
        in_specs=[lay((MLA_Q_RANK, MLA_HEADS * (MLA_NOPE + MLA_ROPE))), lay((MLA_KV_RANK, MLA_QW))],
        out_specs=[lay(s) for s in shapes],
        out_shape=[jax.ShapeDtypeStruct((DEPTH,) + s, BF16) for s in shapes],
        compiler_params=_cparams(("arbitrary",)),
        name="prep_mla",
    )(w_uq, w_ukv)


def _prep_cast_kernel(a_ref, b_ref, c_ref, oa_ref, ob_ref, oc_ref):
    oa_ref[...] = a_ref[...].astype(BF16)
    ob_ref[...] = b_ref[...].astype(BF16)
    oc_ref[...] = c_ref[...].astype(BF16)


def _prep_cast_call(w_out, w_ffn_in, w_ffn_out):
    ws = (w_out, w_ffn_in, w_ffn_out)
    specs = [pl.BlockSpec((None, w.shape[1] // PREP_STEPS, w.shape[2]), lambda l, j: (l, j, 0)) for w in ws]
    return pl.pallas_call(
        _prep_cast_kernel,
        grid=(DEPTH, PREP_STEPS),
        in_specs=specs,
        out_specs=specs,
        out_shape=[jax.ShapeDtypeStruct(w.shape, BF16) for w in ws],
        compiler_params=_cparams(("arbitrary", "arbitrary")),
        name="prep_cast",
    )(*ws)


def _ada_kernel(c_ref, w_ref, b_ref, o_ref):
    s = _silu(c_ref[...]).astype(BF16)
    o_ref[...] = _dot(s, w_ref[...].astype(BF16)) + b_ref[pl.ds(pl.program_id(0), 1), :]


def _ada_call(cvec, w_ada, b_ada):
    tn = 1536
    return pl.pallas_call(
        _ada_kernel,
        grid=(DEPTH, N_ADA // tn),
        in_specs=[
            pl.BlockSpec((MOD_ROWS, D_MODEL), lambda l, j: (0, 0)),
            pl.BlockSpec((None, D_MODEL, tn), lambda l, j: (l, 0, j)),
            pl.BlockSpec((DEPTH, tn), lambda l, j: (0, j)),
        ],
        out_specs=pl.BlockSpec((None, MOD_ROWS, tn), lambda l, j: (l, 0, j)),
        out_shape=jax.ShapeDtypeStruct((DEPTH, MOD_ROWS, N_ADA), F32),
        compiler_params=_cparams(("arbitrary", "arbitrary")),
        name="ada_mod",
    )(cvec, w_ada, b_ada)


def _mod_row(mod_ref, dec, tiles_per_seq):
    if not dec:
        return mod_ref[0:1, :]
    return mod_ref[pl.ds(1 + pl.program_id(0) // tiles_per_seq, 1), :]


def _rope_block(x, c, sa, sb, half):
    return x * c + pltpu.roll(x, LANES - half, 1) * sa + pltpu.roll(x, half, 1) * sb


def _inproj_kernel(layer, rope, x_ref, mod_ref, w_in_ref, w_uq_ref, w_uk_ref, w_uv_ref, qn_ref, kvn_ref, gqn_ref,
                   gkn_ref, ones_ref, *rest):
    if rope:
        (mc_ref, msa_ref, msb_ref, gc_ref, gsa_ref, gsb_ref,
         zhf_ref, zhb_ref, qm_ref, km_ref, vm_ref, gq_ref, gk_ref, gv_ref) = rest
    else:
        rest = rest[-12:]
        (zhf_ref, zhb_ref, qm_ref, km_ref, vm_ref, gq_ref, gk_ref, gv_ref,
         ckv_ref, kpe_ref, gk32_ref, gv32_ref) = rest
    lrow = slice(layer, layer + 1)
    qk_base = LOG2E
    mod = _mod_row(mod_ref, rope, DEC_SEQ // TM_IN)
    sh1 = mod[:, 0:D_MODEL]
    sc1 = mod[:, D_MODEL:2 * D_MODEL]
    h = (x_ref[...] * (1.0 + sc1) + sh1).astype(BF16)
    z = _dot(h, w_in_ref[...])
    gw = HG_HEADS * HG_DK
    zhf_ref[...] = z[:, gw:3 * gw]
    zhb_ref[:, 0:gw] = z[:, 0:gw].astype(BF16)
    zhb_ref[:, gw:3 * gw] = z[:, 3 * gw:HG_COLS].astype(BF16)
    mq = z[:, OFF_MQ:OFF_MKV]
    mkv = z[:, OFF_MKV:OFF_GQ]
    gq = z[:, OFF_GQ:OFF_GK]
    gk = z[:, OFF_GK:OFF_GV]
    gv = z[:, OFF_GV:OFF_KPE_PAD]
    kpe_pad = z[:, OFF_KPE_PAD:OFF_KPE]
    ones = ones_ref[...]
    gqn = _twice(gqn_ref[lrow, :])
    gkn_g = _twice(gkn_ref[lrow, :])

    q = _dot(_rms_rows(mq, qn_ref[lrow, :]).astype(BF16), w_uq_ref[...])
    ckv = _rms_rows(mkv, kvn_ref[lrow, :])
    ckv_b = ckv.astype(BF16)
    kn = _dot(ckv_b, w_uk_ref[...])
    vm_ref[...] = _dot(ckv_b, w_uv_ref[...]).astype(BF16)
    if rope:
        mc, msa, msb = mc_ref[...], msa_ref[...], msb_ref[...]
        kpe_pad = _rope_block(kpe_pad, mc, msa, msb, MLA_ROPE // 4)
    for hd in range(MLA_HEADS):
        sl = slice(hd * HEAD_PAD, (hd + 1) * HEAD_PAD)
        qh = q[:, sl]
        if rope:
            qh = _rope_block(qh, mc, msa, msb, MLA_ROPE // 4)
        qm_ref[:, sl] = (qh * (MLA_SCALE * qk_base)).astype(BF16)
        km_ref[:, sl] = (kn[:, sl] + kpe_pad).astype(BF16)

    gkn = gk * lax.rsqrt(_group_mean(gk * gk, ones) + EPS) * gkn_g
    for p in range(GQA_W // LANES):
        sl = slice(p * LANES, (p + 1) * LANES)
        gqp = gq[:, sl]
        gqp = gqp * lax.rsqrt(_group_mean(gqp * gqp, ones) + EPS) * gqn
        if rope:
            gqp = _rope_block(gqp, gc_ref[...], gsa_ref[...], gsb_ref[...], GQA_HD // 4)
        gq_ref[:, sl] = (gqp * (GQA_SCALE * qk_base)).astype(BF16)
    if rope:
        gk_ref[...] = _rope_block(gkn, gc_ref[...], gsa_ref[...], gsb_ref[...], GQA_HD // 4).astype(BF16)
    else:
        gk_ref[...] = gkn.astype(BF16)
        kpe = z[:, OFF_KPE:N_Z]
        for b in range(TM_IN // SEQ):
            rows = slice(b * SEQ, (b + 1) * SEQ)
            ckv_ref[b] = ckv[rows]
            kpe_ref[b] = kpe[rows]
            gk32_ref[b] = gkn[rows]
            gv32_ref[b] = gv[rows]
    gv_ref[...] = gv.astype(BF16)


def _inproj_call(x, mod, w, prm, cst, layer, rope, prev_states=None):
    t = x.shape[0]
    nt = t // TM_IN
    row = lambda i: (i, 0)
    in_specs = [
        pl.BlockSpec((TM_IN, D_MODEL), row),
        _layer_spec((MOD_ROWS, N_ADA), layer),
        _layer_spec((D_MODEL, N_Z), layer),
        _layer_spec((MLA_Q_RANK, MLA_QW), layer),
        _layer_spec((MLA_KV_RANK, MLA_QW), layer),
        _layer_spec((MLA_KV_RANK, MLA_W), layer),
        _const_spec((DEPTH, MLA_Q_RANK)),
        _const_spec((DEPTH, MLA_KV_RANK)),
        _const_spec((DEPTH, GQA_HD)),
        _const_spec((DEPTH, GQA_HD)),
        _const_spec((LANES, LANES)),
    ]
    args = [x, mod, w["w_in"], w["w_uq"], w["w_uk"], w["w_uv"], prm["mla_q_norm"], prm["mla_kv_norm"],
            prm["gqa_q_norm"], prm["gqa_k_norm"], cst["ones64"]]
    outs = [(2 * HG_HEADS * HG_DK, F32), (3 * HG_HEADS * HG_DK, BF16), (MLA_QW, BF16), (MLA_QW, BF16), (MLA_W, BF16),
            (GQA_W, BF16), (LANES, BF16), (LANES, BF16)]
    out_specs = [pl.BlockSpec((TM_IN, wd), row) for wd, _ in outs]
    out_shape = [jax.ShapeDtypeStruct((t, wd), dt) for wd, dt in outs]
    aliases = {}
    if rope:
        tab = lambda i: (i % (DEC_SEQ // TM_IN), 0)
        in_specs += [pl.BlockSpec((TM_IN, LANES), tab)] * 6
        args += [cst["mla_c"], cst["mla_sa"], cst["mla_sb"], cst["gqa_c"], cst["gqa_sa"], cst["gqa_sb"]]
    else:
        per = TM_IN // SEQ
        for wd in (MLA_KV_RANK, MLA_ROPE, LANES, LANES):
            out_specs.append(pl.BlockSpec((per, None, SEQ, wd), lambda i: (i, layer, 0, 0)))
            out_shape.append(jax.ShapeDtypeStruct((BATCH, DEPTH, SEQ, wd), F32))
        if prev_states is not None:
            for k, arr in enumerate(prev_states):
                aliases[len(args)] = len(outs) + k
                in_specs.append(_ANY)
                args.append(arr)
    return pl.pallas_call(
        functools.partial(_inproj_kernel, layer, rope),
        grid=(nt,),
        in_specs=in_specs,
        out_specs=out_specs,
        out_shape=out_shape,
        input_output_aliases=aliases,
        compiler_params=_cparams(("arbitrary",)),
        name="inproj_dec" if rope else "inproj_ctx",
    )(*args)


def _cachekv_kernel(ckv_ref, kpe_ref, w_uk_ref, w_uv_ref, place_ref, k_ref, v_ref):
    ckv_b = ckv_ref[...].astype(BF16)
    k_ref[...] = (_dot(ckv_b, w_uk_ref[...]) + _dot(kpe_ref[...].astype(BF16), place_ref[...])).astype(BF16)
    v_ref[...] = _dot(ckv_b, w_uv_ref[...]).astype(BF16)


def _cachekv_call(cache_ckv, cache_kpe, w, cst, layer):
    t = DEC_BATCH * PAST_LEN
    cache = lambda wd: pl.BlockSpec((None, None, PAST_LEN, wd), lambda b: (b, layer, 0, 0))
    return pl.pallas_call(
        _cachekv_kernel,
        grid=(DEC_BATCH,),
        in_specs=[cache(MLA_KV_RANK), cache(MLA_ROPE),
                  _layer_spec((MLA_KV_RANK, MLA_QW), layer), _layer_spec((MLA_KV_RANK, MLA_W), layer),
                  _const_spec((MLA_ROPE, MLA_QW))],
        out_specs=[pl.BlockSpec((PAST_LEN, MLA_QW), lambda b: (b, 0)),
                   pl.BlockSpec((PAST_LEN, MLA_W), lambda b: (b, 0))],
        out_shape=[jax.ShapeDtypeStruct((t, MLA_QW), BF16), jax.ShapeDtypeStruct((t, MLA_W), BF16)],
        compiler_params=_cparams(("arbitrary",)),
        name="cache_kv",
    )(cache_ckv, cache_kpe, w["w_uk"], w["w_uv"], cst["kpe_place"])


def _bcast_row(x, r, rows):
    return jnp.broadcast_to(x[r:r + 1, :], (rows, x.shape[1]))


def _cumsum_rows(tri, x):
    hi = x.astype(BF16)
    r1 = x - hi.astype(F32)
    mid = r1.astype(BF16)
    lo = (r1 - mid.astype(F32)).astype(BF16)
    return _dot(tri, hi) + _dot(tri, mid) + _dot(tri, lo)


def _hgrn_chunks(items, states, tri, lvl, stk, blk, lane_lo):
    C = HG_CHUNK
    zero_b = jnp.zeros((SUBLANES, LANES), F32)
    trow = lax.broadcasted_iota(jnp.int32, (SUBLANES, LANES), 0)
    for it in items:
        it["c"] = _cumsum_rows(tri[it["fwd"]], it["lf"])
    for it in items:
        it["q_a"] = jnp.where(lane_lo, it["q"], 0.0)
        it["q_b"] = jnp.where(lane_lo, 0.0, it["q"])
        it["ck"] = it["c"] - jnp.log(it["kk"])

    m = C // 2
    bit = m.bit_length() - 1
    while m >= SUBLANES:
        for it in items:
            fwd, c = it["fwd"], it["c"]
            qa_rows, qb_rows, k_rows = [], [], []
            for b in range(C // m):
                rows = slice(b * m, (b + 1) * m)
                anchor = (b // 2) * 2 * m + (m - 1 if fwd else m)
                r = _bcast_row(c, anchor, m)
                if (b % 2 == 1) == fwd:
                    e = jnp.exp(c[rows] - r)
                    qa_rows.append(it["q_a"][rows] * e)
                    qb_rows.append(it["q_b"][rows] * e)
                    k_rows += [zero_b] * (m // SUBLANES)
                else:
                    e = jnp.exp(r - c[rows])
                    qa_rows += [zero_b] * (m // SUBLANES)
                    qb_rows += [zero_b] * (m // SUBLANES)
                    k_rows.append(it["kk"][rows] * e)
            prod = _dot_nt(jnp.concatenate(qa_rows + qb_rows, axis=0).astype(BF16),
                           jnp.concatenate(k_rows, axis=0).astype(BF16))
            it["a"] = prod if "a" not in it else jnp.where(lvl[fwd] == bit, prod, it["a"])
        m //= 2
        bit -= 1

    for it in items:
        it["w"] = []
    for j in range(C // SUBLANES):
        sl = slice(j * SUBLANES, (j + 1) * SUBLANES)
        for it in items:
            cs, cks, qs = it["c"][sl], it["ck"][sl], it["q"][sl]
            w_s = []
            for s in range(SUBLANES):
                e = jnp.exp(cs - _bcast_row(cks, s, SUBLANES))
                keep = (trow >= s) if it["fwd"] else (trow <= s)
                w_s.append(jnp.where(keep, qs * e, 0.0))
            it["w"].append(jnp.concatenate(w_s, axis=1))
    for it in items:
        red = _dot(jnp.concatenate(it["w"], axis=0).astype(BF16), stk)
        it["a"] = jnp.where(lvl[it["fwd"]] == 0,
                            jnp.concatenate([red[:, 0:LANES], red[:, LANES:2 * LANES]], axis=0), it["a"])
    for it in items:
        off = _dot(it["a"].astype(BF16), it["iv"].astype(BF16))
        it["o"] = jnp.where(lane_lo, off[0:C], off[C:2 * C])
        it["edge"] = _bcast_row(it["c"], C - 1 if it["fwd"] else 0, C)
        it["q_dec"] = (it["q"] * jnp.exp(it["c"])).astype(BF16)
        k_dec = (it["kk"] * jnp.exp(it["edge"] - it["c"])).astype(BF16)
        it["upd"] = _dot_tn(it["iv"].astype(BF16), k_dec) * blk

    states = list(states)
    outs = []
    for it in items:
        st = states[it["sid"]]
        outs.append(it["o"] + _dot_nt(it["q_dec"], st.astype(BF16)))
        states[it["sid"]] = st * jnp.exp(it["edge"][0:1]) + it["upd"]
    return outs, states


def _hgrn_kernel(layer, nchunks, has_state, zq_ref, zff_ref, zfb_ref, zi_ref, zg_ref, lb_ref, g_ref,
                 trif_ref, trib_ref, lvlf_ref, lvlb_ref, stk_ref, ones_ref, blk_ref, *rest):
    if has_state:
        s0_ref, o_ref, of_scr, ob_scr = rest
    else:
        o_ref, st_ref, of_scr, ob_scr = rest[-4:]
    C = HG_CHUNK
    lane_lo = lax.broadcasted_iota(jnp.int32, (1, LANES), 1) < HG_DK

    lb_all = [lb_ref[j] for j in range(DEPTH)]
    mx = functools.reduce(jnp.maximum, lb_all)
    ex = [jnp.exp(v - mx) for v in lb_all]
    den = functools.reduce(lambda a, b: a + b, ex)
    sm = [v / den for v in ex]
    lbs = functools.reduce(lambda a, b: a + b, sm[:layer + 1]) - sm[0]
    lbf, lbb = lbs[0:1], lbs[1:2]

    def gates(zf, lbd):
        f = lbd + (1.0 - lbd) * jax.nn.sigmoid(zf)
        return jnp.log(jnp.maximum(f, GATE_FLOOR)), jnp.maximum(1.0 - f, 0.0)

    def block_diag_t(s_ref_d):
        za = jnp.zeros((HG_DK, HG_DV), F32)
        top = jnp.concatenate([s_ref_d[0], za], axis=1)
        bot = jnp.concatenate([za, s_ref_d[1]], axis=1)
        return jnp.concatenate([top, bot], axis=0).T

    if has_state:
        st_f0 = block_diag_t(s0_ref[0])
        st_b0 = block_diag_t(s0_ref[1])
    else:
        st_f0 = jnp.zeros((LANES, LANES), F32)
        st_b0 = st_f0

    def body(j, carry):
        tri = {True: trif_ref[...], False: trib_ref[...]}
        lvl = {True: lvlf_ref[...], False: lvlb_ref[...]}
        items = []
        for u in range(HG_UNROLL):
            jf = j * HG_UNROLL + u
            for fwd in (True, False):
                rows = pl.ds(pl.multiple_of((jf if fwd else nchunks - 1 - jf) * C, C), C)
                lf, kk = gates((zff_ref if fwd else zfb_ref)[rows, :], lbf if fwd else lbb)
                items.append({"fwd": fwd, "sid": 0 if fwd else 1, "rows": rows, "lf": lf, "kk": kk,
                              "q": _silu(zq_ref[rows, :].astype(F32)) * (HG_DK ** -0.5), "iv": zi_ref[rows, :]})
        outs, states = _hgrn_chunks(items, carry, tri, lvl, stk_ref[...], blk_ref[...], lane_lo)
        for it, o in zip(items, outs):
            (of_scr if it["fwd"] else ob_scr)[it["rows"], :] = o
        return tuple(states)

    st_f, st_b = lax.fori_loop(0, nchunks // HG_UNROLL, body, (st_f0, st_b0))

    o = of_scr[...] + ob_scr[...]
    ms = _group_mean(o * o, ones_ref[...] * (1.0 / HG_DV))
    g = _twice(g_ref[layer:layer + 1, :])
    o_ref[...] = (o * lax.rsqrt(ms + EPS) * g * _silu(zg_ref[...].astype(F32))).astype(BF16)
    if not has_state:
        for d, st in enumerate((st_f, st_b)):
            s_kv = st.T
            st_ref[d, 0] = s_kv[0:HG_DK, 0:HG_DV]
            st_ref[d, 1] = s_kv[HG_DK:2 * HG_DK, HG_DV:2 * HG_DV]


def _hgrn_call(zhf, zhb, lb, g, cst, layer, nb, s, state_in=None, prev_state_out=None):
    has_state = state_in is not None
    nchunks = s // HG_CHUNK
    npair = HG_HEADS // 2
    part = lambda k: pl.BlockSpec((s, LANES), lambda b, hp: (b, k * npair + hp))
    st_spec = pl.BlockSpec((None, None, 2, 2, HG_DK, HG_DV), lambda b, hp: (b, layer, 0, hp, 0, 0))
    in_specs = [part(0), part(0), part(1), part(1), part(2),
                pl.BlockSpec((DEPTH, 2, LANES), lambda b, hp: (0, 0, hp)),
                _const_spec((DEPTH, HG_DV)),
                _const_spec((HG_CHUNK, HG_CHUNK)), _const_spec((HG_CHUNK, HG_CHUNK)),
                _const_spec((2 * HG_CHUNK, HG_CHUNK)), _const_spec((2 * HG_CHUNK, HG_CHUNK)),
                _const_spec((SUBLANES * LANES, 2 * LANES)),
                _const_spec((LANES, LANES)), _const_spec((LANES, LANES))]
    args = [zhb, zhf, zhf, zhb, zhb, lb, g, cst["tri_f"], cst["tri_b"], cst["lvl_f"], cst["lvl_b"], cst["stk"],
            cst["ones"], cst["blk"]]
    out_specs = [pl.BlockSpec((s, LANES), lambda b, hp: (b, hp))]
    out_shape = [jax.ShapeDtypeStruct((nb * s, HG_W), BF16)]
    aliases = {}
    if has_state:
        in_specs.append(st_spec)
        args.append(state_in)
    else:
        out_specs.append(st_spec)
        out_shape.append(jax.ShapeDtypeStruct((nb, DEPTH, 2, HG_HEADS, HG_DK, HG_DV), F32))
        if prev_state_out is not None:
            aliases[len(args)] = 1
            in_specs.append(_ANY)
            args.append(prev_state_out)
    return pl.pallas_call(
        functools.partial(_hgrn_kernel, layer, nchunks, has_state),
        grid=(nb, npair),
        in_specs=in_specs,
        out_specs=out_specs,
        out_shape=out_shape,
        input_output_aliases=aliases,
        scratch_shapes=[pltpu.VMEM((s, LANES), F32), pltpu.VMEM((s, LANES), F32)],
        compiler_params=_cparams(("arbitrary", "arbitrary")),
        name="hgrn_dec" if has_state else "hgrn_ctx",
    )(*args)


def _attend_t(jobs, ss):
    add = lambda a, b: a + b
    mxs = [functools.reduce(jnp.maximum, [jnp.max(s, axis=-1, keepdims=True) for s in sj]) for sj in ss]
    ebs = [[jnp.exp2(s - mx).astype(BF16) for s in sj] for sj, mx in zip(ss, mxs)]
    prods = [[functools.reduce(add, [_dot_nt(vt, e[rows]) for vt, e in zip(vts, ej)]) for rows, vts in parts]
             for ej, (_, _, parts) in zip(ebs, jobs)]
    return [[pr[0:MLA_V] * (1.0 / pr[MLA_V:MLA_V + 1]) for pr in pj] for pj in prods]


def _scores(jobs, keys_transposed):
    mm = _dot if keys_transposed else _dot_nt
    return [[mm(q, k) for k in ks] for q, ks, _ in jobs]


def _attn_kernel(has_cache, qm_ref, gq_ref, km_ref, vm_ref, gk_ref, gv_ref, *rest):
    if has_cache:
        kc_ref, vc_ref, gkc_ref, gvc_ref, o_ref, vt_scr, kt_scr = rest
    else:
        o_ref, vt_scr = rest
    sq = qm_ref.shape[0]
    sk = vt_scr.shape[1]
    nblk = MLA_HEADS + GQA_KV_HEADS
    cols = [slice(0, PAST_LEN), slice(PAST_LEN, sk)] if has_cache else [slice(0, sk)]

    def fill():
        tr = lambda v: v.astype(F32).T.astype(BF16)
        srcs = [(vc_ref, gvc_ref), (vm_ref, gv_ref)] if has_cache else [(vm_ref, gv_ref)]
        for cl, (mla_ref, gqa_ref) in zip(cols, srcs):
            mla_t, gqa_t = tr(mla_ref[...]), tr(gqa_ref[...])
            tail = (lax.broadcasted_iota(jnp.int32, (VT_ROWS - MLA_V, cl.stop - cl.start), 0) == 0).astype(BF16)
            for b in range(nblk):
                src = mla_t[b * MLA_V:(b + 1) * MLA_V] if b < MLA_HEADS else \
                    gqa_t[(b - MLA_HEADS) * GQA_HD:(b - MLA_HEADS + 1) * GQA_HD]
                vt_scr[b * VT_ROWS:b * VT_ROWS + MLA_V, cl] = src
                vt_scr[b * VT_ROWS + MLA_V:(b + 1) * VT_ROWS, cl] = tail
        if has_cache:
            for cl, mla_ref, gqa_ref in zip(cols, (kc_ref, km_ref), (gkc_ref, gk_ref)):
                kt_scr[0:MLA_QW, cl] = tr(mla_ref[...])
                kt_scr[MLA_QW:, cl] = tr(gqa_ref[...])

    if has_cache:
        pl.when(pl.program_id(1) == 0)(fill)
    else:
        fill()

    def vt(b):
        return [vt_scr[b * VT_ROWS:(b + 1) * VT_ROWS, cl] for cl in cols]

    lo = lax.broadcasted_iota(jnp.int32, (GQA_ROWS, LANES), 1) < GQA_HD
    zero = jnp.zeros((GQA_ROWS, LANES), BF16)
    gks = [kt_scr[MLA_QW:, cl] for cl in cols] if has_cache else [gk_ref[...]]
    jobs = []
    for hd in range(MLA_HEADS):
        sl = slice(hd * HEAD_PAD, (hd + 1) * HEAD_PAD)
        ks = [kt_scr[sl, cl] for cl in cols] if has_cache else [km_ref[:, sl]]
        jobs.append((qm_ref[:, sl], ks, [(slice(0, sq), vt(hd))]))
    halves = [(slice(0, GQA_ROWS), vt(MLA_HEADS)), (slice(GQA_ROWS, 2 * GQA_ROWS), vt(MLA_HEADS + 1))]
    for r0 in range(0, sq, GQA_ROWS):
        for p in range(GQA_W // LANES):
            qp = gq_ref[r0:r0 + GQA_ROWS, p * LANES:(p + 1) * LANES]
            jobs.append((jnp.concatenate([jnp.where(lo, qp, zero), jnp.where(lo, zero, qp)], axis=0), gks, halves))
    group = ATT_GROUP_DEC if has_cache else ATT_GROUP_CTX
    groups = [jobs[j0:j0 + group] for j0 in range(0, len(jobs), group)]
    outs = []
    ss_next = _scores(groups[0], has_cache)
    for g, grp in enumerate(groups):
        ss = ss_next
        if g + 1 < len(groups):
            ss_next = _scores(groups[g + 1], has_cache)
        outs += _attend_t(grp, ss)
    for p in range(MLA_HEADS // 2):
        pair = jnp.concatenate([outs[2 * p][0], outs[2 * p + 1][0]], axis=0)
        o_ref[:, p * LANES:(p + 1) * LANES] = pair.T.astype(BF16)
    for i, r0 in enumerate(range(0, sq, GQA_ROWS)):
        gres = [jnp.concatenate(outs[MLA_HEADS + 3 * i + p], axis=0).T for p in range(GQA_W // LANES)]
        swapped = pltpu.roll(gres[1], GQA_HD, 1)
        blocks = (jnp.where(lo, gres[0], swapped), jnp.where(lo, gres[2], gres[0]), jnp.where(lo, swapped, gres[2]))
        for p, blkv in enumerate(blocks):
            o_ref[r0:r0 + GQA_ROWS, MLA_W + p * LANES:MLA_W + (p + 1) * LANES] = blkv.astype(BF16)


def _attn_call(qm, gq, km, vm, gk, gv, cache, layer):
    has_cache = cache is not None
    t = qm.shape[0]
    widths = (MLA_QW, GQA_W, MLA_QW, MLA_W, LANES, LANES)
    if has_cache:
        nq = DEC_SEQ // TQ_DEC
        grid = (DEC_BATCH, nq)
        qmap = lambda b, i: (b * nq + i, 0)
        kmap = lambda b, i: (b, 0)
        once = pl.Buffered(1)
        in_specs = [pl.BlockSpec((TQ_DEC, widths[0]), qmap), pl.BlockSpec((TQ_DEC, widths[1]), qmap)]
        in_specs += [pl.BlockSpec((DEC_SEQ, wd), kmap, pipeline_mode=once) for wd in widths[2:]]
        in_specs += [pl.BlockSpec((PAST_LEN, wd), kmap, pipeline_mode=once) for wd in widths[2:4]]
        in_specs += [pl.BlockSpec((None, None, PAST_LEN, LANES), lambda b, i: (b, layer, 0, 0),
                                  pipeline_mode=once)] * 2
        args = [qm, gq, km, vm, gk, gv, *cache]
        out_spec = pl.BlockSpec((TQ_DEC, MLA_W + GQA_W), qmap)
        sk = PAST_LEN + DEC_SEQ
        sem = ("arbitrary", "arbitrary")
    else:
        grid = (t // SEQ,)
        bmap = lambda b: (b, 0)
        in_specs = [pl.BlockSpec((SEQ, wd), bmap) for wd in widths]
        args = [qm, gq, km, vm, gk, gv]
        out_spec = pl.BlockSpec((SEQ, MLA_W + GQA_W), bmap)
        sk = SEQ
        sem = ("arbitrary",)
    return pl.pallas_call(
        functools.partial(_attn_kernel, has_cache),
        grid=grid,
        in_specs=in_specs,
        out_specs=out_spec,
        out_shape=jax.ShapeDtypeStruct((t, MLA_W + GQA_W), BF16),
        scratch_shapes=[pltpu.VMEM(((MLA_HEADS + GQA_KV_HEADS) * VT_ROWS, sk), BF16)] + (
            [pltpu.VMEM((MLA_QW + LANES, sk), BF16)] if has_cache else []),
        compiler_params=_cparams(sem),
        name="attn_dec" if has_cache else "attn_ctx",
    )(*args)


def _ffn_kernel(layer, dec, hg_ref, at_ref, x_ref, mod_ref, w_out_ref, ln1g_ref, ln1b_ref, w_gu_ref, w_dn_ref,
                ln2g_ref, ln2b_ref, o_ref):
    lrow = slice(layer, layer + 1)
    mod = _mod_row(mod_ref, dec, DEC_SEQ // TM_FFN)
    g1 = mod[:, 2 * D_MODEL:3 * D_MODEL]
    sh2 = mod[:, 3 * D_MODEL:4 * D_MODEL]
    sc2 = mod[:, 4 * D_MODEL:5 * D_MODEL]
    g2 = mod[:, 5 * D_MODEL:6 * D_MODEL]
    y = _dot(hg_ref[...], w_out_ref[0:HG_W, :]) + _dot(at_ref[...], w_out_ref[HG_W:D_MODEL, :])
    x1 = _layernorm_rows(ALPHA * x_ref[...] + g1 * y, ln1g_ref[lrow, :], ln1b_ref[lrow, :])
    hf = (x1 * (1.0 + sc2) + sh2).astype(BF16)
    ch = D_FF // FF_CHUNKS
    f = None
    for c in range(FF_CHUNKS):
        gate = _dot(hf, w_gu_ref[:, c * ch:(c + 1) * ch])
        up = _dot(hf, w_gu_ref[:, D_FF + c * ch:D_FF + (c + 1) * ch])
        part = _dot((_silu(gate) * up).astype(BF16), w_dn_ref[c * ch:(c + 1) * ch, :])
        f = part if f is None else f + part
    o_ref[...] = _layernorm_rows(ALPHA * x1 + g2 * f, ln2g_ref[lrow, :], ln2b_ref[lrow, :])


def _ffn_call(hg, at, x, mod, w, prm, layer, dec):
    t = x.shape[0]
    row = lambda i: (i, 0)
    once = pl.Buffered(1)
    vec = pl.BlockSpec((DEPTH, D_MODEL), lambda i: (0, 0), pipeline_mode=once)
    return pl.pallas_call(
        functools.partial(_ffn_kernel, layer, dec),
        grid=(t // TM_FFN,),
        in_specs=[
            pl.BlockSpec((TM_FFN, HG_W), row),
            pl.BlockSpec((TM_FFN, MLA_W + GQA_W), row),
            pl.BlockSpec((TM_FFN, D_MODEL), row),
            _layer_spec((MOD_ROWS, N_ADA), layer, pipeline_mode=once),
            _layer_spec((D_MODEL, D_MODEL), layer, pipeline_mode=once),
            vec, vec,
            _layer_spec((D_MODEL, 2 * D_FF), layer, pipeline_mode=once),
            _layer_spec((D_FF, D_MODEL), layer, pipeline_mode=once),
            vec, vec,
        ],
        out_specs=pl.BlockSpec((TM_FFN, D_MODEL), row),
        out_shape=jax.ShapeDtypeStruct((t, D_MODEL), F32),
        compiler_params=_cparams(("arbitrary",)),
        name="ffn_dec" if dec else "ffn_ctx",
    )(hg, at, x, mod, w["w_out"], prm["ln1_g"], prm["ln1_b"], w["w_gu"], w["w_dn"], prm["ln2_g"], prm["ln2_b"])


def _rope_tables(width, rot_off, rot_dim):
    t = np.arange(DEC_SEQ)
    pos = (t // GRID_W, t % GRID_W)
    half = rot_dim // 4
    freqs = ROPE_THETA ** (-np.arange(half, dtype=np.float64) / half)
    c = np.ones((DEC_SEQ, LANES), np.float64)
    sa = np.zeros((DEC_SEQ, LANES), np.float64)
    sb = np.zeros((DEC_SEQ, LANES), np.float64)
    for blk in range(LANES // width):
        for axis in range(2):
            base = blk * width + rot_off + axis * 2 * half
            ang = pos[axis][:, None].astype(np.float64) * freqs[None, :]
            c[:, base:base + half] = np.cos(ang)
            c[:, base + half:base + 2 * half] = np.cos(ang)
            sa[:, base:base + half] = -np.sin(ang)
            sb[:, base + half:base + 2 * half] = np.sin(ang)
    return tuple(jnp.asarray(a, F32) for a in (c, sa, sb))


def _constants():
    C = HG_CHUNK
    t = np.arange(C)
    tri_f = (t[None, :] <= t[:, None]).astype(np.float32)
    tri_b = (t[None, :] >= t[:, None]).astype(np.float32)
    x = t[:, None] ^ t[None, :]
    top = np.where(x > 0, np.floor(np.log2(np.maximum(x, 1))).astype(np.int32), -1)
    same = (t[:, None] // SUBLANES) == (t[None, :] // SUBLANES)
    lvl_f = np.where(same, np.where(t[:, None] >= t[None, :], 0, -1),
                     np.where(t[:, None] > t[None, :], top, -1)).astype(np.int32)
    lvl_b = np.where(same, np.where(t[:, None] <= t[None, :], 0, -1),
                     np.where(t[:, None] < t[None, :], top, -1)).astype(np.int32)
    grp = np.arange(LANES) // HG_DK
    blk = (grp[:, None] == grp[None, :]).astype(np.float32)
    stk = np.zeros((SUBLANES, LANES, 2, C), np.float32)
    for s in range(SUBLANES):
        for hb in range(2):
            stk[s, hb * HG_DK:(hb + 1) * HG_DK, hb, s::SUBLANES] = 1.0
    stk = stk.reshape(SUBLANES * LANES, 2 * C)
    place = np.zeros((MLA_ROPE, MLA_QW), np.float32)
    for hd in range(MLA_HEADS):
        place[np.arange(MLA_ROPE), hd * HEAD_PAD + MLA_NOPE + np.arange(MLA_ROPE)] = 1.0
    mla_c, mla_sa, mla_sb = _rope_tables(HEAD_PAD, MLA_NOPE, MLA_ROPE)
    gqa_c, gqa_sa, gqa_sb = _rope_tables(GQA_HD, 0, GQA_HD)
    return {
        "tri_f": jnp.asarray(tri_f, BF16), "tri_b": jnp.asarray(tri_b, BF16),
        "lvl_f": jnp.asarray(np.concatenate([lvl_f, lvl_f], axis=0)),
        "lvl_b": jnp.asarray(np.concatenate([lvl_b, lvl_b], axis=0)),
        "stk": jnp.asarray(stk, BF16), "ones": jnp.asarray(blk, BF16), "ones64": jnp.asarray(blk / GQA_HD, BF16),
        "blk": jnp.asarray(blk, F32), "kpe_place": jnp.asarray(place, BF16),
        "mla_c": mla_c, "mla_sa": mla_sa, "mla_sb": mla_sb, "gqa_c": gqa_c, "gqa_sa": gqa_sa, "gqa_sb": gqa_sb,
    }


def kernel(x_prompt, x_sample, cache_mla_ckv, cache_mla_kpe, cache_gqa_k, cache_gqa_v, state_hgrn, c, c_ctx, w_ada,
           b_ada, w_in, hg_lb, hg_norm, mla_q_norm, mla_w_uq, mla_kv_norm, mla_w_ukv, gqa_q_norm, gqa_k_norm, w_out,
           ln1_g, ln1_b, w_ffn_in, w_ffn_out, ln2_g, ln2_b):
    cst = _constants()
    prm = {"mla_q_norm": mla_q_norm, "mla_kv_norm": mla_kv_norm, "gqa_q_norm": gqa_q_norm, "gqa_k_norm": gqa_k_norm,
           "ln1_g": ln1_g, "ln1_b": ln1_b, "ln2_g": ln2_g, "ln2_b": ln2_b}
    w = {"w_in": _prep_in_call(w_in)}
    w["w_uq"], w["w_uk"], w["w_uv"] = _prep_mla_call(mla_w_uq, mla_w_ukv)
    w["w_out"], w["w_gu"], w["w_dn"] = _prep_cast_call(w_out, w_ffn_in, w_ffn_out)

    cvec = jnp.concatenate([c_ctx[None, :], c, jnp.zeros((MOD_ROWS - 1 - DEC_BATCH, D_MODEL), F32)], axis=0)
    mod = _ada_call(cvec, w_ada, b_ada)
    cache_gk = cache_gqa_k.reshape(DEC_BATCH, DEPTH, PAST_LEN, LANES)
    cache_gv = cache_gqa_v.reshape(DEC_BATCH, DEPTH, PAST_LEN, LANES)

    x_ctx = x_prompt.reshape(BATCH * SEQ, D_MODEL)
    x_dec = x_sample.reshape(DEC_BATCH * DEC_SEQ, D_MODEL)
    states, hg_state = None, None
    for l in range(DEPTH):
        zhf, zhb, qm, km, vm, gq, gk, gv, *states = _inproj_call(x_ctx, mod, w, prm, cst, l, rope=False,
                                                          prev_states=states)
        hg_out, hg_state = _hgrn_call(zhf, zhb, hg_lb, hg_norm, cst, l, BATCH, SEQ, prev_state_out=hg_state)
        at = _attn_call(qm, gq, km, vm, gk, gv, None, l)
        x_ctx = _ffn_call(hg_out, at, x_ctx, mod, w, prm, l, dec=False)

        zhf, zhb, qm, km, vm, gq, gk, gv = _inproj_call(x_dec, mod, w, prm, cst, l, rope=True)
        hg_out = _hgrn_call(zhf, zhb, hg_lb, hg_norm, cst, l, DEC_BATCH, DEC_SEQ, state_in=state_hgrn)[0]
        kc, vc = _cachekv_call(cache_mla_ckv, cache_mla_kpe, w, cst, l)
        at = _attn_call(qm, gq, km, vm, gk, gv, (kc, vc, cache_gk, cache_gv), l)
        x_dec = _ffn_call(hg_out, at, x_dec, mod, w, prm, l, dec=True)

    ckv, kpe, gk32, gv32 = states
    kv_shape = (BATCH, DEPTH, SEQ, GQA_KV_HEADS, GQA_HD)
    return (x_ctx.reshape(BATCH, SEQ, D_MODEL), x_dec.reshape(DEC_BATCH, DEC_SEQ, D_MODEL),
            ckv, kpe, gk32.reshape(kv_shape), gv32.reshape(kv_shape), hg_state)
```

```python
import functools

import numpy as np
import jax
import jax.numpy as jnp
from jax import lax
from jax.experimental import pallas as pl
from jax.experimental.pallas import tpu as pltpu

F32 = jnp.float32
BF16 = jnp.bfloat16

D_MODEL = 1024
BATCH = 32
SEQ = 256
DEPTH = 2
DEC_BATCH = 2
DEC_SEQ = 2048
PAST_LEN = 256
GRID_W = 64
ROPE_THETA = 10000.0
EPS = 1e-6
GATE_FLOOR = 1e-30
HG_HEADS = 4
HG_DK = 64
HG_DV = 64
MLA_HEADS = 6
MLA_NOPE = 64
MLA_ROPE = 32
MLA_V = 64
MLA_Q_RANK = 256
MLA_KV_RANK = 128
GQA_HEADS = 6
GQA_KV_HEADS = 2
GQA_HD = 64
HG_W = HG_HEADS * HG_DV
MLA_W = MLA_HEADS * MLA_V
GQA_W = GQA_HEADS * GQA_HD
D_FF = 2816
MLA_SCALE = (MLA_NOPE + MLA_ROPE) ** -0.5
GQA_SCALE = GQA_HD ** -0.5
ALPHA = (2 * DEPTH) ** 0.25
LOG2E = 1.4426950408889634

LANES = 128
SUBLANES = 8
HEAD_PAD = 128
MLA_QW = MLA_HEADS * HEAD_PAD
N_ADA = 6 * D_MODEL
MOD_ROWS = 8

N_IN = 5 * HG_HEADS * HG_DK + MLA_Q_RANK + MLA_KV_RANK + MLA_ROPE + GQA_W + 2 * GQA_KV_HEADS * GQA_HD
SRC_KPE = 5 * HG_HEADS * HG_DK + MLA_Q_RANK + MLA_KV_RANK
SRC_GQ = SRC_KPE + MLA_ROPE
SRC_GK = SRC_GQ + GQA_W
SRC_GV = SRC_GK + GQA_KV_HEADS * GQA_HD
HG_COLS = 5 * HG_HEADS * HG_DK
OFF_MQ = HG_COLS
OFF_MKV = OFF_MQ + MLA_Q_RANK
OFF_GQ = OFF_MKV + MLA_KV_RANK
OFF_GK = OFF_GQ + GQA_W
OFF_GV = OFF_GK + GQA_KV_HEADS * GQA_HD
OFF_KPE_PAD = OFF_GV + GQA_KV_HEADS * GQA_HD
OFF_KPE = OFF_KPE_PAD + HEAD_PAD
N_Z = OFF_KPE + MLA_ROPE

TM_IN = 512
TM_FFN = 512
TQ_DEC = 512
GQA_ROWS = 256
ATT_GROUP_CTX = 9
ATT_GROUP_DEC = 1
VT_ROWS = 80
HG_CHUNK = 128
HG_UNROLL = 2
FF_CHUNKS = 2
PREP_STEPS = 4
VMEM_LIMIT = 56 * 1024 * 1024


def _cparams(sem):
    return pltpu.CompilerParams(dimension_semantics=sem, vmem_limit_bytes=VMEM_LIMIT)


def _const_spec(shape):
    nd = len(shape)
    return pl.BlockSpec(shape, lambda *_: (0,) * nd)


def _layer_spec(shape, layer, **kw):
    nd = len(shape)
    return pl.BlockSpec((None,) + tuple(shape), lambda *_: (layer,) + (0,) * nd, **kw)


_ANY = pl.BlockSpec(memory_space=pl.ANY)


def _silu(x):
    return x * jax.nn.sigmoid(x)


def _dot(a, b):
    return jnp.dot(a, b, preferred_element_type=F32)


def _dot_nt(a, b):
    return lax.dot_general(a, b, (((1,), (1,)), ((), ())), preferred_element_type=F32)


def _dot_tn(a, b):
    return lax.dot_general(a, b, (((0,), (0,)), ((), ())), preferred_element_type=F32)


def _rms_rows(x, g):
    y = x * lax.rsqrt(jnp.mean(x * x, axis=-1, keepdims=True) + EPS)
    return y * g


def _layernorm_rows(x, g, b):
    xc = x - jnp.mean(x, axis=-1, keepdims=True)
    var = jnp.mean(xc * xc, axis=-1, keepdims=True)
    return xc * lax.rsqrt(var + EPS) * g + b


def _group_mean(x, ones_blk):
    return _dot(x.astype(BF16), ones_blk)


def _twice(g):
    return jnp.concatenate([g, g], axis=1)


_GQ_HEAD_ORDER = (0, 3, 1, 4, 2, 5)


def _prep_in_kernel(w_ref, o_ref):
    x = w_ref[...]
    rows = x.shape[0]
    kr = x[:, SRC_KPE:SRC_GQ]
    gq = [x[:, SRC_GQ + h * GQA_HD:SRC_GQ + (h + 1) * GQA_HD] for h in _GQ_HEAD_ORDER]
    pieces = [x[:, 0:SRC_KPE]] + gq + [
        x[:, SRC_GK:SRC_GV], x[:, SRC_GV:N_IN],
        jnp.zeros((rows, MLA_NOPE), F32), kr, jnp.zeros((rows, HEAD_PAD - MLA_NOPE - MLA_ROPE), F32), kr]
    o_ref[...] = jnp.concatenate(pieces, axis=1).astype(BF16)


def _prep_in_call(w_in):
    rb = 128
    return pl.pallas_call(
        _prep_in_kernel,
        grid=(DEPTH, D_MODEL // rb),
        in_specs=[pl.BlockSpec((None, rb, N_IN), lambda l, j: (l, j, 0))],
        out_specs=pl.BlockSpec((None, rb, N_Z), lambda l, j: (l, j, 0)),
        out_shape=jax.ShapeDtypeStruct((DEPTH, D_MODEL, N_Z), BF16),
        compiler_params=_cparams(("arbitrary", "arbitrary")),
        name="prep_in",
    )(w_in)


def _prep_mla_kernel(uq_ref, ukv_ref, oq_ref, ok_ref, ov_ref):
    uq = uq_ref[...]
    dq = MLA_NOPE + MLA_ROPE
    zq = jnp.zeros((MLA_Q_RANK, HEAD_PAD - dq), F32)
    oq_ref[...] = jnp.concatenate(
        [p for h in range(MLA_HEADS) for p in (uq[:, h * dq:(h + 1) * dq], zq)], axis=1).astype(BF16)
    ukv = ukv_ref[...]
    lane = lax.broadcasted_iota(jnp.int32, ukv.shape, 1)
    ok_ref[...] = jnp.where((lane & (HEAD_PAD - 1)) < MLA_NOPE, ukv, 0.0).astype(BF16)
    ov_ref[...] = jnp.concatenate(
        [ukv[:, h * HEAD_PAD + MLA_NOPE:(h + 1) * HEAD_PAD] for h in range(MLA_HEADS)], axis=1).astype(BF16)


def _prep_mla_call(w_uq, w_ukv):
    lay = lambda n: pl.BlockSpec((None, n[0], n[1]), lambda l: (l, 0, 0))
    shapes = [(MLA_Q_RANK, MLA_QW), (MLA_KV_RANK, MLA_QW), (MLA_KV_RANK, MLA_W)]
    return pl.pallas_call(
        _prep_mla_kernel,
        grid=(DEPTH,),
        in_specs=[lay((MLA_Q_RANK, MLA_HEADS * (MLA_NOPE + MLA_ROPE))), lay((MLA_KV_RANK, MLA_QW))],
        out_specs=[lay(s) for s in shapes],
        out_shape=[jax.ShapeDtypeStruct((DEPTH,) + s, BF16) for s in shapes],
        compiler_params=_cparams(("arbitrary",)),
        name="prep_mla",
    )(w_uq, w_ukv)


def _prep_cast_kernel(a_ref, b_ref, c_ref, oa_ref, ob_ref, oc_ref):
    oa_ref[...] = a_ref[...].astype(BF16)
    ob_ref[...] = b_ref[...].astype(BF16)
    oc_ref[...] = c_ref[...].astype(BF16)


def _prep_cast_call(w_out, w_ffn_in, w_ffn_out):
    ws = (w_out, w_ffn_in, w_ffn_out)
    specs = [pl.BlockSpec((None, w.shape[1] // PREP_STEPS, w.shape[2]), lambda l, j: (l, j, 0)) for w in ws]
    return pl.pallas_call(
        _prep_cast_kernel,
        grid=(DEPTH, PREP_STEPS),
        in_specs=specs,
        out_specs=specs,
        out_shape=[jax.ShapeDtypeStruct(w.shape, BF16) for w in ws],
        compiler_params=_cparams(("arbitrary", "arbitrary")),
        name="prep_cast",
    )(*ws)


def _ada_kernel(c_ref, w_ref, b_ref, o_ref):
    s = _silu(c_ref[...]).astype(BF16)
    o_ref[...] = _dot(s, w_ref[...].astype(BF16)) + b_ref[pl.ds(pl.program_id(0), 1), :]


def _ada_call(cvec, w_ada, b_ada):
    tn = 1536
    return pl.pallas_call(
        _ada_kernel,
        grid=(DEPTH, N_ADA // tn),
        in_specs=[
            pl.BlockSpec((MOD_ROWS, D_MODEL), lambda l, j: (0, 0)),
            pl.BlockSpec((None, D_MODEL, tn), lambda l, j: (l, 0, j)),
            pl.BlockSpec((DEPTH, tn), lambda l, j: (0, j)),
        ],
        out_specs=pl.BlockSpec((None, MOD_ROWS, tn), lambda l, j: (l, 0, j)),
        out_shape=jax.ShapeDtypeStruct((DEPTH, MOD_ROWS, N_ADA), F32),
        compiler_params=_cparams(("arbitrary", "arbitrary")),
        name="ada_mod",
    )(cvec, w_ada, b_ada)


def _mod_row(mod_ref, dec, tiles_per_seq):
    if not dec:
        return mod_ref[0:1, :]
    return mod_ref[pl.ds(1 + pl.program_id(0) // tiles_per_seq, 1), :]


def _rope_block(x, c, sa, sb, half):
    return x * c + pltpu.roll(x, LANES - half, 1) * sa + pltpu.roll(x, half, 1) * sb


def _inproj_kernel(layer, rope, x_ref, mod_ref, w_in_ref, w_uq_ref, w_uk_ref, w_uv_ref, qn_ref, kvn_ref, gqn_ref,
                   gkn_ref, ones_ref, *rest):
    if rope:
        (mc_ref, msa_ref, msb_ref, gc_ref, gsa_ref, gsb_ref,
         zhf_ref, zhb_ref, qm_ref, km_ref, vm_ref, gq_ref, gk_ref, gv_ref) = rest
    else:
        rest = rest[-12:]
        (zhf_ref, zhb_ref, qm_ref, km_ref, vm_ref, gq_ref, gk_ref, gv_ref,
         ckv_ref, kpe_ref, gk32_ref, gv32_ref) = rest
    lrow = slice(layer, layer + 1)
    qk_base = LOG2E
    mod = _mod_row(mod_ref, rope, DEC_SEQ // TM_IN)
    sh1 = mod[:, 0:D_MODEL]
    sc1 = mod[:, D_MODEL:2 * D_MODEL]
    h = (x_ref[...] * (1.0 + sc1) + sh1).astype(BF16)
    z = _dot(h, w_in_ref[...])
    gw = HG_HEADS * HG_DK
    zhf_ref[...] = z[:, gw:3 * gw]
    zhb_ref[:, 0:gw] = z[:, 0:gw].astype(BF16)
    zhb_ref[:, gw:3 * gw] = z[:, 3 * gw:HG_COLS].astype(BF16)
    mq = z[:, OFF_MQ:OFF_MKV]
    mkv = z[:, OFF_MKV:OFF_GQ]
    gq = z[:, OFF_GQ:OFF_GK]
    gk = z[:, OFF_GK:OFF_GV]
    gv = z[:, OFF_GV:OFF_KPE_PAD]
    kpe_pad = z[:, OFF_KPE_PAD:OFF_KPE]
    ones = ones_ref[...]
    gqn = _twice(gqn_ref[lrow, :])
    gkn_g = _twice(gkn_ref[lrow, :])

    q = _dot(_rms_rows(mq, qn_ref[lrow, :]).astype(BF16), w_uq_ref[...])
    ckv = _rms_rows(mkv, kvn_ref[lrow, :])
    ckv_b = ckv.astype(BF16)
    kn = _dot(ckv_b, w_uk_ref[...])
    vm_ref[...] = _dot(ckv_b, w_uv_ref[...]).astype(BF16)
    if rope:
        mc, msa, msb = mc_ref[...], msa_ref[...], msb_ref[...]
        kpe_pad = _rope_block(kpe_pad, mc, msa, msb, MLA_ROPE // 4)
    for hd in range(MLA_HEADS):
        sl = slice(hd * HEAD_PAD, (hd + 1) * HEAD_PAD)
        qh = q[:, sl]
        if rope:
            qh = _rope_block(qh, mc, msa, msb, MLA_ROPE // 4)
        qm_ref[:, sl] = (qh * (MLA_SCALE * qk_base)).astype(BF16)
        km_ref[:, sl] = (kn[:, sl] + kpe_pad).astype(BF16)

    gkn = gk * lax.rsqrt(_group_mean(gk * gk, ones) + EPS) * gkn_g
    for p in range(GQA_W // LANES):
        sl = slice(p * LANES, (p + 1) * LANES)
        gqp = gq[:, sl]
        gqp = gqp * lax.rsqrt(_group_mean(gqp * gqp, ones) + EPS) * gqn
        if rope:
            gqp = _rope_block(gqp, gc_ref[...], gsa_ref[...], gsb_ref[...], GQA_HD // 4)
        gq_ref[:, sl] = (gqp * (GQA_SCALE * qk_base)).astype(BF16)
    if rope:
        gk_ref[...] = _rope_block(gkn, gc_ref[...], gsa_ref[...], gsb_ref[...], GQA_HD // 4).astype(BF16)
    else:
        gk_ref[...] = gkn.astype(BF16)
        kpe = z[:, OFF_KPE:N_Z]
        for b in range(TM_IN // SEQ):
            rows = slice(b * SEQ, (b + 1) * SEQ)
            ckv_ref[b] = ckv[rows]
            kpe_ref[b] = kpe[rows]
            gk32_ref[b] = gkn[rows]
            gv32_ref[b] = gv[rows]
    gv_ref[...] = gv.astype(BF16)


def _inproj_call(x, mod, w, prm, cst, layer, rope, prev_states=None):
    t = x.shape[0]
    nt = t // TM_IN
    row = lambda i: (i, 0)
    in_specs = [
        pl.BlockSpec((TM_IN, D_MODEL), row),
        _layer_spec((MOD_ROWS, N_ADA), layer),
        _layer_spec((D_MODEL, N_Z), layer),
        _layer_spec((MLA_Q_RANK, MLA_QW), layer),
        _layer_spec((MLA_KV_RANK, MLA_QW), layer),
        _layer_spec((MLA_KV_RANK, MLA_W), layer),
        _const_spec((DEPTH, MLA_Q_RANK)),
        _const_spec((DEPTH, MLA_KV_RANK)),
        _const_spec((DEPTH, GQA_HD)),
        _const_spec((DEPTH, GQA_HD)),
        _const_spec((LANES, LANES)),
    ]
    args = [x, mod, w["w_in"], w["w_uq"], w["w_uk"], w["w_uv"], prm["mla_q_norm"], prm["mla_kv_norm"],
            prm["gqa_q_norm"], prm["gqa_k_norm"], cst["ones64"]]
    outs = [(2 * HG_HEADS * HG_DK, F32), (3 * HG_HEADS * HG_DK, BF16), (MLA_QW, BF16), (MLA_QW, BF16), (MLA_W, BF16),
            (GQA_W, BF16), (LANES, BF16), (LANES, BF16)]
    out_specs = [pl.BlockSpec((TM_IN, wd), row) for wd, _ in outs]
    out_shape = [jax.ShapeDtypeStruct((t, wd), dt) for wd, dt in outs]
    aliases = {}
    if rope:
        tab = lambda i: (i % (DEC_SEQ // TM_IN), 0)
        in_specs += [pl.BlockSpec((TM_IN, LANES), tab)] * 6
        args += [cst["mla_c"], cst["mla_sa"], cst["mla_sb"], cst["gqa_c"], cst["gqa_sa"], cst["gqa_sb"]]
    else:
        per = TM_IN // SEQ
        for wd in (MLA_KV_RANK, MLA_ROPE, LANES, LANES):
            out_specs.append(pl.BlockSpec((per, None, SEQ, wd), lambda i: (i, layer, 0, 0)))
            out_shape.append(jax.ShapeDtypeStruct((BATCH, DEPTH, SEQ, wd), F32))
        if prev_states is not None:
            for k, arr in enumerate(prev_states):
                aliases[len(args)] = len(outs) + k
                in_specs.append(_ANY)
                args.append(arr)
    return pl.pallas_call(
        functools.partial(_inproj_kernel, layer, rope),
        grid=(nt,),
        in_specs=in_specs,
        out_specs=out_specs,
        out_shape=out_shape,
        input_output_aliases=aliases,
        compiler_params=_cparams(("arbitrary",)),
        name="inproj_dec" if rope else "inproj_ctx",
    )(*args)


def _cachekv_kernel(ckv_ref, kpe_ref, w_uk_ref, w_uv_ref, place_ref, k_ref, v_ref):
    ckv_b = ckv_ref[...].astype(BF16)
    k_ref[...] = (_dot(ckv_b, w_uk_ref[...]) + _dot(kpe_ref[...].astype(BF16), place_ref[...])).astype(BF16)
    v_ref[...] = _dot(ckv_b, w_uv_ref[...]).astype(BF16)


def _cachekv_call(cache_ckv, cache_kpe, w, cst, layer):
    t = DEC_BATCH * PAST_LEN
    cache = lambda wd: pl.BlockSpec((None, None, PAST_LEN, wd), lambda b: (b, layer, 0, 0))
    return pl.pallas_call(
        _cachekv_kernel,
        grid=(DEC_BATCH,),
        in_specs=[cache(MLA_KV_RANK), cache(MLA_ROPE),
                  _layer_spec((MLA_KV_RANK, MLA_QW), layer), _layer_spec((MLA_KV_RANK, MLA_W), layer),
                  _const_spec((MLA_ROPE, MLA_QW))],
        out_specs=[pl.BlockSpec((PAST_LEN, MLA_QW), lambda b: (b, 0)),
                   pl.BlockSpec((PAST_LEN, MLA_W), lambda b: (b, 0))],
        out_shape=[jax.ShapeDtypeStruct((t, MLA_QW), BF16), jax.ShapeDtypeStruct((t, MLA_W), BF16)],
        compiler_params=_cparams(("arbitrary",)),
        name="cache_kv",
    )(cache_ckv, cache_kpe, w["w_uk"], w["w_uv"], cst["kpe_place"])


def _bcast_row(x, r, rows):
    return jnp.broadcast_to(x[r:r + 1, :], (rows, x.shape[1]))


def _cumsum_rows(tri, x):
    hi = x.astype(BF16)
    r1 = x - hi.astype(F32)
    mid = r1.astype(BF16)
    lo = (r1 - mid.astype(F32)).astype(BF16)
    return _dot(tri, hi) + _dot(tri, mid) + _dot(tri, lo)


def _hgrn_chunks(items, states, tri, lvl, stk, blk, lane_lo):
    C = HG_CHUNK
    zero_b = jnp.zeros((SUBLANES, LANES), F32)
    trow = lax.broadcasted_iota(jnp.int32, (SUBLANES, LANES), 0)
    for it in items:
        it["c"] = _cumsum_rows(tri[it["fwd"]], it["lf"])
    for it in items:
        it["q_a"] = jnp.where(lane_lo, it["q"], 0.0)
        it["q_b"] = jnp.where(lane_lo, 0.0, it["q"])
        it["ck"] = it["c"] - jnp.log(it["kk"])

    m = C // 2
    bit = m.bit_length() - 1
    while m >= SUBLANES:
        for it in items:
            fwd, c = it["fwd"], it["c"]
            qa_rows, qb_rows, k_rows = [], [], []
            for b in range(C // m):
                rows = slice(b * m, (b + 1) * m)
                anchor = (b // 2) * 2 * m + (m - 1 if fwd else m)
                r = _bcast_row(c, anchor, m)
                if (b % 2 == 1) == fwd:
                    e = jnp.exp(c[rows] - r)
                    qa_rows.append(it["q_a"][rows] * e)
                    qb_rows.append(it["q_b"][rows] * e)
                    k_rows += [zero_b] * (m // SUBLANES)
                else:
                    e = jnp.exp(r - c[rows])
                    qa_rows += [zero_b] * (m // SUBLANES)
                    qb_rows += [zero_b] * (m // SUBLANES)
                    k_rows.append(it["kk"][rows] * e)
            prod = _dot_nt(jnp.concatenate(qa_rows + qb_rows, axis=0).astype(BF16),
                           jnp.concatenate(k_rows, axis=0).astype(BF16))
            it["a"] = prod if "a" not in it else jnp.where(lvl[fwd] == bit, prod, it["a"])
        m //= 2
        bit -= 1

    for it in items:
        it["w"] = []
    for j in range(C // SUBLANES):
        sl = slice(j * SUBLANES, (j + 1) * SUBLANES)
        for it in items:
            cs, cks, qs = it["c"][sl], it["ck"][sl], it["q"][sl]
            w_s = []
            for s in range(SUBLANES):
                e = jnp.exp(cs - _bcast_row(cks, s, SUBLANES))
                keep = (trow >= s) if it["fwd"] else (trow <= s)
                w_s.append(jnp.where(keep, qs * e, 0.0))
            it["w"].append(jnp.concatenate(w_s, axis=1))
    for it in items:
        red = _dot(jnp.concatenate(it["w"], axis=0).astype(BF16), stk)
        it["a"] = jnp.where(lvl[it["fwd"]] == 0,
                            jnp.concatenate([red[:, 0:LANES], red[:, LANES:2 * LANES]], axis=0), it["a"])
    for it in items:
        off = _dot(it["a"].astype(BF16), it["iv"].astype(BF16))
        it["o"] = jnp.where(lane_lo, off[0:C], off[C:2 * C])
        it["edge"] = _bcast_row(it["c"], C - 1 if it["fwd"] else 0, C)
        it["q_dec"] = (it["q"] * jnp.exp(it["c"])).astype(BF16)
        k_dec = (it["kk"] * jnp.exp(it["edge"] - it["c"])).astype(BF16)
        it["upd"] = _dot_tn(it["iv"].astype(BF16), k_dec) * blk

    states = list(states)
    outs = []
    for it in items:
        st = states[it["sid"]]
        outs.append(it["o"] + _dot_nt(it["q_dec"], st.astype(BF16)))
        states[it["sid"]] = st * jnp.exp(it["edge"][0:1]) + it["upd"]
    return outs, states


def _hgrn_kernel(layer, nchunks, has_state, zq_ref, zff_ref, zfb_ref, zi_ref, zg_ref, lb_ref, g_ref,
                 trif_ref, trib_ref, lvlf_ref, lvlb_ref, stk_ref, ones_ref, blk_ref, *rest):
    if has_state:
        s0_ref, o_ref, of_scr, ob_scr = rest
    else:
        o_ref, st_ref, of_scr, ob_scr = rest[-4:]
    C = HG_CHUNK
    npair = HG_HEADS // 2
    lane_lo = lax.broadcasted_iota(jnp.int32, (1, LANES), 1) < HG_DK
    pair_lanes = [slice(hp * LANES, (hp + 1) * LANES) for hp in range(npair)]

    lb_all = [lb_ref[j] for j in range(DEPTH)]
    mx = functools.reduce(jnp.maximum, lb_all)
    ex = [jnp.exp(v - mx) for v in lb_all]
    den = functools.reduce(lambda a, b: a + b, ex)
    sm = [v / den for v in ex]
    lbs = functools.reduce(lambda a, b: a + b, sm[:layer + 1]) - sm[0]

    def gates(zf, lbd):
        f = lbd + (1.0 - lbd) * jax.nn.sigmoid(zf)
        return jnp.log(jnp.maximum(f, GATE_FLOOR)), jnp.maximum(1.0 - f, 0.0)

    def block_diag_t(sa, sb):
        za = jnp.zeros((HG_DK, HG_DV), F32)
        top = jnp.concatenate([sa, za], axis=1)
        bot = jnp.concatenate([za, sb], axis=1)
        return jnp.concatenate([top, bot], axis=0).T

    if has_state:
        st0 = tuple(block_diag_t(s0_ref[d, 2 * hp], s0_ref[d, 2 * hp + 1]) for hp in range(npair) for d in range(2))
    else:
        st0 = (jnp.zeros((LANES, LANES), F32),) * (2 * npair)

    def body(j, carry):
        tri = {True: trif_ref[...], False: trib_ref[...]}
        lvl = {True: lvlf_ref[...], False: lvlb_ref[...]}
        items = []
        for u in range(HG_UNROLL):
            jf = j * HG_UNROLL + u
            for hp, ln in enumerate(pair_lanes):
                for fwd in (True, False):
                    rows = pl.ds(pl.multiple_of((jf if fwd else nchunks - 1 - jf) * C, C), C)
                    d = 0 if fwd else 1
                    lf, kk = gates((zff_ref if fwd else zfb_ref)[rows, ln], lbs[d:d + 1, ln])
                    items.append({"fwd": fwd, "sid": 2 * hp + d, "rows": rows, "ln": ln, "lf": lf, "kk": kk,
                                  "q": _silu(zq_ref[rows, ln].astype(F32)) * (HG_DK ** -0.5),
                                  "iv": zi_ref[rows, ln]})
        outs, states = _hgrn_chunks(items, carry, tri, lvl, stk_ref[...], blk_ref[...], lane_lo)
        for it, o in zip(items, outs):
            (of_scr if it["fwd"] else ob_scr)[it["rows"], it["ln"]] = o
        return tuple(states)

    states = lax.fori_loop(0, nchunks // HG_UNROLL, body, st0)

    g = _twice(g_ref[layer:layer + 1, :])
    for ln in pair_lanes:
        o = of_scr[:, ln] + ob_scr[:, ln]
        ms = _group_mean(o * o, ones_ref[...] * (1.0 / HG_DV))
        o_ref[:, ln] = (o * lax.rsqrt(ms + EPS) * g * _silu(zg_ref[:, ln].astype(F32))).astype(BF16)
    if not has_state:
        for hp in range(npair):
            for d in range(2):
                s_kv = states[2 * hp + d].T
                st_ref[d, 2 * hp] = s_kv[0:HG_DK, 0:HG_DV]
                st_ref[d, 2 * hp + 1] = s_kv[HG_DK:2 * HG_DK, HG_DV:2 * HG_DV]


def _hgrn_call(zhf, zhb, lb, g, cst, layer, nb, s, state_in=None, prev_state_out=None):
    has_state = state_in is not None
    nchunks = s // HG_CHUNK
    part = lambda k: pl.BlockSpec((s, HG_W), lambda b: (b, k))
    st_spec = pl.BlockSpec((None, None, 2, HG_HEADS, HG_DK, HG_DV), lambda b: (b, layer, 0, 0, 0, 0))
    in_specs = [part(0), part(0), part(1), part(1), part(2),
                _const_spec((DEPTH, 2, HG_W)),
                _const_spec((DEPTH, HG_DV)),
                _const_spec((HG_CHUNK, HG_CHUNK)), _const_spec((HG_CHUNK, HG_CHUNK)),
                _const_spec((2 * HG_CHUNK, HG_CHUNK)), _const_spec((2 * HG_CHUNK, HG_CHUNK)),
                _const_spec((SUBLANES * LANES, 2 * LANES)),
                _const_spec((LANES, LANES)), _const_spec((LANES, LANES))]
    args = [zhb, zhf, zhf, zhb, zhb, lb, g, cst["tri_f"], cst["tri_b"], cst["lvl_f"], cst["lvl_b"], cst["stk"],
            cst["ones"], cst["blk"]]
    out_specs = [pl.BlockSpec((s, HG_W), lambda b: (b, 0))]
    out_shape = [jax.ShapeDtypeStruct((nb * s, HG_W), BF16)]
    aliases = {}
    if has_state:
        in_specs.append(st_spec)
        args.append(state_in)
    else:
        out_specs.append(st_spec)
        out_shape.append(jax.ShapeDtypeStruct((nb, DEPTH, 2, HG_HEADS, HG_DK, HG_DV), F32))
        if prev_state_out is not None:
            aliases[len(args)] = 1
            in_specs.append(_ANY)
            args.append(prev_state_out)
    return pl.pallas_call(
        functools.partial(_hgrn_kernel, layer, nchunks, has_state),
        grid=(nb,),
        in_specs=in_specs,
        out_specs=out_specs,
        out_shape=out_shape,
        input_output_aliases=aliases,
        scratch_shapes=[pltpu.VMEM((s, HG_W), F32), pltpu.VMEM((s, HG_W), F32)],
        compiler_params=_cparams(("arbitrary",)),
        name="hgrn_dec" if has_state else "hgrn_ctx",
    )(*args)


def _attend_t(jobs, ss):
    add = lambda a, b: a + b
    mxs = [functools.reduce(jnp.maximum, [jnp.max(s, axis=-1, keepdims=True) for s in sj]) for sj in ss]
    ebs = [[jnp.exp2(s - mx).astype(BF16) for s in sj] for sj, mx in zip(ss, mxs)]
    prods = [[functools.reduce(add, [_dot_nt(vt, e[rows]) for vt, e in zip(vts, ej)]) for rows, vts in parts]
             for ej, (_, _, parts) in zip(ebs, jobs)]
    return [[pr[0:MLA_V] * (1.0 / pr[MLA_V:MLA_V + 1]) for pr in pj] for pj in prods]


def _scores(jobs):
    return [[_dot_nt(q, k) for k in ks] for q, ks, _ in jobs]


def _attn_kernel(has_cache, qm_ref, gq_ref, km_ref, vm_ref, gk_ref, gv_ref, *rest):
    if has_cache:
        kc_ref, vc_ref, gkc_ref, gvc_ref, o_ref, vt_scr = rest
    else:
        o_ref, vt_scr = rest
    sq = qm_ref.shape[0]
    sk = vt_scr.shape[1]
    nblk = MLA_HEADS + GQA_KV_HEADS
    cols = [slice(0, PAST_LEN), slice(PAST_LEN, sk)] if has_cache else [slice(0, sk)]

    def fill():
        tr = lambda v: v.astype(F32).T.astype(BF16)
        srcs = [(vc_ref, gvc_ref), (vm_ref, gv_ref)] if has_cache else [(vm_ref, gv_ref)]
        for cl, (mla_ref, gqa_ref) in zip(cols, srcs):
            mla_t, gqa_t = tr(mla_ref[...]), tr(gqa_ref[...])
            tail = (lax.broadcasted_iota(jnp.int32, (VT_ROWS - MLA_V, cl.stop - cl.start), 0) == 0).astype(BF16)
            for b in range(nblk):
                src = mla_t[b * MLA_V:(b + 1) * MLA_V] if b < MLA_HEADS else \
                    gqa_t[(b - MLA_HEADS) * GQA_HD:(b - MLA_HEADS + 1) * GQA_HD]
                vt_scr[b * VT_ROWS:b * VT_ROWS + MLA_V, cl] = src
                vt_scr[b * VT_ROWS + MLA_V:(b + 1) * VT_ROWS, cl] = tail

    if has_cache:
        pl.when(pl.program_id(1) == 0)(fill)
    else:
        fill()

    def vt(b):
        return [vt_scr[b * VT_ROWS:(b + 1) * VT_ROWS, cl] for cl in cols]

    lo = lax.broadcasted_iota(jnp.int32, (GQA_ROWS, LANES), 1) < GQA_HD
    zero = jnp.zeros((GQA_ROWS, LANES), BF16)
    gks = [gkc_ref[...].astype(BF16), gk_ref[...]] if has_cache else [gk_ref[...]]
    jobs = []
    for hd in range(MLA_HEADS):
        sl = slice(hd * HEAD_PAD, (hd + 1) * HEAD_PAD)
        ks = [kc_ref[:, sl], km_ref[:, sl]] if has_cache else [km_ref[:, sl]]
        jobs.append((qm_ref[:, sl], ks, [(slice(0, sq), vt(hd))]))
    halves = [(slice(0, GQA_ROWS), vt(MLA_HEADS)), (slice(GQA_ROWS, 2 * GQA_ROWS), vt(MLA_HEADS + 1))]
    for r0 in range(0, sq, GQA_ROWS):
        for p in range(GQA_W // LANES):
            qp = gq_ref[r0:r0 + GQA_ROWS, p * LANES:(p + 1) * LANES]
            jobs.append((jnp.concatenate([jnp.where(lo, qp, zero), jnp.where(lo, zero, qp)], axis=0), gks, halves))
    group = ATT_GROUP_DEC if has_cache else ATT_GROUP_CTX
    groups = [jobs[j0:j0 + group] for j0 in range(0, len(jobs), group)]
    outs = []
    ss_next = _scores(groups[0])
    for g, grp in enumerate(groups):
        ss = ss_next
        if g + 1 < len(groups):
            ss_next = _scores(groups[g + 1])
        outs += _attend_t(grp, ss)
    for p in range(MLA_HEADS // 2):
        pair = jnp.concatenate([outs[2 * p][0], outs[2 * p + 1][0]], axis=0)
        o_ref[:, p * LANES:(p + 1) * LANES] = pair.T.astype(BF16)
    for i, r0 in enumerate(range(0, sq, GQA_ROWS)):
        gres = [jnp.concatenate(outs[MLA_HEADS + 3 * i + p], axis=0).T for p in range(GQA_W // LANES)]
        swapped = pltpu.roll(gres[1], GQA_HD, 1)
        blocks = (jnp.where(lo, gres[0], swapped), jnp.where(lo, gres[2], gres[0]), jnp.where(lo, swapped, gres[2]))
        for p, blkv in enumerate(blocks):
            o_ref[r0:r0 + GQA_ROWS, MLA_W + p * LANES:MLA_W + (p + 1) * LANES] = blkv.astype(BF16)


def _attn_call(qm, gq, km, vm, gk, gv, cache, layer):
    has_cache = cache is not None
    t = qm.shape[0]
    widths = (MLA_QW, GQA_W, MLA_QW, MLA_W, LANES, LANES)
    if has_cache:
        nq = DEC_SEQ // TQ_DEC
        grid = (DEC_BATCH, nq)
        qmap = lambda b, i: (b * nq + i, 0)
        kmap = lambda b, i: (b, 0)
        once = pl.Buffered(1)
        in_specs = [pl.BlockSpec((TQ_DEC, widths[0]), qmap), pl.BlockSpec((TQ_DEC, widths[1]), qmap)]
        in_specs += [pl.BlockSpec((DEC_SEQ, wd), kmap, pipeline_mode=once) for wd in widths[2:]]
        in_specs += [pl.BlockSpec((PAST_LEN, wd), kmap, pipeline_mode=once) for wd in widths[2:4]]
        in_specs += [pl.BlockSpec((None, None, PAST_LEN, LANES), lambda b, i: (b, layer, 0, 0),
                                  pipeline_mode=once)] * 2
        args = [qm, gq, km, vm, gk, gv, *cache]
        out_spec = pl.BlockSpec((TQ_DEC, MLA_W + GQA_W), qmap)
        sk = PAST_LEN + DEC_SEQ
        sem = ("arbitrary", "arbitrary")
    else:
        grid = (t // SEQ,)
        bmap = lambda b: (b, 0)
        in_specs = [pl.BlockSpec((SEQ, wd), bmap) for wd in widths]
        args = [qm, gq, km, vm, gk, gv]
        out_spec = pl.BlockSpec((SEQ, MLA_W + GQA_W), bmap)
        sk = SEQ
        sem = ("arbitrary",)
    return pl.pallas_call(
        functools.partial(_attn_kernel, has_cache),
        grid=grid,
        in_specs=in_specs,
        out_specs=out_spec,
        out_shape=jax.ShapeDtypeStruct((t, MLA_W + GQA_W), BF16),
        scratch_shapes=[pltpu.VMEM(((MLA_HEADS + GQA_KV_HEADS) * VT_ROWS, sk), BF16)],
        compiler_params=_cparams(sem),
        name="attn_dec" if has_cache else "attn_ctx",
    )(*args)


def _ffn_kernel(layer, dec, hg_ref, at_ref, x_ref, mod_ref, w_out_ref, ln1g_ref, ln1b_ref, w_gu_ref, w_dn_ref,
                ln2g_ref, ln2b_ref, o_ref):
    lrow = slice(layer, layer + 1)
    mod = _mod_row(mod_ref, dec, DEC_SEQ // TM_FFN)
    g1 = mod[:, 2 * D_MODEL:3 * D_MODEL]
    sh2 = mod[:, 3 * D_MODEL:4 * D_MODEL]
    sc2 = mod[:, 4 * D_MODEL:5 * D_MODEL]
    g2 = mod[:, 5 * D_MODEL:6 * D_MODEL]
    y = _dot(hg_ref[...], w_out_ref[0:HG_W, :]) + _dot(at_ref[...], w_out_ref[HG_W:D_MODEL, :])
    x1 = _layernorm_rows(ALPHA * x_ref[...] + g1 * y, ln1g_ref[lrow, :], ln1b_ref[lrow, :])
    hf = (x1 * (1.0 + sc2) + sh2).astype(BF16)
    ch = D_FF // FF_CHUNKS
    f = None
    for c in range(FF_CHUNKS):
        gate = _dot(hf, w_gu_ref[:, c * ch:(c + 1) * ch])
        up = _dot(hf, w_gu_ref[:, D_FF + c * ch:D_FF + (c + 1) * ch])
        part = _dot((_silu(gate) * up).astype(BF16), w_dn_ref[c * ch:(c + 1) * ch, :])
        f = part if f is None else f + part
    o_ref[...] = _layernorm_rows(ALPHA * x1 + g2 * f, ln2g_ref[lrow, :], ln2b_ref[lrow, :])


def _ffn_call(hg, at, x, mod, w, prm, layer, dec):
    t = x.shape[0]
    row = lambda i: (i, 0)
    once = pl.Buffered(1)
    vec = pl.BlockSpec((DEPTH, D_MODEL), lambda i: (0, 0), pipeline_mode=once)
    return pl.pallas_call(
        functools.partial(_ffn_kernel, layer, dec),
        grid=(t // TM_FFN,),
        in_specs=[
            pl.BlockSpec((TM_FFN, HG_W), row),
            pl.BlockSpec((TM_FFN, MLA_W + GQA_W), row),
            pl.BlockSpec((TM_FFN, D_MODEL), row),
            _layer_spec((MOD_ROWS, N_ADA), layer, pipeline_mode=once),
            _layer_spec((D_MODEL, D_MODEL), layer, pipeline_mode=once),
            vec, vec,
            _layer_spec((D_MODEL, 2 * D_FF), layer, pipeline_mode=once),
            _layer_spec((D_FF, D_MODEL), layer, pipeline_mode=once),
            vec, vec,
        ],
        out_specs=pl.BlockSpec((TM_FFN, D_MODEL), row),
        out_shape=jax.ShapeDtypeStruct((t, D_MODEL), F32),
        compiler_params=_cparams(("arbitrary",)),
        name="ffn_dec" if dec else "ffn_ctx",
    )(hg, at, x, mod, w["w_out"], prm["ln1_g"], prm["ln1_b"], w["w_gu"], w["w_dn"], prm["ln2_g"], prm["ln2_b"])


def _rope_tables(width, rot_off, rot_dim):
    t = np.arange(DEC_SEQ)
    pos = (t // GRID_W, t % GRID_W)
    half = rot_dim // 4
    freqs = ROPE_THETA ** (-np.arange(half, dtype=np.float64) / half)
    c = np.ones((DEC_SEQ, LANES), np.float64)
    sa = np.zeros((DEC_SEQ, LANES), np.float64)
    sb = np.zeros((DEC_SEQ, LANES), np.float64)
    for blk in range(LANES // width):
        for axis in range(2):
            base = blk * width + rot_off + axis * 2 * half
            ang = pos[axis][:, None].astype(np.float64) * freqs[None, :]
            c[:, base:base + half] = np.cos(ang)
            c[:, base + half:base + 2 * half] = np.cos(ang)
            sa[:, base:base + half] = -np.sin(ang)
            sb[:, base + half:base + 2 * half] = np.sin(ang)
    return tuple(jnp.asarray(a, F32) for a in (c, sa, sb))


def _constants():
    C = HG_CHUNK
    t = np.arange(C)
    tri_f = (t[None, :] <= t[:, None]).astype(np.float32)
    tri_b = (t[None, :] >= t[:, None]).astype(np.float32)
    x = t[:, None] ^ t[None, :]
    top = np.where(x > 0, np.floor(np.log2(np.maximum(x, 1))).astype(np.int32), -1)
    same = (t[:, None] // SUBLANES) == (t[None, :] // SUBLANES)
    lvl_f = np.where(same, np.where(t[:, None] >= t[None, :], 0, -1),
                     np.where(t[:, None] > t[None, :], top, -1)).astype(np.int32)
    lvl_b = np.where(same, np.where(t[:, None] <= t[None, :], 0, -1),
                     np.where(t[:, None] < t[None, :], top, -1)).astype(np.int32)
    grp = np.arange(LANES) // HG_DK
    blk = (grp[:, None] == grp[None, :]).astype(np.float32)
    stk = np.zeros((SUBLANES, LANES, 2, C), np.float32)
    for s in range(SUBLANES):
        for hb in range(2):
            stk[s, hb * HG_DK:(hb + 1) * HG_DK, hb, s::SUBLANES] = 1.0
    stk = stk.reshape(SUBLANES * LANES, 2 * C)
    place = np.zeros((MLA_ROPE, MLA_QW), np.float32)
    for hd in range(MLA_HEADS):
        place[np.arange(MLA_ROPE), hd * HEAD_PAD + MLA_NOPE + np.arange(MLA_ROPE)] = 1.0
    mla_c, mla_sa, mla_sb = _rope_tables(HEAD_PAD, MLA_NOPE, MLA_ROPE)
    gqa_c, gqa_sa, gqa_sb = _rope_tables(GQA_HD, 0, GQA_HD)
    return {
        "tri_f": jnp.asarray(tri_f, BF16), "tri_b": jnp.asarray(tri_b, BF16),
        "lvl_f": jnp.asarray(np.concatenate([lvl_f, lvl_f], axis=0)),
        "lvl_b": jnp.asarray(np.concatenate([lvl_b, lvl_b], axis=0)),
        "stk": jnp.asarray(stk, BF16), "ones": jnp.asarray(blk, BF16), "ones64": jnp.asarray(blk / GQA_HD, BF16),
        "blk": jnp.asarray(blk, F32), "kpe_place": jnp.asarray(place, BF16),
        "mla_c": mla_c, "mla_sa": mla_sa, "mla_sb": mla_sb, "gqa_c": gqa_c, "gqa_sa": gqa_sa, "gqa_sb": gqa_sb,
    }


def kernel(x_prompt, x_sample, cache_mla_ckv, cache_mla_kpe, cache_gqa_k, cache_gqa_v, state_hgrn, c, c_ctx, w_ada,
           b_ada, w_in, hg_lb, hg_norm, mla_q_norm, mla_w_uq, mla_kv_norm, mla_w_ukv, gqa_q_norm, gqa_k_norm, w_out,
           ln1_g, ln1_b, w_ffn_in, w_ffn_out, ln2_g, ln2_b):
    cst = _constants()
    prm = {"mla_q_norm": mla_q_norm, "mla_kv_norm": mla_kv_norm, "gqa_q_norm": gqa_q_norm, "gqa_k_norm": gqa_k_norm,
           "ln1_g": ln1_g, "ln1_b": ln1_b, "ln2_g": ln2_g, "ln2_b": ln2_b}
    w = {"w_in": _prep_in_call(w_in)}
    w["w_uq"], w["w_uk"], w["w_uv"] = _prep_mla_call(mla_w_uq, mla_w_ukv)
    w["w_out"], w["w_gu"], w["w_dn"] = _prep_cast_call(w_out, w_ffn_in, w_ffn_out)

    cvec = jnp.concatenate([c_ctx[None, :], c, jnp.zeros((MOD_ROWS - 1 - DEC_BATCH, D_MODEL), F32)], axis=0)
    mod = _ada_call(cvec, w_ada, b_ada)
    cache_gk = cache_gqa_k.reshape(DEC_BATCH, DEPTH, PAST_LEN, LANES)
    cache_gv = cache_gqa_v.reshape(DEC_BATCH, DEPTH, PAST_LEN, LANES)

    x_ctx = x_prompt.reshape(BATCH * SEQ, D_MODEL)
    x_dec = x_sample.reshape(DEC_BATCH * DEC_SEQ, D_MODEL)
    states, hg_state = None, None
    for l in range(DEPTH):
        zhf, zhb, qm, km, vm, gq, gk, gv, *states = _inproj_call(x_ctx, mod, w, prm, cst, l, rope=False,
                                                          prev_states=states)
        hg_out, hg_state = _hgrn_call(zhf, zhb, hg_lb, hg_norm, cst, l, BATCH, SEQ, prev_state_out=hg_state)
        at = _attn_call(qm, gq, km, vm, gk, gv, None, l)
        x_ctx = _ffn_call(hg_out, at, x_ctx, mod, w, prm, l, dec=False)

        zhf, zhb, qm, km, vm, gq, gk, gv = _inproj_call(x_dec, mod, w, prm, cst, l, rope=True)
        hg_out = _hgrn_call(zhf, zhb, hg_lb, hg_norm, cst, l, DEC_BATCH, DEC_SEQ, state_in=state_hgrn)[0]
        kc, vc = _cachekv_call(cache_mla_ckv, cache_mla_kpe, w, cst, l)
        at = _attn_call(qm, gq, km, vm, gk, gv, (kc, vc, cache_gk, cache_gv), l)
        x_dec = _ffn_call(hg_out, at, x_dec, mod, w, prm, l, dec=True)

    ckv, kpe, gk32, gv32 = states
    kv_shape = (BATCH, DEPTH, SEQ, GQA_KV_HEADS, GQA_HD)
    return (x_ctx.reshape(BATCH, SEQ, D_MODEL), x_dec.reshape(DEC_BATCH, DEC_SEQ, D_MODEL),
            ckv, kpe, gk32.reshape(kv_shape), gv32.reshape(kv_shape), hg_state)
```

```python
import functools

import numpy as np
import jax
import jax.numpy as jnp
from jax import lax
from jax.experimental import pallas as pl
from jax.experimental.pallas import tpu as pltpu

F32 = jnp.float32
BF16 = jnp.bfloat16

D_MODEL = 1024
BATCH = 32
SEQ = 256
DEPTH = 2
DEC_BATCH = 2
DEC_SEQ = 2048
PAST_LEN = 256
GRID_W = 64
ROPE_THETA = 10000.0
EPS = 1e-6
GATE_FLOOR = 1e-30
HG_HEADS = 4
HG_DK = 64
HG_DV = 64
MLA_HEADS = 6
MLA_NOPE = 64
MLA_ROPE = 32
MLA_V = 64
MLA_Q_RANK = 256
MLA_KV_RANK = 128
GQA_HEADS = 6
GQA_KV_HEADS = 2
GQA_HD = 64
HG_W = HG_HEADS * HG_DV
MLA_W = MLA_HEADS * MLA_V
GQA_W = GQA_HEADS * GQA_HD
D_FF = 2816
MLA_SCALE = (MLA_NOPE + MLA_ROPE) ** -0.5
GQA_SCALE = GQA_HD ** -0.5
ALPHA = (2 * DEPTH) ** 0.25
LOG2E = 1.4426950408889634

LANES = 128
SUBLANES = 8
HEAD_PAD = 128
MLA_QW = MLA_HEADS * HEAD_PAD
N_ADA = 6 * D_MODEL
MOD_ROWS = 8

N_IN = 5 * HG_HEADS * HG_DK + MLA_Q_RANK + MLA_KV_RANK + MLA_ROPE + GQA_W + 2 * GQA_KV_HEADS * GQA_HD
SRC_KPE = 5 * HG_HEADS * HG_DK + MLA_Q_RANK + MLA_KV_RANK
SRC_GQ = SRC_KPE + MLA_ROPE
SRC_GK = SRC_GQ + GQA_W
SRC_GV = SRC_GK + GQA_KV_HEADS * GQA_HD
HG_COLS = 5 * HG_HEADS * HG_DK
OFF_MQ = HG_COLS
OFF_MKV = OFF_MQ + MLA_Q_RANK
OFF_GQ = OFF_MKV + MLA_KV_RANK
OFF_GK = OFF_GQ + GQA_W
OFF_GV = OFF_GK + GQA_KV_HEADS * GQA_HD
OFF_KPE_PAD = OFF_GV + GQA_KV_HEADS * GQA_HD
OFF_KPE = OFF_KPE_PAD + HEAD_PAD
N_Z = OFF_KPE + MLA_ROPE

TM_IN = 512
TM_FFN = 512
TQ_DEC = 512
GQA_ROWS = 256
ATT_GROUP_CTX = 9
ATT_GROUP_DEC = 1
VT_ROWS = 80
HG_CHUNK = 128
HG_UNROLL = 2
HG_SEQS_CTX = 2
FF_CHUNKS = 2
PREP_STEPS = 4
VMEM_LIMIT = 56 * 1024 * 1024


def _cparams(sem):
    return pltpu.CompilerParams(dimension_semantics=sem, vmem_limit_bytes=VMEM_LIMIT)


def _const_spec(shape):
    nd = len(shape)
    return pl.BlockSpec(shape, lambda *_: (0,) * nd)


def _layer_spec(shape, layer, **kw):
    nd = len(shape)
    return pl.BlockSpec((None,) + tuple(shape), lambda *_: (layer,) + (0,) * nd, **kw)


_ANY = pl.BlockSpec(memory_space=pl.ANY)


def _silu(x):
    return x * jax.nn.sigmoid(x)


def _dot(a, b):
    return jnp.dot(a, b, preferred_element_type=F32)


def _dot_nt(a, b):
    return lax.dot_general(a, b, (((1,), (1,)), ((), ())), preferred_element_type=F32)


def _dot_tn(a, b):
    return lax.dot_general(a, b, (((0,), (0,)), ((), ())), preferred_element_type=F32)


def _rms_rows(x, g):
    y = x * lax.rsqrt(jnp.mean(x * x, axis=-1, keepdims=True) + EPS)
    return y * g


def _layernorm_rows(x, g, b):
    xc = x - jnp.mean(x, axis=-1, keepdims=True)
    var = jnp.mean(xc * xc, axis=-1, keepdims=True)
    return xc * lax.rsqrt(var + EPS) * g + b


def _group_mean(x, ones_blk):
    return _dot(x.astype(BF16), ones_blk)


def _twice(g):
    return jnp.concatenate([g, g], axis=1)


_GQ_HEAD_ORDER = (0, 3, 1, 4, 2, 5)


def _prep_in_kernel(w_ref, o_ref):
    x = w_ref[...]
    rows = x.shape[0]
    kr = x[:, SRC_KPE:SRC_GQ]
    gq = [x[:, SRC_GQ + h * GQA_HD:SRC_GQ + (h + 1) * GQA_HD] for h in _GQ_HEAD_ORDER]
    pieces = [x[:, 0:SRC_KPE]] + gq + [
        x[:, SRC_GK:SRC_GV], x[:, SRC_GV:N_IN],
        jnp.zeros((rows, MLA_NOPE), F32), kr, jnp.zeros((rows, HEAD_PAD - MLA_NOPE - MLA_ROPE), F32), kr]
    o_ref[...] = jnp.concatenate(pieces, axis=1).astype(BF16)


def _prep_in_call(w_in):
    rb = 128
    return pl.pallas_call(
        _prep_in_kernel,
        grid=(DEPTH, D_MODEL // rb),
        in_specs=[pl.BlockSpec((None, rb, N_IN), lambda l, j: (l, j, 0))],
        out_specs=pl.BlockSpec((None, rb, N_Z), lambda l, j: (l, j, 0)),
        out_shape=jax.ShapeDtypeStruct((DEPTH, D_MODEL, N_Z), BF16),
        compiler_params=_cparams(("arbitrary", "arbitrary")),
        name="prep_in",
    )(w_in)


def _prep_mla_kernel(uq_ref, ukv_ref, oq_ref, ok_ref, ov_ref):
    uq = uq_ref[...]
    dq = MLA_NOPE + MLA_ROPE
    zq = jnp.zeros((MLA_Q_RANK, HEAD_PAD - dq), F32)
    oq_ref[...] = jnp.concatenate(
        [p for h in range(MLA_HEADS) for p in (uq[:, h * dq:(h + 1) * dq], zq)], axis=1).astype(BF16)
    ukv = ukv_ref[...]
    lane = lax.broadcasted_iota(jnp.int32, ukv.shape, 1)
    ok_ref[...] = jnp.where((lane & (HEAD_PAD - 1)) < MLA_NOPE, ukv, 0.0).astype(BF16)
    ov_ref[...] = jnp.concatenate(
        [ukv[:, h * HEAD_PAD + MLA_NOPE:(h + 1) * HEAD_PAD] for h in range(MLA_HEADS)], axis=1).astype(BF16)


def _prep_mla_call(w_uq, w_ukv):
    lay = lambda n: pl.BlockSpec((None, n[0], n[1]), lambda l: (l, 0, 0))
    shapes = [(MLA_Q_RANK, MLA_QW), (MLA_KV_RANK, MLA_QW), (MLA_KV_RANK, MLA_W)]
    return pl.pallas_call(
        _prep_mla_kernel,
        grid=(DEPTH,),
        in_specs=[lay((MLA_Q_RANK, MLA_HEADS * (MLA_NOPE + MLA_ROPE))), lay((MLA_KV_RANK, MLA_QW))],
        out_specs=[lay(s) for s in shapes],
        out_shape=[jax.ShapeDtypeStruct((DEPTH,) + s, BF16) for s in shapes],
        compiler_params=_cparams(("arbitrary",)),
        name="prep_mla",
    )(w_uq, w_ukv)


def _prep_cast_kernel(a_ref, b_ref, c_ref, oa_ref, ob_ref, oc_ref):
    oa_ref[...] = a_ref[...].astype(BF16)
    ob_ref[...] = b_ref[...].astype(BF16)
    oc_ref[...] = c_ref[...].astype(BF16)


def _prep_cast_call(w_out, w_ffn_in, w_ffn_out):
    ws = (w_out, w_ffn_in, w_ffn_out)
    specs = [pl.BlockSpec((None, w.shape[1] // PREP_STEPS, w.shape[2]), lambda l, j: (l, j, 0)) for w in ws]
    return pl.pallas_call(
        _prep_cast_kernel,
        grid=(DEPTH, PREP_STEPS),
        in_specs=specs,
        out_specs=specs,
        out_shape=[jax.ShapeDtypeStruct(w.shape, BF16) for w in ws],
        compiler_params=_cparams(("arbitrary", "arbitrary")),
        name="prep_cast",
    )(*ws)


def _ada_kernel(c_ref, w_ref, b_ref, o_ref):
    s = _silu(c_ref[...]).astype(BF16)
    o_ref[...] = _dot(s, w_ref[...].astype(BF16)) + b_ref[pl.ds(pl.program_id(0), 1), :]


def _ada_call(cvec, w_ada, b_ada):
    tn = 1536
    return pl.pallas_call(
        _ada_kernel,
        grid=(DEPTH, N_ADA // tn),
        in_specs=[
            pl.BlockSpec((MOD_ROWS, D_MODEL), lambda l, j: (0, 0)),
            pl.BlockSpec((None, D_MODEL, tn), lambda l, j: (l, 0, j)),
            pl.BlockSpec((DEPTH, tn), lambda l, j: (0, j)),
        ],
        out_specs=pl.BlockSpec((None, MOD_ROWS, tn), lambda l, j: (l, 0, j)),
        out_shape=jax.ShapeDtypeStruct((DEPTH, MOD_ROWS, N_ADA), F32),
        compiler_params=_cparams(("arbitrary", "arbitrary")),
        name="ada_mod",
    )(cvec, w_ada, b_ada)


def _mod_row(mod_ref, dec, tiles_per_seq):
    if not dec:
        return mod_ref[0:1, :]
    return mod_ref[pl.ds(1 + pl.program_id(0) // tiles_per_seq, 1), :]


def _rope_block(x, c, sa, sb, half):
    return x * c + pltpu.roll(x, LANES - half, 1) * sa + pltpu.roll(x, half, 1) * sb


def _inproj_kernel(layer, rope, x_ref, mod_ref, w_in_ref, w_uq_ref, w_uk_ref, w_uv_ref, qn_ref, kvn_ref, gqn_ref,
                   gkn_ref, ones_ref, *rest):
    if rope:
        (mc_ref, msa_ref, msb_ref, gc_ref, gsa_ref, gsb_ref,
         zhf_ref, zhb_ref, qm_ref, km_ref, vm_ref, gq_ref, gk_ref, gv_ref) = rest
    else:
        rest = rest[-12:]
        (zhf_ref, zhb_ref, qm_ref, km_ref, vm_ref, gq_ref, gk_ref, gv_ref,
         ckv_ref, kpe_ref, gk32_ref, gv32_ref) = rest
    lrow = slice(layer, layer + 1)
    qk_base = LOG2E
    mod = _mod_row(mod_ref, rope, DEC_SEQ // TM_IN)
    sh1 = mod[:, 0:D_MODEL]
    sc1 = mod[:, D_MODEL:2 * D_MODEL]
    h = (x_ref[...] * (1.0 + sc1) + sh1).astype(BF16)
    z = _dot(h, w_in_ref[...])
    gw = HG_HEADS * HG_DK
    zhf_ref[...] = z[:, gw:3 * gw]
    zhb_ref[:, 0:gw] = z[:, 0:gw].astype(BF16)
    zhb_ref[:, gw:3 * gw] = z[:, 3 * gw:HG_COLS].astype(BF16)
    mq = z[:, OFF_MQ:OFF_MKV]
    mkv = z[:, OFF_MKV:OFF_GQ]
    gq = z[:, OFF_GQ:OFF_GK]
    gk = z[:, OFF_GK:OFF_GV]
    gv = z[:, OFF_GV:OFF_KPE_PAD]
    kpe_pad = z[:, OFF_KPE_PAD:OFF_KPE]
    ones = ones_ref[...]
    gqn = _twice(gqn_ref[lrow, :])
    gkn_g = _twice(gkn_ref[lrow, :])

    q = _dot(_rms_rows(mq, qn_ref[lrow, :]).astype(BF16), w_uq_ref[...])
    ckv = _rms_rows(mkv, kvn_ref[lrow, :])
    ckv_b = ckv.astype(BF16)
    kn = _dot(ckv_b, w_uk_ref[...])
    vm_ref[...] = _dot(ckv_b, w_uv_ref[...]).astype(BF16)
    if rope:
        mc, msa, msb = mc_ref[...], msa_ref[...], msb_ref[...]
        kpe_pad = _rope_block(kpe_pad, mc, msa, msb, MLA_ROPE // 4)
    for hd in range(MLA_HEADS):
        sl = slice(hd * HEAD_PAD, (hd + 1) * HEAD_PAD)
        qh = q[:, sl]
        if rope:
            qh = _rope_block(qh, mc, msa, msb, MLA_ROPE // 4)
        qm_ref[:, sl] = (qh * (MLA_SCALE * qk_base)).astype(BF16)
        km_ref[:, sl] = (kn[:, sl] + kpe_pad).astype(BF16)

    gkn = gk * lax.rsqrt(_group_mean(gk * gk, ones) + EPS) * gkn_g
    for p in range(GQA_W // LANES):
        sl = slice(p * LANES, (p + 1) * LANES)
        gqp = gq[:, sl]
        gqp = gqp * lax.rsqrt(_group_mean(gqp * gqp, ones) + EPS) * gqn
        if rope:
            gqp = _rope_block(gqp, gc_ref[...], gsa_ref[...], gsb_ref[...], GQA_HD // 4)
        gq_ref[:, sl] = (gqp * (GQA_SCALE * qk_base)).astype(BF16)
    if rope:
        gk_ref[...] = _rope_block(gkn, gc_ref[...], gsa_ref[...], gsb_ref[...], GQA_HD // 4).astype(BF16)
    else:
        gk_ref[...] = gkn.astype(BF16)
        kpe = z[:, OFF_KPE:N_Z]
        for b in range(TM_IN // SEQ):
            rows = slice(b * SEQ, (b + 1) * SEQ)
            ckv_ref[b] = ckv[rows]
            kpe_ref[b] = kpe[rows]
            gk32_ref[b] = gkn[rows]
            gv32_ref[b] = gv[rows]
    gv_ref[...] = gv.astype(BF16)


def _inproj_call(x, mod, w, prm, cst, layer, rope, prev_states=None):
    t = x.shape[0]
    nt = t // TM_IN
    row = lambda i: (i, 0)
    in_specs = [
        pl.BlockSpec((TM_IN, D_MODEL), row),
        _layer_spec((MOD_ROWS, N_ADA), layer),
        _layer_spec((D_MODEL, N_Z), layer),
        _layer_spec((MLA_Q_RANK, MLA_QW), layer),
        _layer_spec((MLA_KV_RANK, MLA_QW), layer),
        _layer_spec((MLA_KV_RANK, MLA_W), layer),
        _const_spec((DEPTH, MLA_Q_RANK)),
        _const_spec((DEPTH, MLA_KV_RANK)),
        _const_spec((DEPTH, GQA_HD)),
        _const_spec((DEPTH, GQA_HD)),
        _const_spec((LANES, LANES)),
    ]
    args = [x, mod, w["w_in"], w["w_uq"], w["w_uk"], w["w_uv"], prm["mla_q_norm"], prm["mla_kv_norm"],
            prm["gqa_q_norm"], prm["gqa_k_norm"], cst["ones64"]]
    outs = [(2 * HG_HEADS * HG_DK, F32), (3 * HG_HEADS * HG_DK, BF16), (MLA_QW, BF16), (MLA_QW, BF16), (MLA_W, BF16),
            (GQA_W, BF16), (LANES, BF16), (LANES, BF16)]
    out_specs = [pl.BlockSpec((TM_IN, wd), row) for wd, _ in outs]
    out_shape = [jax.ShapeDtypeStruct((t, wd), dt) for wd, dt in outs]
    aliases = {}
    if rope:
        tab = lambda i: (i % (DEC_SEQ // TM_IN), 0)
        in_specs += [pl.BlockSpec((TM_IN, LANES), tab)] * 6
        args += [cst["mla_c"], cst["mla_sa"], cst["mla_sb"], cst["gqa_c"], cst["gqa_sa"], cst["gqa_sb"]]
    else:
        per = TM_IN // SEQ
        for wd in (MLA_KV_RANK, MLA_ROPE, LANES, LANES):
            out_specs.append(pl.BlockSpec((per, None, SEQ, wd), lambda i: (i, layer, 0, 0)))
            out_shape.append(jax.ShapeDtypeStruct((BATCH, DEPTH, SEQ, wd), F32))
        if prev_states is not None:
            for k, arr in enumerate(prev_states):
                aliases[len(args)] = len(outs) + k
                in_specs.append(_ANY)
                args.append(arr)
    return pl.pallas_call(
        functools.partial(_inproj_kernel, layer, rope),
        grid=(nt,),
        in_specs=in_specs,
        out_specs=out_specs,
        out_shape=out_shape,
        input_output_aliases=aliases,
        compiler_params=_cparams(("arbitrary",)),
        name="inproj_dec" if rope else "inproj_ctx",
    )(*args)


def _cachekv_kernel(ckv_ref, kpe_ref, w_uk_ref, w_uv_ref, place_ref, k_ref, v_ref):
    ckv_b = ckv_ref[...].astype(BF16)
    k_ref[...] = (_dot(ckv_b, w_uk_ref[...]) + _dot(kpe_ref[...].astype(BF16), place_ref[...])).astype(BF16)
    v_ref[...] = _dot(ckv_b, w_uv_ref[...]).astype(BF16)


def _cachekv_call(cache_ckv, cache_kpe, w, cst, layer):
    t = DEC_BATCH * PAST_LEN
    cache = lambda wd: pl.BlockSpec((None, None, PAST_LEN, wd), lambda b: (b, layer, 0, 0))
    return pl.pallas_call(
        _cachekv_kernel,
        grid=(DEC_BATCH,),
        in_specs=[cache(MLA_KV_RANK), cache(MLA_ROPE),
                  _layer_spec((MLA_KV_RANK, MLA_QW), layer), _layer_spec((MLA_KV_RANK, MLA_W), layer),
                  _const_spec((MLA_ROPE, MLA_QW))],
        out_specs=[pl.BlockSpec((PAST_LEN, MLA_QW), lambda b: (b, 0)),
                   pl.BlockSpec((PAST_LEN, MLA_W), lambda b: (b, 0))],
        out_shape=[jax.ShapeDtypeStruct((t, MLA_QW), BF16), jax.ShapeDtypeStruct((t, MLA_W), BF16)],
        compiler_params=_cparams(("arbitrary",)),
        name="cache_kv",
    )(cache_ckv, cache_kpe, w["w_uk"], w["w_uv"], cst["kpe_place"])


def _bcast_row(x, r, rows):
    return jnp.broadcast_to(x[r:r + 1, :], (rows, x.shape[1]))


def _cumsum_rows(tri, x):
    hi = x.astype(BF16)
    r1 = x - hi.astype(F32)
    mid = r1.astype(BF16)
    lo = (r1 - mid.astype(F32)).astype(BF16)
    return _dot(tri, hi) + _dot(tri, mid) + _dot(tri, lo)


def _hgrn_chunks(items, states, tri, lvl, stk, blk, lane_lo):
    C = HG_CHUNK
    zero_b = jnp.zeros((SUBLANES, LANES), F32)
    trow = lax.broadcasted_iota(jnp.int32, (SUBLANES, LANES), 0)
    for it in items:
        it["c"] = _cumsum_rows(tri[it["fwd"]], it["lf"])
    for it in items:
        it["q_a"] = jnp.where(lane_lo, it["q"], 0.0)
        it["q_b"] = jnp.where(lane_lo, 0.0, it["q"])
        it["ck"] = it["c"] - jnp.log(it["kk"])

    m = C // 2
    bit = m.bit_length() - 1
    while m >= SUBLANES:
        for it in items:
            fwd, c = it["fwd"], it["c"]
            qa_rows, qb_rows, k_rows = [], [], []
            for b in range(C // m):
                rows = slice(b * m, (b + 1) * m)
                anchor = (b // 2) * 2 * m + (m - 1 if fwd else m)
                r = _bcast_row(c, anchor, m)
                if (b % 2 == 1) == fwd:
                    e = jnp.exp(c[rows] - r)
                    qa_rows.append(it["q_a"][rows] * e)
                    qb_rows.append(it["q_b"][rows] * e)
                    k_rows += [zero_b] * (m // SUBLANES)
                else:
                    e = jnp.exp(r - c[rows])
                    qa_rows += [zero_b] * (m // SUBLANES)
                    qb_rows += [zero_b] * (m // SUBLANES)
                    k_rows.append(it["kk"][rows] * e)
            prod = _dot_nt(jnp.concatenate(qa_rows + qb_rows, axis=0).astype(BF16),
                           jnp.concatenate(k_rows, axis=0).astype(BF16))
            it["a"] = prod if "a" not in it else jnp.where(lvl[fwd] == bit, prod, it["a"])
        m //= 2
        bit -= 1

    pk = 2 * SUBLANES
    trow = lax.broadcasted_iota(jnp.int32, (pk, LANES), 0) & (SUBLANES - 1)
    zero_w = jnp.zeros((pk, LANES), BF16)
    for it in items:
        it["w"] = []
        it["q_bf"] = it["q"].astype(BF16)
    for j in range(C // pk):
        sl = slice(j * pk, (j + 1) * pk)
        for it in items:
            cs, cks, qs = it["c"][sl], it["ck"][sl], it["q_bf"][sl]
            w_s = []
            for s in range(SUBLANES):
                ref = jnp.concatenate([_bcast_row(cks, s, SUBLANES), _bcast_row(cks, SUBLANES + s, SUBLANES)], axis=0)
                e = jnp.exp(cs - ref).astype(BF16)
                keep = (trow >= s) if it["fwd"] else (trow <= s)
                w_s.append(jnp.where(keep, qs * e, zero_w))
            it["w"].append(jnp.concatenate(w_s, axis=1))
    for it in items:
        red = _dot(jnp.concatenate(it["w"], axis=0), stk)
        it["a"] = jnp.where(lvl[it["fwd"]] == 0,
                            jnp.concatenate([red[:, 0:LANES], red[:, LANES:2 * LANES]], axis=0), it["a"])
    for it in items:
        off = _dot(it["a"].astype(BF16), it["iv"].astype(BF16))
        it["o"] = jnp.where(lane_lo, off[0:C], off[C:2 * C])
        it["edge"] = _bcast_row(it["c"], C - 1 if it["fwd"] else 0, C)
        it["q_dec"] = (it["q"] * jnp.exp(it["c"])).astype(BF16)
        k_dec = (it["kk"] * jnp.exp(it["edge"] - it["c"])).astype(BF16)
        it["upd"] = _dot_tn(it["iv"].astype(BF16), k_dec) * blk

    states = list(states)
    outs = []
    for it in items:
        st = states[it["sid"]]
        outs.append(it["o"] + _dot_nt(it["q_dec"], st.astype(BF16)))
        states[it["sid"]] = st * jnp.exp(it["edge"][0:1]) + it["upd"]
    return outs, states


def _hgrn_kernel(layer, nchunks, nseq, has_state, zq_ref, zff_ref, zfb_ref, zi_ref, zg_ref, lb_ref, g_ref,
                 trif_ref, trib_ref, lvlf_ref, lvlb_ref, stk_ref, ones_ref, blk_ref, *rest):
    if has_state:
        s0_ref, o_ref, of_scr, ob_scr = rest
    else:
        o_ref, st_ref, of_scr, ob_scr = rest[-4:]
    C = HG_CHUNK
    npair = HG_HEADS // 2
    lane_lo = lax.broadcasted_iota(jnp.int32, (1, LANES), 1) < HG_DK
    pair_lanes = [slice(hp * LANES, (hp + 1) * LANES) for hp in range(npair)]

    lb_all = [lb_ref[j] for j in range(DEPTH)]
    mx = functools.reduce(jnp.maximum, lb_all)
    ex = [jnp.exp(v - mx) for v in lb_all]
    den = functools.reduce(lambda a, b: a + b, ex)
    sm = [v / den for v in ex]
    lbs = functools.reduce(lambda a, b: a + b, sm[:layer + 1]) - sm[0]

    def gates(zf, lbd):
        f = lbd + (1.0 - lbd) * jax.nn.sigmoid(zf)
        return jnp.log(jnp.maximum(f, GATE_FLOOR)), jnp.maximum(1.0 - f, 0.0)

    def block_diag_t(sa, sb):
        za = jnp.zeros((HG_DK, HG_DV), F32)
        top = jnp.concatenate([sa, za], axis=1)
        bot = jnp.concatenate([za, sb], axis=1)
        return jnp.concatenate([top, bot], axis=0).T

    if has_state:
        st0 = tuple(block_diag_t(s0_ref[d, 2 * hp], s0_ref[d, 2 * hp + 1]) for hp in range(npair) for d in range(2))
    else:
        st0 = (jnp.zeros((LANES, LANES), F32),) * (2 * npair * nseq)

    def body(j, carry):
        tri = {True: trif_ref[...], False: trib_ref[...]}
        lvl = {True: lvlf_ref[...], False: lvlb_ref[...]}
        items = []
        for u in range(HG_UNROLL):
            jf = j * HG_UNROLL + u
            for sq in range(nseq):
                for hp, ln in enumerate(pair_lanes):
                    for fwd in (True, False):
                        chunk = sq * nchunks + (jf if fwd else nchunks - 1 - jf)
                        rows = pl.ds(pl.multiple_of(chunk * C, C), C)
                        d = 0 if fwd else 1
                        lf, kk = gates((zff_ref if fwd else zfb_ref)[rows, ln], lbs[d:d + 1, ln])
                        items.append({"fwd": fwd, "sid": (sq * npair + hp) * 2 + d, "rows": rows, "ln": ln,
                                      "lf": lf, "kk": kk, "iv": zi_ref[rows, ln],
                                      "q": _silu(zq_ref[rows, ln].astype(F32)) * (HG_DK ** -0.5)})
        outs, states = _hgrn_chunks(items, carry, tri, lvl, stk_ref[...], blk_ref[...], lane_lo)
        for it, o in zip(items, outs):
            (of_scr if it["fwd"] else ob_scr)[it["rows"], it["ln"]] = o
        return tuple(states)

    states = lax.fori_loop(0, nchunks // HG_UNROLL, body, st0)

    g = _twice(g_ref[layer:layer + 1, :])
    for ln in pair_lanes:
        o = of_scr[:, ln] + ob_scr[:, ln]
        ms = _group_mean(o * o, ones_ref[...] * (1.0 / HG_DV))
        o_ref[:, ln] = (o * lax.rsqrt(ms + EPS) * g * _silu(zg_ref[:, ln].astype(F32))).astype(BF16)
    if not has_state:
        for sq in range(nseq):
            for hp in range(npair):
                for d in range(2):
                    s_kv = states[(sq * npair + hp) * 2 + d].T
                    st_ref[sq, d, 2 * hp] = s_kv[0:HG_DK, 0:HG_DV]
                    st_ref[sq, d, 2 * hp + 1] = s_kv[HG_DK:2 * HG_DK, HG_DV:2 * HG_DV]


def _hgrn_call(zhf, zhb, lb, g, cst, layer, nb, s, state_in=None, prev_state_out=None):
    has_state = state_in is not None
    nchunks = s // HG_CHUNK
    nseq = 1 if has_state else HG_SEQS_CTX
    part = lambda k: pl.BlockSpec((nseq * s, HG_W), lambda b: (b, k))
    if has_state:
        st_spec = pl.BlockSpec((None, None, 2, HG_HEADS, HG_DK, HG_DV), lambda b: (b, layer, 0, 0, 0, 0))
    else:
        st_spec = pl.BlockSpec((nseq, None, 2, HG_HEADS, HG_DK, HG_DV), lambda b: (b, layer, 0, 0, 0, 0))
    in_specs = [part(0), part(0), part(1), part(1), part(2),
                _const_spec((DEPTH, 2, HG_W)),
                _const_spec((DEPTH, HG_DV)),
                _const_spec((HG_CHUNK, HG_CHUNK)), _const_spec((HG_CHUNK, HG_CHUNK)),
                _const_spec((2 * HG_CHUNK, HG_CHUNK)), _const_spec((2 * HG_CHUNK, HG_CHUNK)),
                _const_spec((SUBLANES * LANES, 2 * LANES)),
                _const_spec((LANES, LANES)), _const_spec((LANES, LANES))]
    args = [zhb, zhf, zhf, zhb, zhb, lb, g, cst["tri_f"], cst["tri_b"], cst["lvl_f"], cst["lvl_b"], cst["stk"],
            cst["ones"], cst["blk"]]
    out_specs = [pl.BlockSpec((nseq * s, HG_W), lambda b: (b, 0))]
    out_shape = [jax.ShapeDtypeStruct((nb * s, HG_W), BF16)]
    aliases = {}
    if has_state:
        in_specs.append(st_spec)
        args.append(state_in)
    else:
        out_specs.append(st_spec)
        out_shape.append(jax.ShapeDtypeStruct((nb, DEPTH, 2, HG_HEADS, HG_DK, HG_DV), F32))
        if prev_state_out is not None:
            aliases[len(args)] = 1
            in_specs.append(_ANY)
            args.append(prev_state_out)
    return pl.pallas_call(
        functools.partial(_hgrn_kernel, layer, nchunks, nseq, has_state),
        grid=(nb // nseq,),
        in_specs=in_specs,
        out_specs=out_specs,
        out_shape=out_shape,
        input_output_aliases=aliases,
        scratch_shapes=[pltpu.VMEM((nseq * s, HG_W), F32), pltpu.VMEM((nseq * s, HG_W), F32)],
        compiler_params=_cparams(("arbitrary",)),
        name="hgrn_dec" if has_state else "hgrn_ctx",
    )(*args)


def _attend_t(jobs, ss):
    add = lambda a, b: a + b
    mxs = [functools.reduce(jnp.maximum, [jnp.max(s, axis=-1, keepdims=True) for s in sj]) for sj in ss]
    ebs = [[jnp.exp2(s - mx).astype(BF16) for s in sj] for sj, mx in zip(ss, mxs)]
    prods = [[functools.reduce(add, [_dot_nt(vt, e[rows]) for vt, e in zip(vts, ej)]) for rows, vts in parts]
             for ej, (_, _, parts) in zip(ebs, jobs)]
    return [[pr[0:MLA_V] * (1.0 / pr[MLA_V:MLA_V + 1]) for pr in pj] for pj in prods]


def _scores(jobs):
    return [[_dot_nt(q, k) for k in ks] for q, ks, _ in jobs]


def _attn_kernel(has_cache, qm_ref, gq_ref, km_ref, vm_ref, gk_ref, gv_ref, *rest):
    if has_cache:
        kc_ref, vc_ref, gkc_ref, gvc_ref, o_ref, vt_scr = rest
    else:
        o_ref, vt_scr = rest
    sq = qm_ref.shape[0]
    sk = vt_scr.shape[1]
    nblk = MLA_HEADS + GQA_KV_HEADS
    cols = [slice(0, PAST_LEN), slice(PAST_LEN, sk)] if has_cache else [slice(0, sk)]

    def fill():
        tr = lambda v: v.astype(F32).T.astype(BF16)
        srcs = [(vc_ref, gvc_ref), (vm_ref, gv_ref)] if has_cache else [(vm_ref, gv_ref)]
        for cl, (mla_ref, gqa_ref) in zip(cols, srcs):
            mla_t, gqa_t = tr(mla_ref[...]), tr(gqa_ref[...])
            tail = (lax.broadcasted_iota(jnp.int32, (VT_ROWS - MLA_V, cl.stop - cl.start), 0) == 0).astype(BF16)
            for b in range(nblk):
                src = mla_t[b * MLA_V:(b + 1) * MLA_V] if b < MLA_HEADS else \
                    gqa_t[(b - MLA_HEADS) * GQA_HD:(b - MLA_HEADS + 1) * GQA_HD]
                vt_scr[b * VT_ROWS:b * VT_ROWS + MLA_V, cl] = src
                vt_scr[b * VT_ROWS + MLA_V:(b + 1) * VT_ROWS, cl] = tail

    if has_cache:
        pl.when(pl.program_id(1) == 0)(fill)
    else:
        fill()

    def vt(b):
        return [vt_scr[b * VT_ROWS:(b + 1) * VT_ROWS, cl] for cl in cols]

    lo = lax.broadcasted_iota(jnp.int32, (GQA_ROWS, LANES), 1) < GQA_HD
    zero = jnp.zeros((GQA_ROWS, LANES), BF16)
    gks = [gkc_ref[...].astype(BF16), gk_ref[...]] if has_cache else [gk_ref[...]]
    jobs = []
    for hd in range(MLA_HEADS):
        sl = slice(hd * HEAD_PAD, (hd + 1) * HEAD_PAD)
        ks = [kc_ref[:, sl], km_ref[:, sl]] if has_cache else [km_ref[:, sl]]
        jobs.append((qm_ref[:, sl], ks, [(slice(0, sq), vt(hd))]))
    halves = [(slice(0, GQA_ROWS), vt(MLA_HEADS)), (slice(GQA_ROWS, 2 * GQA_ROWS), vt(MLA_HEADS + 1))]
    for r0 in range(0, sq, GQA_ROWS):
        for p in range(GQA_W // LANES):
            qp = gq_ref[r0:r0 + GQA_ROWS, p * LANES:(p + 1) * LANES]
            jobs.append((jnp.concatenate([jnp.where(lo, qp, zero), jnp.where(lo, zero, qp)], axis=0), gks, halves))
    group = ATT_GROUP_DEC if has_cache else ATT_GROUP_CTX
    groups = [jobs[j0:j0 + group] for j0 in range(0, len(jobs), group)]
    outs = []
    ss_next = _scores(groups[0])
    for g, grp in enumerate(groups):
        ss = ss_next
        if g + 1 < len(groups):
            ss_next = _scores(groups[g + 1])
        outs += _attend_t(grp, ss)
    for p in range(MLA_HEADS // 2):
        pair = jnp.concatenate([outs[2 * p][0], outs[2 * p + 1][0]], axis=0)
        o_ref[:, p * LANES:(p + 1) * LANES] = pair.T.astype(BF16)
    for i, r0 in enumerate(range(0, sq, GQA_ROWS)):
        gres = [jnp.concatenate(outs[MLA_HEADS + 3 * i + p], axis=0).T for p in range(GQA_W // LANES)]
        swapped = pltpu.roll(gres[1], GQA_HD, 1)
        blocks = (jnp.where(lo, gres[0], swapped), jnp.where(lo, gres[2], gres[0]), jnp.where(lo, swapped, gres[2]))
        for p, blkv in enumerate(blocks):
            o_ref[r0:r0 + GQA_ROWS, MLA_W + p * LANES:MLA_W + (p + 1) * LANES] = blkv.astype(BF16)


def _attn_call(qm, gq, km, vm, gk, gv, cache, layer):
    has_cache = cache is not None
    t = qm.shape[0]
    widths = (MLA_QW, GQA_W, MLA_QW, MLA_W, LANES, LANES)
    if has_cache:
        nq = DEC_SEQ // TQ_DEC
        grid = (DEC_BATCH, nq)
        qmap = lambda b, i: (b * nq + i, 0)
        kmap = lambda b, i: (b, 0)
        once = pl.Buffered(1)
        in_specs = [pl.BlockSpec((TQ_DEC, widths[0]), qmap), pl.BlockSpec((TQ_DEC, widths[1]), qmap)]
        in_specs += [pl.BlockSpec((DEC_SEQ, wd), kmap, pipeline_mode=once) for wd in widths[2:]]
        in_specs += [pl.BlockSpec((PAST_LEN, wd), kmap, pipeline_mode=once) for wd in widths[2:4]]
        in_specs += [pl.BlockSpec((None, None, PAST_LEN, LANES), lambda b, i: (b, layer, 0, 0),
                                  pipeline_mode=once)] * 2
        args = [qm, gq, km, vm, gk, gv, *cache]
        out_spec = pl.BlockSpec((TQ_DEC, MLA_W + GQA_W), qmap)
        sk = PAST_LEN + DEC_SEQ
        sem = ("arbitrary", "arbitrary")
    else:
        grid = (t // SEQ,)
        bmap = lambda b: (b, 0)
        in_specs = [pl.BlockSpec((SEQ, wd), bmap) for wd in widths]
        args = [qm, gq, km, vm, gk, gv]
        out_spec = pl.BlockSpec((SEQ, MLA_W + GQA_W), bmap)
        sk = SEQ
        sem = ("arbitrary",)
    return pl.pallas_call(
        functools.partial(_attn_kernel, has_cache),
        grid=grid,
        in_specs=in_specs,
        out_specs=out_spec,
        out_shape=jax.ShapeDtypeStruct((t, MLA_W + GQA_W), BF16),
        scratch_shapes=[pltpu.VMEM(((MLA_HEADS + GQA_KV_HEADS) * VT_ROWS, sk), BF16)],
        compiler_params=_cparams(sem),
        name="attn_dec" if has_cache else "attn_ctx",
    )(*args)


def _ffn_kernel(layer, dec, hg_ref, at_ref, x_ref, mod_ref, w_out_ref, ln1g_ref, ln1b_ref, w_gu_ref, w_dn_ref,
                ln2g_ref, ln2b_ref, o_ref):
    lrow = slice(layer, layer + 1)
    mod = _mod_row(mod_ref, dec, DEC_SEQ // TM_FFN)
    g1 = mod[:, 2 * D_MODEL:3 * D_MODEL]
    sh2 = mod[:, 3 * D_MODEL:4 * D_MODEL]
    sc2 = mod[:, 4 * D_MODEL:5 * D_MODEL]
    g2 = mod[:, 5 * D_MODEL:6 * D_MODEL]
    y = _dot(hg_ref[...], w_out_ref[0:HG_W, :]) + _dot(at_ref[...], w_out_ref[HG_W:D_MODEL, :])
    x1 = _layernorm_rows(ALPHA * x_ref[...] + g1 * y, ln1g_ref[lrow, :], ln1b_ref[lrow, :])
    hf = (x1 * (1.0 + sc2) + sh2).astype(BF16)
    ch = D_FF // FF_CHUNKS
    f = None
    for c in range(FF_CHUNKS):
        gate = _dot(hf, w_gu_ref[:, c * ch:(c + 1) * ch])
        up = _dot(hf, w_gu_ref[:, D_FF + c * ch:D_FF + (c + 1) * ch])
        part = _dot((_silu(gate) * up).astype(BF16), w_dn_ref[c * ch:(c + 1) * ch, :])
        f = part if f is None else f + part
    o_ref[...] = _layernorm_rows(ALPHA * x1 + g2 * f, ln2g_ref[lrow, :], ln2b_ref[lrow, :])


def _ffn_call(hg, at, x, mod, w, prm, layer, dec):
    t = x.shape[0]
    row = lambda i: (i, 0)
    once = pl.Buffered(1)
    vec = pl.BlockSpec((DEPTH, D_MODEL), lambda i: (0, 0), pipeline_mode=once)
    return pl.pallas_call(
        functools.partial(_ffn_kernel, layer, dec),
        grid=(t // TM_FFN,),
        in_specs=[
            pl.BlockSpec((TM_FFN, HG_W), row),
            pl.BlockSpec((TM_FFN, MLA_W + GQA_W), row),
            pl.BlockSpec((TM_FFN, D_MODEL), row),
            _layer_spec((MOD_ROWS, N_ADA), layer, pipeline_mode=once),
            _layer_spec((D_MODEL, D_MODEL), layer, pipeline_mode=once),
            vec, vec,
            _layer_spec((D_MODEL, 2 * D_FF), layer, pipeline_mode=once),
            _layer_spec((D_FF, D_MODEL), layer, pipeline_mode=once),
            vec, vec,
        ],
        out_specs=pl.BlockSpec((TM_FFN, D_MODEL), row),
        out_shape=jax.ShapeDtypeStruct((t, D_MODEL), F32),
        compiler_params=_cparams(("arbitrary",)),
        name="ffn_dec" if dec else "ffn_ctx",
    )(hg, at, x, mod, w["w_out"], prm["ln1_g"], prm["ln1_b"], w["w_gu"], w["w_dn"], prm["ln2_g"], prm["ln2_b"])


def _rope_tables(width, rot_off, rot_dim):
    t = np.arange(DEC_SEQ)
    pos = (t // GRID_W, t % GRID_W)
    half = rot_dim // 4
    freqs = ROPE_THETA ** (-np.arange(half, dtype=np.float64) / half)
    c = np.ones((DEC_SEQ, LANES), np.float64)
    sa = np.zeros((DEC_SEQ, LANES), np.float64)
    sb = np.zeros((DEC_SEQ, LANES), np.float64)
    for blk in range(LANES // width):
        for axis in range(2):
            base = blk * width + rot_off + axis * 2 * half
            ang = pos[axis][:, None].astype(np.float64) * freqs[None, :]
            c[:, base:base + half] = np.cos(ang)
            c[:, base + half:base + 2 * half] = np.cos(ang)
            sa[:, base:base + half] = -np.sin(ang)
            sb[:, base + half:base + 2 * half] = np.sin(ang)
    return tuple(jnp.asarray(a, F32) for a in (c, sa, sb))


def _constants():
    C = HG_CHUNK
    t = np.arange(C)
    tri_f = (t[None, :] <= t[:, None]).astype(np.float32)
    tri_b = (t[None, :] >= t[:, None]).astype(np.float32)
    x = t[:, None] ^ t[None, :]
    top = np.where(x > 0, np.floor(np.log2(np.maximum(x, 1))).astype(np.int32), -1)
    same = (t[:, None] // SUBLANES) == (t[None, :] // SUBLANES)
    lvl_f = np.where(same, np.where(t[:, None] >= t[None, :], 0, -1),
                     np.where(t[:, None] > t[None, :], top, -1)).astype(np.int32)
    lvl_b = np.where(same, np.where(t[:, None] <= t[None, :], 0, -1),
                     np.where(t[:, None] < t[None, :], top, -1)).astype(np.int32)
    grp = np.arange(LANES) // HG_DK
    blk = (grp[:, None] == grp[None, :]).astype(np.float32)
    stk = np.zeros((SUBLANES, LANES, 2, C), np.float32)
    for s in range(SUBLANES):
        for hb in range(2):
            stk[s, hb * HG_DK:(hb + 1) * HG_DK, hb, s::SUBLANES] = 1.0
    stk = stk.reshape(SUBLANES * LANES, 2 * C)
    place = np.zeros((MLA_ROPE, MLA_QW), np.float32)
    for hd in range(MLA_HEADS):
        place[np.arange(MLA_ROPE), hd * HEAD_PAD + MLA_NOPE + np.arange(MLA_ROPE)] = 1.0
    mla_c, mla_sa, mla_sb = _rope_tables(HEAD_PAD, MLA_NOPE, MLA_ROPE)
    gqa_c, gqa_sa, gqa_sb = _rope_tables(GQA_HD, 0, GQA_HD)
    return {
        "tri_f": jnp.asarray(tri_f, BF16), "tri_b": jnp.asarray(tri_b, BF16),
        "lvl_f": jnp.asarray(np.concatenate([lvl_f, lvl_f], axis=0)),
        "lvl_b": jnp.asarray(np.concatenate([lvl_b, lvl_b], axis=0)),
        "stk": jnp.asarray(stk, BF16), "ones": jnp.asarray(blk, BF16), "ones64": jnp.asarray(blk / GQA_HD, BF16),
        "blk": jnp.asarray(blk, F32), "kpe_place": jnp.asarray(place, BF16),
        "mla_c": mla_c, "mla_sa": mla_sa, "mla_sb": mla_sb, "gqa_c": gqa_c, "gqa_sa": gqa_sa, "gqa_sb": gqa_sb,
    }


def kernel(x_prompt, x_sample, cache_mla_ckv, cache_mla_kpe, cache_gqa_k, cache_gqa_v, state_hgrn, c, c_ctx, w_ada,
           b_ada, w_in, hg_lb, hg_norm, mla_q_norm, mla_w_uq, mla_kv_norm, mla_w_ukv, gqa_q_norm, gqa_k_norm, w_out,
           ln1_g, ln1_b, w_ffn_in, w_ffn_out, ln2_g, ln2_b):
    cst = _constants()
    prm = {"mla_q_norm": mla_q_norm, "mla_kv_norm": mla_kv_norm, "gqa_q_norm": gqa_q_norm, "gqa_k_norm": gqa_k_norm,
           "ln1_g": ln1_g, "ln1_b": ln1_b, "ln2_g": ln2_g, "ln2_b": ln2_b}
    w = {"w_in": _prep_in_call(w_in)}
    w["w_uq"], w["w_uk"], w["w_uv"] = _prep_mla_call(mla_w_uq, mla_w_ukv)
    w["w_out"], w["w_gu"], w["w_dn"] = _prep_cast_call(w_out, w_ffn_in, w_ffn_out)

    cvec = jnp.concatenate([c_ctx[None, :], c, jnp.zeros((MOD_ROWS - 1 - DEC_BATCH, D_MODEL), F32)], axis=0)
    mod = _ada_call(cvec, w_ada, b_ada)
    cache_gk = cache_gqa_k.reshape(DEC_BATCH, DEPTH, PAST_LEN, LANES)
    cache_gv = cache_gqa_v.reshape(DEC_BATCH, DEPTH, PAST_LEN, LANES)

    x_ctx = x_prompt.reshape(BATCH * SEQ, D_MODEL)
    x_dec = x_sample.reshape(DEC_BATCH * DEC_SEQ, D_MODEL)
    states, hg_state = None, None
    for l in range(DEPTH):
        zhf, zhb, qm, km, vm, gq, gk, gv, *states = _inproj_call(x_ctx, mod, w, prm, cst, l, rope=False,
                                                          prev_states=states)
        hg_out, hg_state = _hgrn_call(zhf, zhb, hg_lb, hg_norm, cst, l, BATCH, SEQ, prev_state_out=hg_state)
        at = _attn_call(qm, gq, km, vm, gk, gv, None, l)
        x_ctx = _ffn_call(hg_out, at, x_ctx, mod, w, prm, l, dec=False)

        zhf, zhb, qm, km, vm, gq, gk, gv = _inproj_call(x_dec, mod, w, prm, cst, l, rope=True)
        hg_out = _hgrn_call(zhf, zhb, hg_lb, hg_norm, cst, l, DEC_BATCH, DEC_SEQ, state_in=state_hgrn)[0]
        kc, vc = _cachekv_call(cache_mla_ckv, cache_mla_kpe, w, cst, l)
        at = _attn_call(qm, gq, km, vm, gk, gv, (kc, vc, cache_gk, cache_gv), l)
        x_dec = _ffn_call(hg_out, at, x_dec, mod, w, prm, l, dec=True)

    ckv, kpe, gk32, gv32 = states
    kv_shape = (BATCH, DEPTH, SEQ, GQA_KV_HEADS, GQA_HD)
    return (x_ctx.reshape(BATCH, SEQ, D_MODEL), x_dec.reshape(DEC_BATCH, DEC_SEQ, D_MODEL),
            ckv, kpe, gk32.reshape(kv_shape), gv32.reshape(kv_shape), hg_state)
```

```python
import functools

import numpy as np
import jax
import jax.numpy as jnp
from jax import lax
from jax.experimental import pallas as pl
from jax.experimental.pallas import tpu as pltpu

F32 = jnp.float32
BF16 = jnp.bfloat16

D_MODEL = 1024
BATCH = 32
SEQ = 256
DEPTH = 2
DEC_BATCH = 2
DEC_SEQ = 2048
PAST_LEN = 256
GRID_W = 64
ROPE_THETA = 10000.0
EPS = 1e-6
GATE_FLOOR = 1e-30
HG_HEADS = 4
HG_DK = 64
HG_DV = 64
MLA_HEADS = 6
MLA_NOPE = 64
MLA_ROPE = 32
MLA_V = 64
MLA_Q_RANK = 256
MLA_KV_RANK = 128
GQA_HEADS = 6
GQA_KV_HEADS = 2
GQA_HD = 64
HG_W = HG_HEADS * HG_DV
MLA_W = MLA_HEADS * MLA_V
GQA_W = GQA_HEADS * GQA_HD
D_FF = 2816
MLA_SCALE = (MLA_NOPE + MLA_ROPE) ** -0.5
GQA_SCALE = GQA_HD ** -0.5
ALPHA = (2 * DEPTH) ** 0.25
LOG2E = 1.4426950408889634

LANES = 128
SUBLANES = 8
HEAD_PAD = 128
MLA_QW = MLA_HEADS * HEAD_PAD
N_ADA = 6 * D_MODEL
MOD_ROWS = 8

N_IN = 5 * HG_HEADS * HG_DK + MLA_Q_RANK + MLA_KV_RANK + MLA_ROPE + GQA_W + 2 * GQA_KV_HEADS * GQA_HD
SRC_KPE = 5 * HG_HEADS * HG_DK + MLA_Q_RANK + MLA_KV_RANK
SRC_GQ = SRC_KPE + MLA_ROPE
SRC_GK = SRC_GQ + GQA_W
SRC_GV = SRC_GK + GQA_KV_HEADS * GQA_HD
HG_COLS = 5 * HG_HEADS * HG_DK
OFF_MQ = HG_COLS
OFF_MKV = OFF_MQ + MLA_Q_RANK
OFF_GQ = OFF_MKV + MLA_KV_RANK
OFF_GK = OFF_GQ + GQA_W
OFF_GV = OFF_GK + GQA_KV_HEADS * GQA_HD
OFF_KPE_PAD = OFF_GV + GQA_KV_HEADS * GQA_HD
OFF_KPE = OFF_KPE_PAD + HEAD_PAD
N_Z = OFF_KPE + MLA_ROPE

TM_IN = 512
TM_FFN = 512
TQ_DEC = 512
GQA_ROWS = 256
ATT_GROUP_CTX = 9
ATT_SEQS_CTX = 2
ATT_GROUP_DEC = 1
VT_ROWS = 80
HG_CHUNK = 128
HG_UNROLL = 2
HG_SEQS_CTX = 4
HG_SEQS_DEC = 1
FF_CHUNKS = 2
PREP_STEPS = 4
VMEM_LIMIT = 56 * 1024 * 1024


def _cparams(sem):
    return pltpu.CompilerParams(dimension_semantics=sem, vmem_limit_bytes=VMEM_LIMIT)


def _const_spec(shape):
    nd = len(shape)
    return pl.BlockSpec(shape, lambda *_: (0,) * nd)


def _layer_spec(shape, layer, **kw):
    nd = len(shape)
    return pl.BlockSpec((None,) + tuple(shape), lambda *_: (layer,) + (0,) * nd, **kw)


_ANY = pl.BlockSpec(memory_space=pl.ANY)


def _silu(x):
    return x * jax.nn.sigmoid(x)


def _dot(a, b):
    return jnp.dot(a, b, preferred_element_type=F32)


def _dot_nt(a, b):
    return lax.dot_general(a, b, (((1,), (1,)), ((), ())), preferred_element_type=F32)


def _dot_tn(a, b):
    return lax.dot_general(a, b, (((0,), (0,)), ((), ())), preferred_element_type=F32)


def _rms_rows(x, g):
    y = x * lax.rsqrt(jnp.mean(x * x, axis=-1, keepdims=True) + EPS)
    return y * g


def _layernorm_rows(x, g, b):
    xc = x - jnp.mean(x, axis=-1, keepdims=True)
    var = jnp.mean(xc * xc, axis=-1, keepdims=True)
    return xc * lax.rsqrt(var + EPS) * g + b


def _group_mean(x, ones_blk):
    return _dot(x.astype(BF16), ones_blk)


def _twice(g):
    return jnp.concatenate([g, g], axis=1)


_GQ_HEAD_ORDER = (0, 3, 1, 4, 2, 5)


def _prep_in_kernel(w_ref, o_ref):
    x = w_ref[...]
    rows = x.shape[0]
    kr = x[:, SRC_KPE:SRC_GQ]
    gq = [x[:, SRC_GQ + h * GQA_HD:SRC_GQ + (h + 1) * GQA_HD] for h in _GQ_HEAD_ORDER]
    pieces = [x[:, 0:SRC_KPE]] + gq + [
        x[:, SRC_GK:SRC_GV], x[:, SRC_GV:N_IN],
        jnp.zeros((rows, MLA_NOPE), F32), kr, jnp.zeros((rows, HEAD_PAD - MLA_NOPE - MLA_ROPE), F32), kr]
    o_ref[...] = jnp.concatenate(pieces, axis=1).astype(BF16)


def _prep_in_call(w_in):
    rb = 128
    return pl.pallas_call(
        _prep_in_kernel,
        grid=(DEPTH, D_MODEL // rb),
        in_specs=[pl.BlockSpec((None, rb, N_IN), lambda l, j: (l, j, 0))],
        out_specs=pl.BlockSpec((None, rb, N_Z), lambda l, j: (l, j, 0)),
        out_shape=jax.ShapeDtypeStruct((DEPTH, D_MODEL, N_Z), BF16),
        compiler_params=_cparams(("arbitrary", "arbitrary")),
        name="prep_in",
    )(w_in)


def _prep_mla_kernel(uq_ref, ukv_ref, oq_ref, ok_ref, ov_ref):
    uq = uq_ref[...]
    dq = MLA_NOPE + MLA_ROPE
    zq = jnp.zeros((MLA_Q_RANK, HEAD_PAD - dq), F32)
    oq_ref[...] = jnp.concatenate(
        [p for h in range(MLA_HEADS) for p in (uq[:, h * dq:(h + 1) * dq], zq)], axis=1).astype(BF16)
    ukv = ukv_ref[...]
    lane = lax.broadcasted_iota(jnp.int32, ukv.shape, 1)
    ok_ref[...] = jnp.where((lane & (HEAD_PAD - 1)) < MLA_NOPE, ukv, 0.0).astype(BF16)
    ov_ref[...] = jnp.concatenate(
        [ukv[:, h * HEAD_PAD + MLA_NOPE:(h + 1) * HEAD_PAD] for h in range(MLA_HEADS)], axis=1).astype(BF16)


def _prep_mla_call(w_uq, w_ukv):
    lay = lambda n: pl.BlockSpec((None, n[0], n[1]), lambda l: (l, 0, 0))
    shapes = [(MLA_Q_RANK, MLA_QW), (MLA_KV_RANK, MLA_QW), (MLA_KV_RANK, MLA_W)]
    return pl.pallas_call(
        _prep_mla_kernel,
        grid=(DEPTH,),
        in_specs=[lay((MLA_Q_RANK, MLA_HEADS * (MLA_NOPE + MLA_ROPE))), lay((MLA_KV_RANK, MLA_QW))],
        out_specs=[lay(s) for s in shapes],
        out_shape=[jax.ShapeDtypeStruct((DEPTH,) + s, BF16) for s in shapes],
        compiler_params=_cparams(("arbitrary",)),
        name="prep_mla",
    )(w_uq, w_ukv)


def _prep_cast_kernel(a_ref, b_ref, c_ref, oa_ref, ob_ref, oc_ref):
    oa_ref[...] = a_ref[...].astype(BF16)
    ob_ref[...] = b_ref[...].astype(BF16)
    oc_ref[...] = c_ref[...].astype(BF16)


def _prep_cast_call(w_out, w_ffn_in, w_ffn_out):
    ws = (w_out, w_ffn_in, w_ffn_out)
    specs = [pl.BlockSpec((None, w.shape[1] // PREP_STEPS, w.shape[2]), lambda l, j: (l, j, 0)) for w in ws]
    return pl.pallas_call(
        _prep_cast_kernel,
        grid=(DEPTH, PREP_STEPS),
        in_specs=specs,
        out_specs=specs,
        out_shape=[jax.ShapeDtypeStruct(w.shape, BF16) for w in ws],
        compiler_params=_cparams(("arbitrary", "arbitrary")),
        name="prep_cast",
    )(*ws)


def _ada_kernel(c_ref, w_ref, b_ref, o_ref):
    s = _silu(c_ref[...]).astype(BF16)
    o_ref[...] = _dot(s, w_ref[...].astype(BF16)) + b_ref[pl.ds(pl.program_id(0), 1), :]


def _ada_call(cvec, w_ada, b_ada):
    tn = 1536
    return pl.pallas_call(
        _ada_kernel,
        grid=(DEPTH, N_ADA // tn),
        in_specs=[
            pl.BlockSpec((MOD_ROWS, D_MODEL), lambda l, j: (0, 0)),
            pl.BlockSpec((None, D_MODEL, tn), lambda l, j: (l, 0, j)),
            pl.BlockSpec((DEPTH, tn), lambda l, j: (0, j)),
        ],
        out_specs=pl.BlockSpec((None, MOD_ROWS, tn), lambda l, j: (l, 0, j)),
        out_shape=jax.ShapeDtypeStruct((DEPTH, MOD_ROWS, N_ADA), F32),
        compiler_params=_cparams(("arbitrary", "arbitrary")),
        name="ada_mod",
    )(cvec, w_ada, b_ada)


def _mod_row(mod_ref, dec, tiles_per_seq):
    if not dec:
        return mod_ref[0:1, :]
    return mod_ref[pl.ds(1 + pl.program_id(0) // tiles_per_seq, 1), :]


def _rope_block(x, c, sa, sb, half):
    return x * c + pltpu.roll(x, LANES - half, 1) * sa + pltpu.roll(x, half, 1) * sb


def _inproj_kernel(layer, rope, x_ref, mod_ref, w_in_ref, w_uq_ref, w_uk_ref, w_uv_ref, qn_ref, kvn_ref, gqn_ref,
                   gkn_ref, ones_ref, *rest):
    if rope:
        (mc_ref, msa_ref, msb_ref, gc_ref, gsa_ref, gsb_ref,
         zhf_ref, zhb_ref, qm_ref, km_ref, vm_ref, gq_ref, gk_ref, gv_ref) = rest
    else:
        rest = rest[-12:]
        (zhf_ref, zhb_ref, qm_ref, km_ref, vm_ref, gq_ref, gk_ref, gv_ref,
         ckv_ref, kpe_ref, gk32_ref, gv32_ref) = rest
    lrow = slice(layer, layer + 1)
    qk_base = LOG2E
    mod = _mod_row(mod_ref, rope, DEC_SEQ // TM_IN)
    sh1 = mod[:, 0:D_MODEL]
    sc1 = mod[:, D_MODEL:2 * D_MODEL]
    h = (x_ref[...] * (1.0 + sc1) + sh1).astype(BF16)
    z = _dot(h, w_in_ref[...])
    gw = HG_HEADS * HG_DK
    zhf_ref[...] = z[:, gw:3 * gw]
    zhb_ref[:, 0:gw] = z[:, 0:gw].astype(BF16)
    zhb_ref[:, gw:3 * gw] = z[:, 3 * gw:HG_COLS].astype(BF16)
    mq = z[:, OFF_MQ:OFF_MKV]
    mkv = z[:, OFF_MKV:OFF_GQ]
    gq = z[:, OFF_GQ:OFF_GK]
    gk = z[:, OFF_GK:OFF_GV]
    gv = z[:, OFF_GV:OFF_KPE_PAD]
    kpe_pad = z[:, OFF_KPE_PAD:OFF_KPE]
    ones = ones_ref[...]
    gqn = _twice(gqn_ref[lrow, :])
    gkn_g = _twice(gkn_ref[lrow, :])

    q = _dot(_rms_rows(mq, qn_ref[lrow, :]).astype(BF16), w_uq_ref[...])
    ckv = _rms_rows(mkv, kvn_ref[lrow, :])
    ckv_b = ckv.astype(BF16)
    kn = _dot(ckv_b, w_uk_ref[...])
    vm_ref[...] = _dot(ckv_b, w_uv_ref[...]).astype(BF16)
    if rope:
        mc, msa, msb = mc_ref[...], msa_ref[...], msb_ref[...]
        kpe_pad = _rope_block(kpe_pad, mc, msa, msb, MLA_ROPE // 4)
    for hd in range(MLA_HEADS):
        sl = slice(hd * HEAD_PAD, (hd + 1) * HEAD_PAD)
        qh = q[:, sl]
        if rope:
            qh = _rope_block(qh, mc, msa, msb, MLA_ROPE // 4)
        qm_ref[:, sl] = (qh * (MLA_SCALE * qk_base)).astype(BF16)
        km_ref[:, sl] = (kn[:, sl] + kpe_pad).astype(BF16)

    gkn = gk * lax.rsqrt(_group_mean(gk * gk, ones) + EPS) * gkn_g
    for p in range(GQA_W // LANES):
        sl = slice(p * LANES, (p + 1) * LANES)
        gqp = gq[:, sl]
        gqp = gqp * lax.rsqrt(_group_mean(gqp * gqp, ones) + EPS) * gqn
        if rope:
            gqp = _rope_block(gqp, gc_ref[...], gsa_ref[...], gsb_ref[...], GQA_HD // 4)
        gq_ref[:, sl] = (gqp * (GQA_SCALE * qk_base)).astype(BF16)
    if rope:
        gk_ref[...] = _rope_block(gkn, gc_ref[...], gsa_ref[...], gsb_ref[...], GQA_HD // 4).astype(BF16)
    else:
        gk_ref[...] = gkn.astype(BF16)
        kpe = z[:, OFF_KPE:N_Z]
        for b in range(TM_IN // SEQ):
            rows = slice(b * SEQ, (b + 1) * SEQ)
            ckv_ref[b] = ckv[rows]
            kpe_ref[b] = kpe[rows]
            gk32_ref[b] = gkn[rows]
            gv32_ref[b] = gv[rows]
    gv_ref[...] = gv.astype(BF16)


def _inproj_call(x, mod, w, prm, cst, layer, rope, prev_states=None):
    t = x.shape[0]
    nt = t // TM_IN
    row = lambda i: (i, 0)
    in_specs = [
        pl.BlockSpec((TM_IN, D_MODEL), row),
        _layer_spec((MOD_ROWS, N_ADA), layer),
        _layer_spec((D_MODEL, N_Z), layer),
        _layer_spec((MLA_Q_RANK, MLA_QW), layer),
        _layer_spec((MLA_KV_RANK, MLA_QW), layer),
        _layer_spec((MLA_KV_RANK, MLA_W), layer),
        _const_spec((DEPTH, MLA_Q_RANK)),
        _const_spec((DEPTH, MLA_KV_RANK)),
        _const_spec((DEPTH, GQA_HD)),
        _const_spec((DEPTH, GQA_HD)),
        _const_spec((LANES, LANES)),
    ]
    args = [x, mod, w["w_in"], w["w_uq"], w["w_uk"], w["w_uv"], prm["mla_q_norm"], prm["mla_kv_norm"],
            prm["gqa_q_norm"], prm["gqa_k_norm"], cst["ones64"]]
    outs = [(2 * HG_HEADS * HG_DK, F32), (3 * HG_HEADS * HG_DK, BF16), (MLA_QW, BF16), (MLA_QW, BF16), (MLA_W, BF16),
            (GQA_W, BF16), (LANES, BF16), (LANES, BF16)]
    out_specs = [pl.BlockSpec((TM_IN, wd), row) for wd, _ in outs]
    out_shape = [jax.ShapeDtypeStruct((t, wd), dt) for wd, dt in outs]
    aliases = {}
    if rope:
        tab = lambda i: (i % (DEC_SEQ // TM_IN), 0)
        in_specs += [pl.BlockSpec((TM_IN, LANES), tab)] * 6
        args += [cst["mla_c"], cst["mla_sa"], cst["mla_sb"], cst["gqa_c"], cst["gqa_sa"], cst["gqa_sb"]]
    else:
        per = TM_IN // SEQ
        for wd in (MLA_KV_RANK, MLA_ROPE, LANES, LANES):
            out_specs.append(pl.BlockSpec((per, None, SEQ, wd), lambda i: (i, layer, 0, 0)))
            out_shape.append(jax.ShapeDtypeStruct((BATCH, DEPTH, SEQ, wd), F32))
        if prev_states is not None:
            for k, arr in enumerate(prev_states):
                aliases[len(args)] = len(outs) + k
                in_specs.append(_ANY)
                args.append(arr)
    return pl.pallas_call(
        functools.partial(_inproj_kernel, layer, rope),
        grid=(nt,),
        in_specs=in_specs,
        out_specs=out_specs,
        out_shape=out_shape,
        input_output_aliases=aliases,
        compiler_params=_cparams(("arbitrary",)),
        name="inproj_dec" if rope else "inproj_ctx",
    )(*args)


def _cachekv_kernel(ckv_ref, kpe_ref, w_uk_ref, w_uv_ref, place_ref, k_ref, v_ref):
    ckv_b = ckv_ref[...].astype(BF16)
    k_ref[...] = (_dot(ckv_b, w_uk_ref[...]) + _dot(kpe_ref[...].astype(BF16), place_ref[...])).astype(BF16)
    v_ref[...] = _dot(ckv_b, w_uv_ref[...]).astype(BF16)


def _cachekv_call(cache_ckv, cache_kpe, w, cst, layer):
    t = DEC_BATCH * PAST_LEN
    cache = lambda wd: pl.BlockSpec((None, None, PAST_LEN, wd), lambda b: (b, layer, 0, 0))
    return pl.pallas_call(
        _cachekv_kernel,
        grid=(DEC_BATCH,),
        in_specs=[cache(MLA_KV_RANK), cache(MLA_ROPE),
                  _layer_spec((MLA_KV_RANK, MLA_QW), layer), _layer_spec((MLA_KV_RANK, MLA_W), layer),
                  _const_spec((MLA_ROPE, MLA_QW))],
        out_specs=[pl.BlockSpec((PAST_LEN, MLA_QW), lambda b: (b, 0)),
                   pl.BlockSpec((PAST_LEN, MLA_W), lambda b: (b, 0))],
        out_shape=[jax.ShapeDtypeStruct((t, MLA_QW), BF16), jax.ShapeDtypeStruct((t, MLA_W), BF16)],
        compiler_params=_cparams(("arbitrary",)),
        name="cache_kv",
    )(cache_ckv, cache_kpe, w["w_uk"], w["w_uv"], cst["kpe_place"])


def _bcast_row(x, r, rows):
    return jnp.broadcast_to(x[r:r + 1, :], (rows, x.shape[1]))


def _cumsum_rows(tri, x):
    hi = x.astype(BF16)
    r1 = x - hi.astype(F32)
    mid = r1.astype(BF16)
    lo = (r1 - mid.astype(F32)).astype(BF16)
    return _dot(tri, hi) + _dot(tri, mid) + _dot(tri, lo)


def _hgrn_chunks(items, states, tri, lvl, stk, blk, lane_lo):
    C = HG_CHUNK
    zero_b = jnp.zeros((SUBLANES, LANES), F32)
    trow = lax.broadcasted_iota(jnp.int32, (SUBLANES, LANES), 0)
    for it in items:
        it["c"] = _cumsum_rows(tri[it["fwd"]], it["lf"])
    for it in items:
        it["q_a"] = jnp.where(lane_lo, it["q"], 0.0)
        it["q_b"] = jnp.where(lane_lo, 0.0, it["q"])
        it["ck"] = it["c"] - jnp.log(it["kk"])

    m = C // 2
    bit = m.bit_length() - 1
    while m >= SUBLANES:
        for it in items:
            fwd, c = it["fwd"], it["c"]
            qa_rows, qb_rows, k_rows = [], [], []
            for b in range(C // m):
                rows = slice(b * m, (b + 1) * m)
                anchor = (b // 2) * 2 * m + (m - 1 if fwd else m)
                r = _bcast_row(c, anchor, m)
                if (b % 2 == 1) == fwd:
                    e = jnp.exp(c[rows] - r)
                    qa_rows.append(it["q_a"][rows] * e)
                    qb_rows.append(it["q_b"][rows] * e)
                    k_rows += [zero_b] * (m // SUBLANES)
                else:
                    e = jnp.exp(r - c[rows])
                    qa_rows += [zero_b] * (m // SUBLANES)
                    qb_rows += [zero_b] * (m // SUBLANES)
                    k_rows.append(it["kk"][rows] * e)
            prod = _dot_nt(jnp.concatenate(qa_rows + qb_rows, axis=0).astype(BF16),
                           jnp.concatenate(k_rows, axis=0).astype(BF16))
            it["a"] = prod if "a" not in it else jnp.where(lvl[fwd] == bit, prod, it["a"])
        m //= 2
        bit -= 1

    pk = 2 * SUBLANES
    trow = lax.broadcasted_iota(jnp.int32, (pk, LANES), 0) & (SUBLANES - 1)
    zero_w = jnp.zeros((pk, LANES), BF16)
    for it in items:
        it["w"] = []
        it["q_bf"] = it["q"].astype(BF16)
    for j in range(C // pk):
        sl = slice(j * pk, (j + 1) * pk)
        for it in items:
            cs, cks, qs = it["c"][sl], it["ck"][sl], it["q_bf"][sl]
            w_s = []
            for s in range(SUBLANES):
                ref = jnp.concatenate([_bcast_row(cks, s, SUBLANES), _bcast_row(cks, SUBLANES + s, SUBLANES)], axis=0)
                e = jnp.exp(cs - ref).astype(BF16)
                keep = (trow >= s) if it["fwd"] else (trow <= s)
                w_s.append(jnp.where(keep, qs * e, zero_w))
            it["w"].append(jnp.concatenate(w_s, axis=1))
    for it in items:
        red = _dot(jnp.concatenate(it["w"], axis=0), stk)
        it["a"] = jnp.where(lvl[it["fwd"]] == 0,
                            jnp.concatenate([red[:, 0:LANES], red[:, LANES:2 * LANES]], axis=0), it["a"])
    for it in items:
        off = _dot(it["a"].astype(BF16), it["iv"].astype(BF16))
        it["o"] = jnp.where(lane_lo, off[0:C], off[C:2 * C])
        it["edge"] = _bcast_row(it["c"], C - 1 if it["fwd"] else 0, C)
        it["q_dec"] = (it["q"] * jnp.exp(it["c"])).astype(BF16)
        k_dec = (it["kk"] * jnp.exp(it["edge"] - it["c"])).astype(BF16)
        it["upd"] = _dot_tn(it["iv"].astype(BF16), k_dec) * blk

    states = list(states)
    outs = []
    for it in items:
        st = states[it["sid"]]
        outs.append(it["o"] + _dot_nt(it["q_dec"], st.astype(BF16)))
        states[it["sid"]] = st * jnp.exp(it["edge"][0:1]) + it["upd"]
    return outs, states


def _hgrn_kernel(layer, nchunks, nseq, has_state, zq_ref, zff_ref, zfb_ref, zi_ref, zg_ref, lb_ref, g_ref,
                 trif_ref, trib_ref, lvlf_ref, lvlb_ref, stk_ref, ones_ref, blk_ref, *rest):
    if has_state:
        s0_ref, o_ref, of_scr, ob_scr = rest
    else:
        o_ref, st_ref, of_scr, ob_scr = rest[-4:]
    C = HG_CHUNK
    npair = HG_HEADS // 2
    lane_lo = lax.broadcasted_iota(jnp.int32, (1, LANES), 1) < HG_DK
    pair_lanes = [slice(hp * LANES, (hp + 1) * LANES) for hp in range(npair)]

    lb_all = [lb_ref[j] for j in range(DEPTH)]
    mx = functools.reduce(jnp.maximum, lb_all)
    ex = [jnp.exp(v - mx) for v in lb_all]
    den = functools.reduce(lambda a, b: a + b, ex)
    sm = [v / den for v in ex]
    lbs = functools.reduce(lambda a, b: a + b, sm[:layer + 1]) - sm[0]

    def gates(zf, lbd):
        f = lbd + (1.0 - lbd) * jax.nn.sigmoid(zf)
        return jnp.log(jnp.maximum(f, GATE_FLOOR)), jnp.maximum(1.0 - f, 0.0)

    def block_diag_t(sa, sb):
        za = jnp.zeros((HG_DK, HG_DV), F32)
        top = jnp.concatenate([sa, za], axis=1)
        bot = jnp.concatenate([za, sb], axis=1)
        return jnp.concatenate([top, bot], axis=0).T

    if has_state:
        st0 = tuple(block_diag_t(s0_ref[sq, d, 2 * hp], s0_ref[sq, d, 2 * hp + 1])
                    for sq in range(nseq) for hp in range(npair) for d in range(2))
    else:
        st0 = (jnp.zeros((LANES, LANES), F32),) * (2 * npair * nseq)

    def body(j, carry):
        tri = {True: trif_ref[...], False: trib_ref[...]}
        lvl = {True: lvlf_ref[...], False: lvlb_ref[...]}
        items = []
        for u in range(HG_UNROLL):
            jf = j * HG_UNROLL + u
            for sq in range(nseq):
                for hp, ln in enumerate(pair_lanes):
                    for fwd in (True, False):
                        chunk = sq * nchunks + (jf if fwd else nchunks - 1 - jf)
                        rows = pl.ds(pl.multiple_of(chunk * C, C), C)
                        d = 0 if fwd else 1
                        lf, kk = gates((zff_ref if fwd else zfb_ref)[rows, ln], lbs[d:d + 1, ln])
                        items.append({"fwd": fwd, "sid": (sq * npair + hp) * 2 + d, "rows": rows, "ln": ln,
                                      "lf": lf, "kk": kk, "iv": zi_ref[rows, ln],
                                      "q": _silu(zq_ref[rows, ln].astype(F32)) * (HG_DK ** -0.5)})
        outs, states = _hgrn_chunks(items, carry, tri, lvl, stk_ref[...], blk_ref[...], lane_lo)
        for it, o in zip(items, outs):
            (of_scr if it["fwd"] else ob_scr)[it["rows"], it["ln"]] = o
        return tuple(states)

    states = lax.fori_loop(0, nchunks // HG_UNROLL, body, st0)

    g = _twice(g_ref[layer:layer + 1, :])
    for ln in pair_lanes:
        o = of_scr[:, ln] + ob_scr[:, ln]
        ms = _group_mean(o * o, ones_ref[...] * (1.0 / HG_DV))
        o_ref[:, ln] = (o * lax.rsqrt(ms + EPS) * g * _silu(zg_ref[:, ln].astype(F32))).astype(BF16)
    if not has_state:
        for sq in range(nseq):
            for hp in range(npair):
                for d in range(2):
                    s_kv = states[(sq * npair + hp) * 2 + d].T
                    st_ref[sq, d, 2 * hp] = s_kv[0:HG_DK, 0:HG_DV]
                    st_ref[sq, d, 2 * hp + 1] = s_kv[HG_DK:2 * HG_DK, HG_DV:2 * HG_DV]


def _hgrn_call(zhf, zhb, lb, g, cst, layer, nb, s, state_in=None, prev_state_out=None):
    has_state = state_in is not None
    nchunks = s // HG_CHUNK
    nseq = HG_SEQS_DEC if has_state else HG_SEQS_CTX
    part = lambda k: pl.BlockSpec((nseq * s, HG_W), lambda b: (b, k))
    st_spec = pl.BlockSpec((nseq, None, 2, HG_HEADS, HG_DK, HG_DV), lambda b: (b, layer, 0, 0, 0, 0))
    in_specs = [part(0), part(0), part(1), part(1), part(2),
                _const_spec((DEPTH, 2, HG_W)),
                _const_spec((DEPTH, HG_DV)),
                _const_spec((HG_CHUNK, HG_CHUNK)), _const_spec((HG_CHUNK, HG_CHUNK)),
                _const_spec((2 * HG_CHUNK, HG_CHUNK)), _const_spec((2 * HG_CHUNK, HG_CHUNK)),
                _const_spec((SUBLANES * LANES, 2 * LANES)),
                _const_spec((LANES, LANES)), _const_spec((LANES, LANES))]
    args = [zhb, zhf, zhf, zhb, zhb, lb, g, cst["tri_f"], cst["tri_b"], cst["lvl_f"], cst["lvl_b"], cst["stk"],
            cst["ones"], cst["blk"]]
    out_specs = [pl.BlockSpec((nseq * s, HG_W), lambda b: (b, 0))]
    out_shape = [jax.ShapeDtypeStruct((nb * s, HG_W), BF16)]
    aliases = {}
    if has_state:
        in_specs.append(st_spec)
        args.append(state_in)
    else:
        out_specs.append(st_spec)
        out_shape.append(jax.ShapeDtypeStruct((nb, DEPTH, 2, HG_HEADS, HG_DK, HG_DV), F32))
        if prev_state_out is not None:
            aliases[len(args)] = 1
            in_specs.append(_ANY)
            args.append(prev_state_out)
    return pl.pallas_call(
        functools.partial(_hgrn_kernel, layer, nchunks, nseq, has_state),
        grid=(nb // nseq,),
        in_specs=in_specs,
        out_specs=out_specs,
        out_shape=out_shape,
        input_output_aliases=aliases,
        scratch_shapes=[pltpu.VMEM((nseq * s, HG_W), F32), pltpu.VMEM((nseq * s, HG_W), F32)],
        compiler_params=_cparams(("arbitrary",)),
        name="hgrn_dec" if has_state else "hgrn_ctx",
    )(*args)


def _attend_t(jobs, ss):
    add = lambda a, b: a + b
    mxs = [functools.reduce(jnp.maximum, [jnp.max(s, axis=-1, keepdims=True) for s in sj]) for sj in ss]
    ebs = [[jnp.exp2(s - mx).astype(BF16) for s in sj] for sj, mx in zip(ss, mxs)]
    prods = [[functools.reduce(add, [_dot_nt(vt, e[rows]) for vt, e in zip(vts, ej)]) for rows, vts in parts]
             for ej, (_, _, parts) in zip(ebs, jobs)]
    return [[pr[0:MLA_V] * (1.0 / pr[MLA_V:MLA_V + 1]) for pr in pj] for pj in prods]


def _scores(jobs):
    return [[_dot_nt(q, k) for k in ks] for q, ks, _ in jobs]


def _attn_kernel(has_cache, qm_ref, gq_ref, km_ref, vm_ref, gk_ref, gv_ref, *rest):
    if has_cache:
        kc_ref, vc_ref, gkc_ref, gvc_ref, o_ref, vt_scr = rest
    else:
        o_ref, vt_scr = rest
    sq = qm_ref.shape[0]
    sk = vt_scr.shape[1]
    nblk = MLA_HEADS + GQA_KV_HEADS
    cols = [slice(0, PAST_LEN), slice(PAST_LEN, sk)] if has_cache else [slice(0, sk)]

    def fill():
        tr = lambda v: v.astype(F32).T.astype(BF16)
        srcs = [(vc_ref, gvc_ref), (vm_ref, gv_ref)] if has_cache else [(vm_ref, gv_ref)]
        for cl, (mla_ref, gqa_ref) in zip(cols, srcs):
            mla_t, gqa_t = tr(mla_ref[...]), tr(gqa_ref[...])
            tail = (lax.broadcasted_iota(jnp.int32, (VT_ROWS - MLA_V, cl.stop - cl.start), 0) == 0).astype(BF16)
            for b in range(nblk):
                src = mla_t[b * MLA_V:(b + 1) * MLA_V] if b < MLA_HEADS else \
                    gqa_t[(b - MLA_HEADS) * GQA_HD:(b - MLA_HEADS + 1) * GQA_HD]
                vt_scr[b * VT_ROWS:b * VT_ROWS + MLA_V, cl] = src
                vt_scr[b * VT_ROWS + MLA_V:(b + 1) * VT_ROWS, cl] = tail

    if has_cache:
        pl.when(pl.program_id(1) == 0)(fill)
    else:
        fill()

    lo = lax.broadcasted_iota(jnp.int32, (GQA_ROWS, LANES), 1) < GQA_HD
    zero = jnp.zeros((GQA_ROWS, LANES), BF16)
    if has_cache:
        seqs = [(slice(0, sq), slice(0, None), cols)]
    else:
        seqs = [(slice(r, r + SEQ), slice(r, r + SEQ), [slice(r, r + SEQ)]) for r in range(0, sq, SEQ)]
    jobs, first_job = [], []
    for qrows, krows, vcols in seqs:
        first_job.append(len(jobs))

        def vt(b, vcols=vcols):
            return [vt_scr[b * VT_ROWS:(b + 1) * VT_ROWS, cl] for cl in vcols]

        gks = [gkc_ref[...].astype(BF16), gk_ref[...]] if has_cache else [gk_ref[krows, :]]
        for hd in range(MLA_HEADS):
            sl = slice(hd * HEAD_PAD, (hd + 1) * HEAD_PAD)
            ks = [kc_ref[:, sl], km_ref[:, sl]] if has_cache else [km_ref[krows, sl]]
            jobs.append((qm_ref[qrows, sl], ks, [(slice(0, qrows.stop - qrows.start), vt(hd))]))
        halves = [(slice(0, GQA_ROWS), vt(MLA_HEADS)), (slice(GQA_ROWS, 2 * GQA_ROWS), vt(MLA_HEADS + 1))]
        for r0 in range(qrows.start, qrows.stop, GQA_ROWS):
            for p in range(GQA_W // LANES):
                qp = gq_ref[r0:r0 + GQA_ROWS, p * LANES:(p + 1) * LANES]
                jobs.append((jnp.concatenate([jnp.where(lo, qp, zero), jnp.where(lo, zero, qp)], axis=0), gks,
                             halves))
    group = ATT_GROUP_DEC if has_cache else ATT_GROUP_CTX
    groups = [jobs[j0:j0 + group] for j0 in range(0, len(jobs), group)]
    outs = []
    ss_next = _scores(groups[0])
    for g, grp in enumerate(groups):
        ss = ss_next
        if g + 1 < len(groups):
            ss_next = _scores(groups[g + 1])
        outs += _attend_t(grp, ss)
    for (qrows, _, _), j0 in zip(seqs, first_job):
        for p in range(MLA_HEADS // 2):
            pair = jnp.concatenate([outs[j0 + 2 * p][0], outs[j0 + 2 * p + 1][0]], axis=0)
            o_ref[qrows, p * LANES:(p + 1) * LANES] = pair.T.astype(BF16)
        for i, r0 in enumerate(range(qrows.start, qrows.stop, GQA_ROWS)):
            gres = [jnp.concatenate(outs[j0 + MLA_HEADS + 3 * i + p], axis=0).T for p in range(GQA_W // LANES)]
            swapped = pltpu.roll(gres[1], GQA_HD, 1)
            blocks = (jnp.where(lo, gres[0], swapped), jnp.where(lo, gres[2], gres[0]),
                      jnp.where(lo, swapped, gres[2]))
            for p, blkv in enumerate(blocks):
                o_ref[r0:r0 + GQA_ROWS, MLA_W + p * LANES:MLA_W + (p + 1) * LANES] = blkv.astype(BF16)


def _attn_call(qm, gq, km, vm, gk, gv, cache, layer):
    has_cache = cache is not None
    t = qm.shape[0]
    widths = (MLA_QW, GQA_W, MLA_QW, MLA_W, LANES, LANES)
    if has_cache:
        nq = DEC_SEQ // TQ_DEC
        grid = (DEC_BATCH, nq)
        qmap = lambda b, i: (b * nq + i, 0)
        kmap = lambda b, i: (b, 0)
        once = pl.Buffered(1)
        in_specs = [pl.BlockSpec((TQ_DEC, widths[0]), qmap), pl.BlockSpec((TQ_DEC, widths[1]), qmap)]
        in_specs += [pl.BlockSpec((DEC_SEQ, wd), kmap, pipeline_mode=once) for wd in widths[2:]]
        in_specs += [pl.BlockSpec((PAST_LEN, wd), kmap, pipeline_mode=once) for wd in widths[2:4]]
        in_specs += [pl.BlockSpec((None, None, PAST_LEN, LANES), lambda b, i: (b, layer, 0, 0),
                                  pipeline_mode=once)] * 2
        args = [qm, gq, km, vm, gk, gv, *cache]
        out_spec = pl.BlockSpec((TQ_DEC, MLA_W + GQA_W), qmap)
        sk = PAST_LEN + DEC_SEQ
        sem = ("arbitrary", "arbitrary")
    else:
        rows = ATT_SEQS_CTX * SEQ
        grid = (t // rows,)
        bmap = lambda b: (b, 0)
        in_specs = [pl.BlockSpec((rows, wd), bmap) for wd in widths]
        args = [qm, gq, km, vm, gk, gv]
        out_spec = pl.BlockSpec((rows, MLA_W + GQA_W), bmap)
        sk = rows
        sem = ("arbitrary",)
    return pl.pallas_call(
        functools.partial(_attn_kernel, has_cache),
        grid=grid,
        in_specs=in_specs,
        out_specs=out_spec,
        out_shape=jax.ShapeDtypeStruct((t, MLA_W + GQA_W), BF16),
        scratch_shapes=[pltpu.VMEM(((MLA_HEADS + GQA_KV_HEADS) * VT_ROWS, sk), BF16)],
        compiler_params=_cparams(sem),
        name="attn_dec" if has_cache else "attn_ctx",
    )(*args)


def _ffn_kernel(layer, dec, hg_ref, at_ref, x_ref, mod_ref, w_out_ref, ln1g_ref, ln1b_ref, w_gu_ref, w_dn_ref,
                ln2g_ref, ln2b_ref, o_ref):
    lrow = slice(layer, layer + 1)
    mod = _mod_row(mod_ref, dec, DEC_SEQ // TM_FFN)
    g1 = mod[:, 2 * D_MODEL:3 * D_MODEL]
    sh2 = mod[:, 3 * D_MODEL:4 * D_MODEL]
    sc2 = mod[:, 4 * D_MODEL:5 * D_MODEL]
    g2 = mod[:, 5 * D_MODEL:6 * D_MODEL]
    y = _dot(hg_ref[...], w_out_ref[0:HG_W, :]) + _dot(at_ref[...], w_out_ref[HG_W:D_MODEL, :])
    x1 = _layernorm_rows(ALPHA * x_ref[...] + g1 * y, ln1g_ref[lrow, :], ln1b_ref[lrow, :])
    hf = (x1 * (1.0 + sc2) + sh2).astype(BF16)
    ch = D_FF // FF_CHUNKS
    f = None
    for c in range(FF_CHUNKS):
        gate = _dot(hf, w_gu_ref[:, c * ch:(c + 1) * ch])
        up = _dot(hf, w_gu_ref[:, D_FF + c * ch:D_FF + (c + 1) * ch])
        part = _dot((_silu(gate) * up).astype(BF16), w_dn_ref[c * ch:(c + 1) * ch, :])
        f = part if f is None else f + part
    o_ref[...] = _layernorm_rows(ALPHA * x1 + g2 * f, ln2g_ref[lrow, :], ln2b_ref[lrow, :])


def _ffn_call(hg, at, x, mod, w, prm, layer, dec):
    t = x.shape[0]
    row = lambda i: (i, 0)
    once = pl.Buffered(1)
    vec = pl.BlockSpec((DEPTH, D_MODEL), lambda i: (0, 0), pipeline_mode=once)
    return pl.pallas_call(
        functools.partial(_ffn_kernel, layer, dec),
        grid=(t // TM_FFN,),
        in_specs=[
            pl.BlockSpec((TM_FFN, HG_W), row),
            pl.BlockSpec((TM_FFN, MLA_W + GQA_W), row),
            pl.BlockSpec((TM_FFN, D_MODEL), row),
            _layer_spec((MOD_ROWS, N_ADA), layer, pipeline_mode=once),
            _layer_spec((D_MODEL, D_MODEL), layer, pipeline_mode=once),
            vec, vec,
            _layer_spec((D_MODEL, 2 * D_FF), layer, pipeline_mode=once),
            _layer_spec((D_FF, D_MODEL), layer, pipeline_mode=once),
            vec, vec,
        ],
        out_specs=pl.BlockSpec((TM_FFN, D_MODEL), row),
        out_shape=jax.ShapeDtypeStruct((t, D_MODEL), F32),
        compiler_params=_cparams(("arbitrary",)),
        name="ffn_dec" if dec else "ffn_ctx",
    )(hg, at, x, mod, w["w_out"], prm["ln1_g"], prm["ln1_b"], w["w_gu"], w["w_dn"], prm["ln2_g"], prm["ln2_b"])


def _rope_tables(width, rot_off, rot_dim):
    t = np.arange(DEC_SEQ)
    pos = (t // GRID_W, t % GRID_W)
    half = rot_dim // 4
    freqs = ROPE_THETA ** (-np.arange(half, dtype=np.float64) / half)
    c = np.ones((DEC_SEQ, LANES), np.float64)
    sa = np.zeros((DEC_SEQ, LANES), np.float64)
    sb = np.zeros((DEC_SEQ, LANES), np.float64)
    for blk in range(LANES // width):
        for axis in range(2):
            base = blk * width + rot_off + axis * 2 * half
            ang = pos[axis][:, None].astype(np.float64) * freqs[None, :]
            c[:, base:base + half] = np.cos(ang)
            c[:, base + half:base + 2 * half] = np.cos(ang)
            sa[:, base:base + half] = -np.sin(ang)
            sb[:, base + half:base + 2 * half] = np.sin(ang)
    return tuple(jnp.asarray(a, F32) for a in (c, sa, sb))


def _constants():
    C = HG_CHUNK
    t = np.arange(C)
    tri_f = (t[None, :] <= t[:, None]).astype(np.float32)
    tri_b = (t[None, :] >= t[:, None]).astype(np.float32)
    x = t[:, None] ^ t[None, :]
    top = np.where(x > 0, np.floor(np.log2(np.maximum(x, 1))).astype(np.int32), -1)
    same = (t[:, None] // SUBLANES) == (t[None, :] // SUBLANES)
    lvl_f = np.where(same, np.where(t[:, None] >= t[None, :], 0, -1),
                     np.where(t[:, None] > t[None, :], top, -1)).astype(np.int32)
    lvl_b = np.where(same, np.where(t[:, None] <= t[None, :], 0, -1),
                     np.where(t[:, None] < t[None, :], top, -1)).astype(np.int32)
    grp = np.arange(LANES) // HG_DK
    blk = (grp[:, None] == grp[None, :]).astype(np.float32)
    stk = np.zeros((SUBLANES, LANES, 2, C), np.float32)
    for s in range(SUBLANES):
        for hb in range(2):
            stk[s, hb * HG_DK:(hb + 1) * HG_DK, hb, s::SUBLANES] = 1.0
    stk = stk.reshape(SUBLANES * LANES, 2 * C)
    place = np.zeros((MLA_ROPE, MLA_QW), np.float32)
    for hd in range(MLA_HEADS):
        place[np.arange(MLA_ROPE), hd * HEAD_PAD + MLA_NOPE + np.arange(MLA_ROPE)] = 1.0
    mla_c, mla_sa, mla_sb = _rope_tables(HEAD_PAD, MLA_NOPE, MLA_ROPE)
    gqa_c, gqa_sa, gqa_sb = _rope_tables(GQA_HD, 0, GQA_HD)
    return {
        "tri_f": jnp.asarray(tri_f, BF16), "tri_b": jnp.asarray(tri_b, BF16),
        "lvl_f": jnp.asarray(np.concatenate([lvl_f, lvl_f], axis=0)),
        "lvl_b": jnp.asarray(np.concatenate([lvl_b, lvl_b], axis=0)),
        "stk": jnp.asarray(stk, BF16), "ones": jnp.asarray(blk, BF16), "ones64": jnp.asarray(blk / GQA_HD, BF16),
        "blk": jnp.asarray(blk, F32), "kpe_place": jnp.asarray(place, BF16),
        "mla_c": mla_c, "mla_sa": mla_sa, "mla_sb": mla_sb, "gqa_c": gqa_c, "gqa_sa": gqa_sa, "gqa_sb": gqa_sb,
    }


def kernel(x_prompt, x_sample, cache_mla_ckv, cache_mla_kpe, cache_gqa_k, cache_gqa_v, state_hgrn, c, c_ctx, w_ada,
           b_ada, w_in, hg_lb, hg_norm, mla_q_norm, mla_w_uq, mla_kv_norm, mla_w_ukv, gqa_q_norm, gqa_k_norm, w_out,
           ln1_g, ln1_b, w_ffn_in, w_ffn_out, ln2_g, ln2_b):
    cst = _constants()
    prm = {"mla_q_norm": mla_q_norm, "mla_kv_norm": mla_kv_norm, "gqa_q_norm": gqa_q_norm, "gqa_k_norm": gqa_k_norm,
           "ln1_g": ln1_g, "ln1_b": ln1_b, "ln2_g": ln2_g, "ln2_b": ln2_b}
    w = {"w_in": _prep_in_call(w_in)}
    w["w_uq"], w["w_uk"], w["w_uv"] = _prep_mla_call(mla_w_uq, mla_w_ukv)
    w["w_out"], w["w_gu"], w["w_dn"] = _prep_cast_call(w_out, w_ffn_in, w_ffn_out)

    cvec = jnp.concatenate([c_ctx[None, :], c, jnp.zeros((MOD_ROWS - 1 - DEC_BATCH, D_MODEL), F32)], axis=0)
    mod = _ada_call(cvec, w_ada, b_ada)
    cache_gk = cache_gqa_k.reshape(DEC_BATCH, DEPTH, PAST_LEN, LANES)
    cache_gv = cache_gqa_v.reshape(DEC_BATCH, DEPTH, PAST_LEN, LANES)

    x_ctx = x_prompt.reshape(BATCH * SEQ, D_MODEL)
    x_dec = x_sample.reshape(DEC_BATCH * DEC_SEQ, D_MODEL)
    states, hg_state = None, None
    for l in range(DEPTH):
        zhf, zhb, qm, km, vm, gq, gk, gv, *states = _inproj_call(x_ctx, mod, w, prm, cst, l, rope=False,
                                                          prev_states=states)
        hg_out, hg_state = _hgrn_call(zhf, zhb, hg_lb, hg_norm, cst, l, BATCH, SEQ, prev_state_out=hg_state)
        at = _attn_call(qm, gq, km, vm, gk, gv, None, l)
        x_ctx = _ffn_call(hg_out, at, x_ctx, mod, w, prm, l, dec=False)

        zhf, zhb, qm, km, vm, gq, gk, gv = _inproj_call(x_dec, mod, w, prm, cst, l, rope=True)
        hg_out = _hgrn_call(zhf, zhb, hg_lb, hg_norm, cst, l, DEC_BATCH, DEC_SEQ, state_in=state_hgrn)[0]
        kc, vc = _cachekv_call(cache_mla_ckv, cache_mla_kpe, w, cst, l)
        at = _attn_call(qm, gq, km, vm, gk, gv, (kc, vc, cache_gk, cache_gv), l)
        x_dec = _ffn_call(hg_out, at, x_dec, mod, w, prm, l, dec=True)

    ckv, kpe, gk32, gv32 = states
    kv_shape = (BATCH, DEPTH, SEQ, GQA_KV_HEADS, GQA_HD)
    return (x_ctx.reshape(BATCH, SEQ, D_MODEL), x_dec.reshape(DEC_BATCH, DEC_SEQ, D_MODEL),
            ckv, kpe, gk32.reshape(kv_shape), gv32.reshape(kv_shape), hg_state)
```

```python
import functools

import numpy as np
import jax
import jax.numpy as jnp
from jax import lax
from jax.experimental import pallas as pl
from jax.experimental.pallas import tpu as pltpu

F32 = jnp.float32
BF16 = jnp.bfloat16

D_MODEL = 1024
BATCH = 32
SEQ = 256
DEPTH = 2
DEC_BATCH = 2
DEC_SEQ = 2048
PAST_LEN = 256
GRID_W = 64
ROPE_THETA = 10000.0
EPS = 1e-6
GATE_FLOOR = 1e-30
HG_HEADS = 4
HG_DK = 64
HG_DV = 64
MLA_HEADS = 6
MLA_NOPE = 64
MLA_ROPE = 32
MLA_V = 64
MLA_Q_RANK = 256
MLA_KV_RANK = 128
GQA_HEADS = 6
GQA_KV_HEADS = 2
GQA_HD = 64
HG_W = HG_HEADS * HG_DV
MLA_W = MLA_HEADS * MLA_V
GQA_W = GQA_HEADS * GQA_HD
D_FF = 2816
MLA_SCALE = (MLA_NOPE + MLA_ROPE) ** -0.5
GQA_SCALE = GQA_HD ** -0.5
ALPHA = (2 * DEPTH) ** 0.25
LOG2E = 1.4426950408889634

LANES = 128
SUBLANES = 8
HEAD_PAD = 128
MLA_QW = MLA_HEADS * HEAD_PAD
N_ADA = 6 * D_MODEL
MOD_ROWS = 8

N_IN = 5 * HG_HEADS * HG_DK + MLA_Q_RANK + MLA_KV_RANK + MLA_ROPE + GQA_W + 2 * GQA_KV_HEADS * GQA_HD
SRC_KPE = 5 * HG_HEADS * HG_DK + MLA_Q_RANK + MLA_KV_RANK
SRC_GQ = SRC_KPE + MLA_ROPE
SRC_GK = SRC_GQ + GQA_W
SRC_GV = SRC_GK + GQA_KV_HEADS * GQA_HD
HG_COLS = 5 * HG_HEADS * HG_DK
OFF_MQ = HG_COLS
OFF_MKV = OFF_MQ + MLA_Q_RANK
OFF_GQ = OFF_MKV + MLA_KV_RANK
OFF_GK = OFF_GQ + GQA_W
OFF_GV = OFF_GK + GQA_KV_HEADS * GQA_HD
OFF_KPE_PAD = OFF_GV + GQA_KV_HEADS * GQA_HD
OFF_KPE = OFF_KPE_PAD + HEAD_PAD
N_Z = OFF_KPE + MLA_ROPE

TM_IN = 512
TM_FFN = 512
TQ_DEC = 512
GQA_ROWS = 256
ATT_GROUP_CTX = 9
ATT_SEQS_CTX = 2
ATT_GROUP_DEC = 1
VT_ROWS = 80
HG_CHUNK = 128
HG_UNROLL = 2
HG_UNROLL_DEC = 4
HG_SEQS_CTX = 4
HG_SEQS_DEC = 1
FF_CHUNKS = 2
PREP_STEPS = 4
VMEM_LIMIT = 56 * 1024 * 1024


def _cparams(sem):
    return pltpu.CompilerParams(dimension_semantics=sem, vmem_limit_bytes=VMEM_LIMIT)


def _const_spec(shape):
    nd = len(shape)
    return pl.BlockSpec(shape, lambda *_: (0,) * nd)


def _layer_spec(shape, layer, **kw):
    nd = len(shape)
    return pl.BlockSpec((None,) + tuple(shape), lambda *_: (layer,) + (0,) * nd, **kw)


_ANY = pl.BlockSpec(memory_space=pl.ANY)


def _silu(x):
    return x * jax.nn.sigmoid(x)


def _dot(a, b):
    return jnp.dot(a, b, preferred_element_type=F32)


def _dot_nt(a, b):
    return lax.dot_general(a, b, (((1,), (1,)), ((), ())), preferred_element_type=F32)


def _dot_tn(a, b):
    return lax.dot_general(a, b, (((0,), (0,)), ((), ())), preferred_element_type=F32)


def _rms_rows(x, g):
    y = x * lax.rsqrt(jnp.mean(x * x, axis=-1, keepdims=True) + EPS)
    return y * g


def _layernorm_rows(x, g, b):
    xc = x - jnp.mean(x, axis=-1, keepdims=True)
    var = jnp.mean(xc * xc, axis=-1, keepdims=True)
    return xc * lax.rsqrt(var + EPS) * g + b


def _group_mean(x, ones_blk):
    return _dot(x.astype(BF16), ones_blk)


def _twice(g):
    return jnp.concatenate([g, g], axis=1)


_GQ_HEAD_ORDER = (0, 3, 1, 4, 2, 5)


def _prep_in_kernel(w_ref, o_ref):
    x = w_ref[...]
    rows = x.shape[0]
    kr = x[:, SRC_KPE:SRC_GQ]
    gq = [x[:, SRC_GQ + h * GQA_HD:SRC_GQ + (h + 1) * GQA_HD] for h in _GQ_HEAD_ORDER]
    pieces = [x[:, 0:SRC_KPE]] + gq + [
        x[:, SRC_GK:SRC_GV], x[:, SRC_GV:N_IN],
        jnp.zeros((rows, MLA_NOPE), F32), kr, jnp.zeros((rows, HEAD_PAD - MLA_NOPE - MLA_ROPE), F32), kr]
    o_ref[...] = jnp.concatenate(pieces, axis=1).astype(BF16)


def _prep_in_call(w_in):
    rb = 128
    return pl.pallas_call(
        _prep_in_kernel,
        grid=(DEPTH, D_MODEL // rb),
        in_specs=[pl.BlockSpec((None, rb, N_IN), lambda l, j: (l, j, 0))],
        out_specs=pl.BlockSpec((None, rb, N_Z), lambda l, j: (l, j, 0)),
        out_shape=jax.ShapeDtypeStruct((DEPTH, D_MODEL, N_Z), BF16),
        compiler_params=_cparams(("arbitrary", "arbitrary")),
        name="prep_in",
    )(w_in)


def _prep_mla_kernel(uq_ref, ukv_ref, oq_ref, ok_ref, ov_ref):
    uq = uq_ref[...]
    dq = MLA_NOPE + MLA_ROPE
    zq = jnp.zeros((MLA_Q_RANK, HEAD_PAD - dq), F32)
    oq_ref[...] = jnp.concatenate(
        [p for h in range(MLA_HEADS) for p in (uq[:, h * dq:(h + 1) * dq], zq)], axis=1).astype(BF16)
    ukv = ukv_ref[...]
    lane = lax.broadcasted_iota(jnp.int32, ukv.shape, 1)
    ok_ref[...] = jnp.where((lane & (HEAD_PAD - 1)) < MLA_NOPE, ukv, 0.0).astype(BF16)
    ov_ref[...] = jnp.concatenate(
        [ukv[:, h * HEAD_PAD + MLA_NOPE:(h + 1) * HEAD_PAD] for h in range(MLA_HEADS)], axis=1).astype(BF16)


def _prep_mla_call(w_uq, w_ukv):
    lay = lambda n: pl.BlockSpec((None, n[0], n[1]), lambda l: (l, 0, 0))
    shapes = [(MLA_Q_RANK, MLA_QW), (MLA_KV_RANK, MLA_QW), (MLA_KV_RANK, MLA_W)]
    return pl.pallas_call(
        _prep_mla_kernel,
        grid=(DEPTH,),
        in_specs=[lay((MLA_Q_RANK, MLA_HEADS * (MLA_NOPE + MLA_ROPE))), lay((MLA_KV_RANK, MLA_QW))],
        out_specs=[lay(s) for s in shapes],
        out_shape=[jax.ShapeDtypeStruct((DEPTH,) + s, BF16) for s in shapes],
        compiler_params=_cparams(("arbitrary",)),
        name="prep_mla",
    )(w_uq, w_ukv)


def _prep_cast_kernel(a_ref, b_ref, c_ref, oa_ref, ob_ref, oc_ref):
    oa_ref[...] = a_ref[...].astype(BF16)
    ob_ref[...] = b_ref[...].astype(BF16)
    oc_ref[...] = c_ref[...].astype(BF16)


def _prep_cast_call(w_out, w_ffn_in, w_ffn_out):
    ws = (w_out, w_ffn_in, w_ffn_out)
    specs = [pl.BlockSpec((None, w.shape[1] // PREP_STEPS, w.shape[2]), lambda l, j: (l, j, 0)) for w in ws]
    return pl.pallas_call(
        _prep_cast_kernel,
        grid=(DEPTH, PREP_STEPS),
        in_specs=specs,
        out_specs=specs,
        out_shape=[jax.ShapeDtypeStruct(w.shape, BF16) for w in ws],
        compiler_params=_cparams(("arbitrary", "arbitrary")),
        name="prep_cast",
    )(*ws)


def _ada_kernel(c_ref, w_ref, b_ref, o_ref):
    s = _silu(c_ref[...]).astype(BF16)
    o_ref[...] = _dot(s, w_ref[...].astype(BF16)) + b_ref[pl.ds(pl.program_id(0), 1), :]


def _ada_call(cvec, w_ada, b_ada):
    tn = 1536
    return pl.pallas_call(
        _ada_kernel,
        grid=(DEPTH, N_ADA // tn),
        in_specs=[
            pl.BlockSpec((MOD_ROWS, D_MODEL), lambda l, j: (0, 0)),
            pl.BlockSpec((None, D_MODEL, tn), lambda l, j: (l, 0, j)),
            pl.BlockSpec((DEPTH, tn), lambda l, j: (0, j)),
        ],
        out_specs=pl.BlockSpec((None, MOD_ROWS, tn), lambda l, j: (l, 0, j)),
        out_shape=jax.ShapeDtypeStruct((DEPTH, MOD_ROWS, N_ADA), F32),
        compiler_params=_cparams(("arbitrary", "arbitrary")),
        name="ada_mod",
    )(cvec, w_ada, b_ada)


def _mod_row(mod_ref, dec, tiles_per_seq):
    if not dec:
        return mod_ref[0:1, :]
    return mod_ref[pl.ds(1 + pl.program_id(0) // tiles_per_seq, 1), :]


def _rope_block(x, c, sa, sb, half):
    return x * c + pltpu.roll(x, LANES - half, 1) * sa + pltpu.roll(x, half, 1) * sb


def _inproj_kernel(layer, rope, x_ref, mod_ref, w_in_ref, w_uq_ref, w_uk_ref, w_uv_ref, qn_ref, kvn_ref, gqn_ref,
                   gkn_ref, ones_ref, *rest):
    if rope:
        (mc_ref, msa_ref, msb_ref, gc_ref, gsa_ref, gsb_ref,
         zhf_ref, zhb_ref, qm_ref, km_ref, vm_ref, gq_ref, gk_ref, gv_ref) = rest
    else:
        rest = rest[-12:]
        (zhf_ref, zhb_ref, qm_ref, km_ref, vm_ref, gq_ref, gk_ref, gv_ref,
         ckv_ref, kpe_ref, gk32_ref, gv32_ref) = rest
    lrow = slice(layer, layer + 1)
    qk_base = LOG2E
    mod = _mod_row(mod_ref, rope, DEC_SEQ // TM_IN)
    sh1 = mod[:, 0:D_MODEL]
    sc1 = mod[:, D_MODEL:2 * D_MODEL]
    h = (x_ref[...] * (1.0 + sc1) + sh1).astype(BF16)
    z = _dot(h, w_in_ref[...])
    gw = HG_HEADS * HG_DK
    zhf_ref[...] = z[:, gw:3 * gw]
    zhb_ref[:, 0:gw] = z[:, 0:gw].astype(BF16)
    zhb_ref[:, gw:3 * gw] = z[:, 3 * gw:HG_COLS].astype(BF16)
    mq = z[:, OFF_MQ:OFF_MKV]
    mkv = z[:, OFF_MKV:OFF_GQ]
    gq = z[:, OFF_GQ:OFF_GK]
    gk = z[:, OFF_GK:OFF_GV]
    gv = z[:, OFF_GV:OFF_KPE_PAD]
    kpe_pad = z[:, OFF_KPE_PAD:OFF_KPE]
    ones = ones_ref[...]
    gqn = _twice(gqn_ref[lrow, :])
    gkn_g = _twice(gkn_ref[lrow, :])

    q = _dot(_rms_rows(mq, qn_ref[lrow, :]).astype(BF16), w_uq_ref[...])
    ckv = _rms_rows(mkv, kvn_ref[lrow, :])
    ckv_b = ckv.astype(BF16)
    kn = _dot(ckv_b, w_uk_ref[...])
    vm_ref[...] = _dot(ckv_b, w_uv_ref[...]).astype(BF16)
    if rope:
        mc, msa, msb = mc_ref[...], msa_ref[...], msb_ref[...]
        kpe_pad = _rope_block(kpe_pad, mc, msa, msb, MLA_ROPE // 4)
    for hd in range(MLA_HEADS):
        sl = slice(hd * HEAD_PAD, (hd + 1) * HEAD_PAD)
        qh = q[:, sl]
        if rope:
            qh = _rope_block(qh, mc, msa, msb, MLA_ROPE // 4)
        qm_ref[:, sl] = (qh * (MLA_SCALE * qk_base)).astype(BF16)
        km_ref[:, sl] = (kn[:, sl] + kpe_pad).astype(BF16)

    gkn = gk * lax.rsqrt(_group_mean(gk * gk, ones) + EPS) * gkn_g
    for p in range(GQA_W // LANES):
        sl = slice(p * LANES, (p + 1) * LANES)
        gqp = gq[:, sl]
        gqp = gqp * lax.rsqrt(_group_mean(gqp * gqp, ones) + EPS) * gqn
        if rope:
            gqp = _rope_block(gqp, gc_ref[...], gsa_ref[...], gsb_ref[...], GQA_HD // 4)
        gq_ref[:, sl] = (gqp * (GQA_SCALE * qk_base)).astype(BF16)
    if rope:
        gk_ref[...] = _rope_block(gkn, gc_ref[...], gsa_ref[...], gsb_ref[...], GQA_HD // 4).astype(BF16)
    else:
        gk_ref[...] = gkn.astype(BF16)
        kpe = z[:, OFF_KPE:N_Z]
        for b in range(TM_IN // SEQ):
            rows = slice(b * SEQ, (b + 1) * SEQ)
            ckv_ref[b] = ckv[rows]
            kpe_ref[b] = kpe[rows]
            gk32_ref[b] = gkn[rows]
            gv32_ref[b] = gv[rows]
    gv_ref[...] = gv.astype(BF16)


def _inproj_call(x, mod, w, prm, cst, layer, rope, prev_states=None):
    t = x.shape[0]
    nt = t // TM_IN
    row = lambda i: (i, 0)
    in_specs = [
        pl.BlockSpec((TM_IN, D_MODEL), row),
        _layer_spec((MOD_ROWS, N_ADA), layer),
        _layer_spec((D_MODEL, N_Z), layer),
        _layer_spec((MLA_Q_RANK, MLA_QW), layer),
        _layer_spec((MLA_KV_RANK, MLA_QW), layer),
        _layer_spec((MLA_KV_RANK, MLA_W), layer),
        _const_spec((DEPTH, MLA_Q_RANK)),
        _const_spec((DEPTH, MLA_KV_RANK)),
        _const_spec((DEPTH, GQA_HD)),
        _const_spec((DEPTH, GQA_HD)),
        _const_spec((LANES, LANES)),
    ]
    args = [x, mod, w["w_in"], w["w_uq"], w["w_uk"], w["w_uv"], prm["mla_q_norm"], prm["mla_kv_norm"],
            prm["gqa_q_norm"], prm["gqa_k_norm"], cst["ones64"]]
    outs = [(2 * HG_HEADS * HG_DK, F32), (3 * HG_HEADS * HG_DK, BF16), (MLA_QW, BF16), (MLA_QW, BF16), (MLA_W, BF16),
            (GQA_W, BF16), (LANES, BF16), (LANES, BF16)]
    out_specs = [pl.BlockSpec((TM_IN, wd), row) for wd, _ in outs]
    out_shape = [jax.ShapeDtypeStruct((t, wd), dt) for wd, dt in outs]
    aliases = {}
    if rope:
        tab = lambda i: (i % (DEC_SEQ // TM_IN), 0)
        in_specs += [pl.BlockSpec((TM_IN, LANES), tab)] * 6
        args += [cst["mla_c"], cst["mla_sa"], cst["mla_sb"], cst["gqa_c"], cst["gqa_sa"], cst["gqa_sb"]]
    else:
        per = TM_IN // SEQ
        for wd in (MLA_KV_RANK, MLA_ROPE, LANES, LANES):
            out_specs.append(pl.BlockSpec((per, None, SEQ, wd), lambda i: (i, layer, 0, 0)))
            out_shape.append(jax.ShapeDtypeStruct((BATCH, DEPTH, SEQ, wd), F32))
        if prev_states is not None:
            for k, arr in enumerate(prev_states):
                aliases[len(args)] = len(outs) + k
                in_specs.append(_ANY)
                args.append(arr)
    return pl.pallas_call(
        functools.partial(_inproj_kernel, layer, rope),
        grid=(nt,),
        in_specs=in_specs,
        out_specs=out_specs,
        out_shape=out_shape,
        input_output_aliases=aliases,
        compiler_params=_cparams(("arbitrary",)),
        name="inproj_dec" if rope else "inproj_ctx",
    )(*args)


def _cachekv_kernel(ckv_ref, kpe_ref, w_uk_ref, w_uv_ref, place_ref, k_ref, v_ref):
    ckv_b = ckv_ref[...].astype(BF16)
    k_ref[...] = (_dot(ckv_b, w_uk_ref[...]) + _dot(kpe_ref[...].astype(BF16), place_ref[...])).astype(BF16)
    v_ref[...] = _dot(ckv_b, w_uv_ref[...]).astype(BF16)


def _cachekv_call(cache_ckv, cache_kpe, w, cst, layer):
    t = DEC_BATCH * PAST_LEN
    cache = lambda wd: pl.BlockSpec((None, None, PAST_LEN, wd), lambda b: (b, layer, 0, 0))
    return pl.pallas_call(
        _cachekv_kernel,
        grid=(DEC_BATCH,),
        in_specs=[cache(MLA_KV_RANK), cache(MLA_ROPE),
                  _layer_spec((MLA_KV_RANK, MLA_QW), layer), _layer_spec((MLA_KV_RANK, MLA_W), layer),
                  _const_spec((MLA_ROPE, MLA_QW))],
        out_specs=[pl.BlockSpec((PAST_LEN, MLA_QW), lambda b: (b, 0)),
                   pl.BlockSpec((PAST_LEN, MLA_W), lambda b: (b, 0))],
        out_shape=[jax.ShapeDtypeStruct((t, MLA_QW), BF16), jax.ShapeDtypeStruct((t, MLA_W), BF16)],
        compiler_params=_cparams(("arbitrary",)),
        name="cache_kv",
    )(cache_ckv, cache_kpe, w["w_uk"], w["w_uv"], cst["kpe_place"])


def _bcast_row(x, r, rows):
    return jnp.broadcast_to(x[r:r + 1, :], (rows, x.shape[1]))


def _cumsum_rows(tri, x):
    hi = x.astype(BF16)
    r1 = x - hi.astype(F32)
    mid = r1.astype(BF16)
    lo = (r1 - mid.astype(F32)).astype(BF16)
    return _dot(tri, hi) + _dot(tri, mid) + _dot(tri, lo)


def _hgrn_chunks(items, states, tri, lvl, stk, blk, lane_lo):
    C = HG_CHUNK
    zero_b = jnp.zeros((SUBLANES, LANES), F32)
    trow = lax.broadcasted_iota(jnp.int32, (SUBLANES, LANES), 0)
    for it in items:
        it["c"] = _cumsum_rows(tri[it["fwd"]], it["lf"])
    for it in items:
        it["q_a"] = jnp.where(lane_lo, it["q"], 0.0)
        it["q_b"] = jnp.where(lane_lo, 0.0, it["q"])
        it["ck"] = it["c"] - jnp.log(it["kk"])

    m = C // 2
    bit = m.bit_length() - 1
    while m >= SUBLANES:
        for it in items:
            fwd, c = it["fwd"], it["c"]
            qa_rows, qb_rows, k_rows = [], [], []
            for b in range(C // m):
                rows = slice(b * m, (b + 1) * m)
                anchor = (b // 2) * 2 * m + (m - 1 if fwd else m)
                r = _bcast_row(c, anchor, m)
                if (b % 2 == 1) == fwd:
                    e = jnp.exp(c[rows] - r)
                    qa_rows.append(it["q_a"][rows] * e)
                    qb_rows.append(it["q_b"][rows] * e)
                    k_rows += [zero_b] * (m // SUBLANES)
                else:
                    e = jnp.exp(r - c[rows])
                    qa_rows += [zero_b] * (m // SUBLANES)
                    qb_rows += [zero_b] * (m // SUBLANES)
                    k_rows.append(it["kk"][rows] * e)
            prod = _dot_nt(jnp.concatenate(qa_rows + qb_rows, axis=0).astype(BF16),
                           jnp.concatenate(k_rows, axis=0).astype(BF16))
            it["a"] = prod if "a" not in it else jnp.where(lvl[fwd] == bit, prod, it["a"])
        m //= 2
        bit -= 1

    pk = 2 * SUBLANES
    trow = lax.broadcasted_iota(jnp.int32, (pk, LANES), 0) & (SUBLANES - 1)
    zero_w = jnp.zeros((pk, LANES), BF16)
    for it in items:
        it["w"] = []
        it["q_bf"] = it["q"].astype(BF16)
    for j in range(C // pk):
        sl = slice(j * pk, (j + 1) * pk)
        for it in items:
            cs, cks, qs = it["c"][sl], it["ck"][sl], it["q_bf"][sl]
            w_s = []
            for s in range(SUBLANES):
                ref = jnp.concatenate([_bcast_row(cks, s, SUBLANES), _bcast_row(cks, SUBLANES + s, SUBLANES)], axis=0)
                e = jnp.exp(cs - ref).astype(BF16)
                keep = (trow >= s) if it["fwd"] else (trow <= s)
                w_s.append(jnp.where(keep, qs * e, zero_w))
            it["w"].append(jnp.concatenate(w_s, axis=1))
    for it in items:
        red = _dot(jnp.concatenate(it["w"], axis=0), stk)
        it["a"] = jnp.where(lvl[it["fwd"]] == 0,
                            jnp.concatenate([red[:, 0:LANES], red[:, LANES:2 * LANES]], axis=0), it["a"])
    for it in items:
        off = _dot(it["a"].astype(BF16), it["iv"].astype(BF16))
        it["o"] = jnp.where(lane_lo, off[0:C], off[C:2 * C])
        it["edge"] = _bcast_row(it["c"], C - 1 if it["fwd"] else 0, C)
        it["q_dec"] = (it["q"] * jnp.exp(it["c"])).astype(BF16)
        k_dec = (it["kk"] * jnp.exp(it["edge"] - it["c"])).astype(BF16)
        it["upd"] = _dot_tn(it["iv"].astype(BF16), k_dec) * blk

    states = list(states)
    outs = []
    for it in items:
        st = states[it["sid"]]
        outs.append(it["o"] + _dot_nt(it["q_dec"], st.astype(BF16)))
        states[it["sid"]] = st * jnp.exp(it["edge"][0:1]) + it["upd"]
    return outs, states


def _hgrn_kernel(layer, nchunks, nseq, has_state, zq_ref, zff_ref, zfb_ref, zi_ref, zg_ref, lb_ref, g_ref,
                 trif_ref, trib_ref, lvlf_ref, lvlb_ref, stk_ref, ones_ref, blk_ref, *rest):
    if has_state:
        s0_ref, o_ref, of_scr, ob_scr = rest
    else:
        o_ref, st_ref, of_scr, ob_scr = rest[-4:]
    C = HG_CHUNK
    unroll = HG_UNROLL_DEC if has_state else HG_UNROLL
    npair = HG_HEADS // 2
    lane_lo = lax.broadcasted_iota(jnp.int32, (1, LANES), 1) < HG_DK
    pair_lanes = [slice(hp * LANES, (hp + 1) * LANES) for hp in range(npair)]

    lb_all = [lb_ref[j] for j in range(DEPTH)]
    mx = functools.reduce(jnp.maximum, lb_all)
    ex = [jnp.exp(v - mx) for v in lb_all]
    den = functools.reduce(lambda a, b: a + b, ex)
    sm = [v / den for v in ex]
    lbs = functools.reduce(lambda a, b: a + b, sm[:layer + 1]) - sm[0]

    def gates(zf, lbd):
        f = lbd + (1.0 - lbd) * jax.nn.sigmoid(zf)
        return jnp.log(jnp.maximum(f, GATE_FLOOR)), jnp.maximum(1.0 - f, 0.0)

    def block_diag_t(sa, sb):
        za = jnp.zeros((HG_DK, HG_DV), F32)
        top = jnp.concatenate([sa, za], axis=1)
        bot = jnp.concatenate([za, sb], axis=1)
        return jnp.concatenate([top, bot], axis=0).T

    if has_state:
        st0 = tuple(block_diag_t(s0_ref[sq, d, 2 * hp], s0_ref[sq, d, 2 * hp + 1])
                    for sq in range(nseq) for hp in range(npair) for d in range(2))
    else:
        st0 = (jnp.zeros((LANES, LANES), F32),) * (2 * npair * nseq)

    def body(j, carry):
        tri = {True: trif_ref[...], False: trib_ref[...]}
        lvl = {True: lvlf_ref[...], False: lvlb_ref[...]}
        items = []
        for u in range(unroll):
            jf = j * unroll + u
            for sq in range(nseq):
                for hp, ln in enumerate(pair_lanes):
                    for fwd in (True, False):
                        chunk = sq * nchunks + (jf if fwd else nchunks - 1 - jf)
                        rows = pl.ds(pl.multiple_of(chunk * C, C), C)
                        d = 0 if fwd else 1
                        lf, kk = gates((zff_ref if fwd else zfb_ref)[rows, ln], lbs[d:d + 1, ln])
                        items.append({"fwd": fwd, "sid": (sq * npair + hp) * 2 + d, "rows": rows, "ln": ln,
                                      "lf": lf, "kk": kk, "iv": zi_ref[rows, ln],
                                      "q": _silu(zq_ref[rows, ln].astype(F32)) * (HG_DK ** -0.5)})
        outs, states = _hgrn_chunks(items, carry, tri, lvl, stk_ref[...], blk_ref[...], lane_lo)
        for it, o in zip(items, outs):
            (of_scr if it["fwd"] else ob_scr)[it["rows"], it["ln"]] = o
        return tuple(states)

    states = lax.fori_loop(0, nchunks // unroll, body, st0)

    g = _twice(g_ref[layer:layer + 1, :])
    for ln in pair_lanes:
        o = of_scr[:, ln] + ob_scr[:, ln]
        ms = _group_mean(o * o, ones_ref[...] * (1.0 / HG_DV))
        o_ref[:, ln] = (o * lax.rsqrt(ms + EPS) * g * _silu(zg_ref[:, ln].astype(F32))).astype(BF16)
    if not has_state:
        for sq in range(nseq):
            for hp in range(npair):
                for d in range(2):
                    s_kv = states[(sq * npair + hp) * 2 + d].T
                    st_ref[sq, d, 2 * hp] = s_kv[0:HG_DK, 0:HG_DV]
                    st_ref[sq, d, 2 * hp + 1] = s_kv[HG_DK:2 * HG_DK, HG_DV:2 * HG_DV]


def _hgrn_call(zhf, zhb, lb, g, cst, layer, nb, s, state_in=None, prev_state_out=None):
    has_state = state_in is not None
    nchunks = s // HG_CHUNK
    nseq = HG_SEQS_DEC if has_state else HG_SEQS_CTX
    part = lambda k: pl.BlockSpec((nseq * s, HG_W), lambda b: (b, k))
    st_spec = pl.BlockSpec((nseq, None, 2, HG_HEADS, HG_DK, HG_DV), lambda b: (b, layer, 0, 0, 0, 0))
    in_specs = [part(0), part(0), part(1), part(1), part(2),
                _const_spec((DEPTH, 2, HG_W)),
                _const_spec((DEPTH, HG_DV)),
                _const_spec((HG_CHUNK, HG_CHUNK)), _const_spec((HG_CHUNK, HG_CHUNK)),
                _const_spec((2 * HG_CHUNK, HG_CHUNK)), _const_spec((2 * HG_CHUNK, HG_CHUNK)),
                _const_spec((SUBLANES * LANES, 2 * LANES)),
                _const_spec((LANES, LANES)), _const_spec((LANES, LANES))]
    args = [zhb, zhf, zhf, zhb, zhb, lb, g, cst["tri_f"], cst["tri_b"], cst["lvl_f"], cst["lvl_b"], cst["stk"],
            cst["ones"], cst["blk"]]
    out_specs = [pl.BlockSpec((nseq * s, HG_W), lambda b: (b, 0))]
    out_shape = [jax.ShapeDtypeStruct((nb * s, HG_W), BF16)]
    aliases = {}
    if has_state:
        in_specs.append(st_spec)
        args.append(state_in)
    else:
        out_specs.append(st_spec)
        out_shape.append(jax.ShapeDtypeStruct((nb, DEPTH, 2, HG_HEADS, HG_DK, HG_DV), F32))
        if prev_state_out is not None:
            aliases[len(args)] = 1
            in_specs.append(_ANY)
            args.append(prev_state_out)
    return pl.pallas_call(
        functools.partial(_hgrn_kernel, layer, nchunks, nseq, has_state),
        grid=(nb // nseq,),
        in_specs=in_specs,
        out_specs=out_specs,
        out_shape=out_shape,
        input_output_aliases=aliases,
        scratch_shapes=[pltpu.VMEM((nseq * s, HG_W), F32), pltpu.VMEM((nseq * s, HG_W), F32)],
        compiler_params=_cparams(("arbitrary",)),
        name="hgrn_dec" if has_state else "hgrn_ctx",
    )(*args)


def _attend_t(jobs, ss):
    add = lambda a, b: a + b
    mxs = [functools.reduce(jnp.maximum, [jnp.max(s, axis=-1, keepdims=True) for s in sj]) for sj in ss]
    ebs = [[jnp.exp2(s - mx).astype(BF16) for s in sj] for sj, mx in zip(ss, mxs)]
    prods = [[functools.reduce(add, [_dot_nt(vt, e[rows]) for vt, e in zip(vts, ej)]) for rows, vts in parts]
             for ej, (_, _, parts) in zip(ebs, jobs)]
    return [[pr[0:MLA_V] * (1.0 / pr[MLA_V:MLA_V + 1]) for pr in pj] for pj in prods]


def _scores(jobs):
    return [[_dot_nt(q, k) for k in ks] for q, ks, _ in jobs]


def _attn_kernel(has_cache, qm_ref, gq_ref, km_ref, vm_ref, gk_ref, gv_ref, *rest):
    if has_cache:
        kc_ref, vc_ref, gkc_ref, gvc_ref, o_ref, vt_scr = rest
    else:
        o_ref, vt_scr = rest
    sq = qm_ref.shape[0]
    sk = vt_scr.shape[1]
    nblk = MLA_HEADS + GQA_KV_HEADS
    cols = [slice(0, PAST_LEN), slice(PAST_LEN, sk)] if has_cache else [slice(0, sk)]

    def fill():
        tr = lambda v: v.astype(F32).T.astype(BF16)
        srcs = [(vc_ref, gvc_ref), (vm_ref, gv_ref)] if has_cache else [(vm_ref, gv_ref)]
        for cl, (mla_ref, gqa_ref) in zip(cols, srcs):
            mla_t, gqa_t = tr(mla_ref[...]), tr(gqa_ref[...])
            tail = (lax.broadcasted_iota(jnp.int32, (VT_ROWS - MLA_V, cl.stop - cl.start), 0) == 0).astype(BF16)
            for b in range(nblk):
                src = mla_t[b * MLA_V:(b + 1) * MLA_V] if b < MLA_HEADS else \
                    gqa_t[(b - MLA_HEADS) * GQA_HD:(b - MLA_HEADS + 1) * GQA_HD]
                vt_scr[b * VT_ROWS:b * VT_ROWS + MLA_V, cl] = src
                vt_scr[b * VT_ROWS + MLA_V:(b + 1) * VT_ROWS, cl] = tail

    if has_cache:
        pl.when(pl.program_id(1) == 0)(fill)
    else:
        fill()

    lo = lax.broadcasted_iota(jnp.int32, (GQA_ROWS, LANES), 1) < GQA_HD
    zero = jnp.zeros((GQA_ROWS, LANES), BF16)
    if has_cache:
        seqs = [(slice(0, sq), slice(0, None), cols)]
    else:
        seqs = [(slice(r, r + SEQ), slice(r, r + SEQ), [slice(r, r + SEQ)]) for r in range(0, sq, SEQ)]
    jobs, first_job = [], []
    for qrows, krows, vcols in seqs:
        first_job.append(len(jobs))

        def vt(b, vcols=vcols):
            return [vt_scr[b * VT_ROWS:(b + 1) * VT_ROWS, cl] for cl in vcols]

        gks = [gkc_ref[...].astype(BF16), gk_ref[...]] if has_cache else [gk_ref[krows, :]]
        for hd in range(MLA_HEADS):
            sl = slice(hd * HEAD_PAD, (hd + 1) * HEAD_PAD)
            ks = [kc_ref[:, sl], km_ref[:, sl]] if has_cache else [km_ref[krows, sl]]
            jobs.append((qm_ref[qrows, sl], ks, [(slice(0, qrows.stop - qrows.start), vt(hd))]))
        halves = [(slice(0, GQA_ROWS), vt(MLA_HEADS)), (slice(GQA_ROWS, 2 * GQA_ROWS), vt(MLA_HEADS + 1))]
        for r0 in range(qrows.start, qrows.stop, GQA_ROWS):
            for p in range(GQA_W // LANES):
                qp = gq_ref[r0:r0 + GQA_ROWS, p * LANES:(p + 1) * LANES]
                jobs.append((jnp.concatenate([jnp.where(lo, qp, zero), jnp.where(lo, zero, qp)], axis=0), gks,
                             halves))
    group = ATT_GROUP_DEC if has_cache else ATT_GROUP_CTX
    groups = [jobs[j0:j0 + group] for j0 in range(0, len(jobs), group)]
    outs = []
    ss_next = _scores(groups[0])
    for g, grp in enumerate(groups):
        ss = ss_next
        if g + 1 < len(groups):
            ss_next = _scores(groups[g + 1])
        outs += _attend_t(grp, ss)
    for (qrows, _, _), j0 in zip(seqs, first_job):
        for p in range(MLA_HEADS // 2):
            pair = jnp.concatenate([outs[j0 + 2 * p][0], outs[j0 + 2 * p + 1][0]], axis=0)
            o_ref[qrows, p * LANES:(p + 1) * LANES] = pair.T.astype(BF16)
        for i, r0 in enumerate(range(qrows.start, qrows.stop, GQA_ROWS)):
            gres = [jnp.concatenate(outs[j0 + MLA_HEADS + 3 * i + p], axis=0).T for p in range(GQA_W // LANES)]
            swapped = pltpu.roll(gres[1], GQA_HD, 1)
            blocks = (jnp.where(lo, gres[0], swapped), jnp.where(lo, gres[2], gres[0]),
                      jnp.where(lo, swapped, gres[2]))
            for p, blkv in enumerate(blocks):
                o_ref[r0:r0 + GQA_ROWS, MLA_W + p * LANES:MLA_W + (p + 1) * LANES] = blkv.astype(BF16)


def _attn_call(qm, gq, km, vm, gk, gv, cache, layer):
    has_cache = cache is not None
    t = qm.shape[0]
    widths = (MLA_QW, GQA_W, MLA_QW, MLA_W, LANES, LANES)
    if has_cache:
        nq = DEC_SEQ // TQ_DEC
        grid = (DEC_BATCH, nq)
        qmap = lambda b, i: (b * nq + i, 0)
        kmap = lambda b, i: (b, 0)
        once = pl.Buffered(1)
        in_specs = [pl.BlockSpec((TQ_DEC, widths[0]), qmap), pl.BlockSpec((TQ_DEC, widths[1]), qmap)]
        in_specs += [pl.BlockSpec((DEC_SEQ, wd), kmap, pipeline_mode=once) for wd in widths[2:]]
        in_specs += [pl.BlockSpec((PAST_LEN, wd), kmap, pipeline_mode=once) for wd in widths[2:4]]
        in_specs += [pl.BlockSpec((None, None, PAST_LEN, LANES), lambda b, i: (b, layer, 0, 0),
                                  pipeline_mode=once)] * 2
        args = [qm, gq, km, vm, gk, gv, *cache]
        out_spec = pl.BlockSpec((TQ_DEC, MLA_W + GQA_W), qmap)
        sk = PAST_LEN + DEC_SEQ
        sem = ("arbitrary", "arbitrary")
    else:
        rows = ATT_SEQS_CTX * SEQ
        grid = (t // rows,)
        bmap = lambda b: (b, 0)
        in_specs = [pl.BlockSpec((rows, wd), bmap) for wd in widths]
        args = [qm, gq, km, vm, gk, gv]
        out_spec = pl.BlockSpec((rows, MLA_W + GQA_W), bmap)
        sk = rows
        sem = ("arbitrary",)
    return pl.pallas_call(
        functools.partial(_attn_kernel, has_cache),
        grid=grid,
        in_specs=in_specs,
        out_specs=out_spec,
        out_shape=jax.ShapeDtypeStruct((t, MLA_W + GQA_W), BF16),
        scratch_shapes=[pltpu.VMEM(((MLA_HEADS + GQA_KV_HEADS) * VT_ROWS, sk), BF16)],
        compiler_params=_cparams(sem),
        name="attn_dec" if has_cache else "attn_ctx",
    )(*args)


def _ffn_kernel(layer, dec, hg_ref, at_ref, x_ref, mod_ref, w_out_ref, ln1g_ref, ln1b_ref, w_gu_ref, w_dn_ref,
                ln2g_ref, ln2b_ref, o_ref):
    lrow = slice(layer, layer + 1)
    mod = _mod_row(mod_ref, dec, DEC_SEQ // TM_FFN)
    g1 = mod[:, 2 * D_MODEL:3 * D_MODEL]
    sh2 = mod[:, 3 * D_MODEL:4 * D_MODEL]
    sc2 = mod[:, 4 * D_MODEL:5 * D_MODEL]
    g2 = mod[:, 5 * D_MODEL:6 * D_MODEL]
    y = _dot(hg_ref[...], w_out_ref[0:HG_W, :]) + _dot(at_ref[...], w_out_ref[HG_W:D_MODEL, :])
    x1 = _layernorm_rows(ALPHA * x_ref[...] + g1 * y, ln1g_ref[lrow, :], ln1b_ref[lrow, :])
    hf = (x1 * (1.0 + sc2) + sh2).astype(BF16)
    ch = D_FF // FF_CHUNKS
    f = None
    for c in range(FF_CHUNKS):
        gate = _dot(hf, w_gu_ref[:, c * ch:(c + 1) * ch])
        up = _dot(hf, w_gu_ref[:, D_FF + c * ch:D_FF + (c + 1) * ch])
        part = _dot((_silu(gate) * up).astype(BF16), w_dn_ref[c * ch:(c + 1) * ch, :])
        f = part if f is None else f + part
    o_ref[...] = _layernorm_rows(ALPHA * x1 + g2 * f, ln2g_ref[lrow, :], ln2b_ref[lrow, :])


def _ffn_call(hg, at, x, mod, w, prm, layer, dec):
    t = x.shape[0]
    row = lambda i: (i, 0)
    once = pl.Buffered(1)
    vec = pl.BlockSpec((DEPTH, D_MODEL), lambda i: (0, 0), pipeline_mode=once)
    return pl.pallas_call(
        functools.partial(_ffn_kernel, layer, dec),
        grid=(t // TM_FFN,),
        in_specs=[
            pl.BlockSpec((TM_FFN, HG_W), row),
            pl.BlockSpec((TM_FFN, MLA_W + GQA_W), row),
            pl.BlockSpec((TM_FFN, D_MODEL), row),
            _layer_spec((MOD_ROWS, N_ADA), layer, pipeline_mode=once),
            _layer_spec((D_MODEL, D_MODEL), layer, pipeline_mode=once),
            vec, vec,
            _layer_spec((D_MODEL, 2 * D_FF), layer, pipeline_mode=once),
            _layer_spec((D_FF, D_MODEL), layer, pipeline_mode=once),
            vec, vec,
        ],
        out_specs=pl.BlockSpec((TM_FFN, D_MODEL), row),
        out_shape=jax.ShapeDtypeStruct((t, D_MODEL), F32),
        compiler_params=_cparams(("arbitrary",)),
        name="ffn_dec" if dec else "ffn_ctx",
    )(hg, at, x, mod, w["w_out"], prm["ln1_g"], prm["ln1_b"], w["w_gu"], w["w_dn"], prm["ln2_g"], prm["ln2_b"])


def _rope_tables(width, rot_off, rot_dim):
    t = np.arange(DEC_SEQ)
    pos = (t // GRID_W, t % GRID_W)
    half = rot_dim // 4
    freqs = ROPE_THETA ** (-np.arange(half, dtype=np.float64) / half)
    c = np.ones((DEC_SEQ, LANES), np.float64)
    sa = np.zeros((DEC_SEQ, LANES), np.float64)
    sb = np.zeros((DEC_SEQ, LANES), np.float64)
    for blk in range(LANES // width):
        for axis in range(2):
            base = blk * width + rot_off + axis * 2 * half
            ang = pos[axis][:, None].astype(np.float64) * freqs[None, :]
            c[:, base:base + half] = np.cos(ang)
            c[:, base + half:base + 2 * half] = np.cos(ang)
            sa[:, base:base + half] = -np.sin(ang)
            sb[:, base + half:base + 2 * half] = np.sin(ang)
    return tuple(jnp.asarray(a, F32) for a in (c, sa, sb))


def _constants():
    C = HG_CHUNK
    t = np.arange(C)
    tri_f = (t[None, :] <= t[:, None]).astype(np.float32)
    tri_b = (t[None, :] >= t[:, None]).astype(np.float32)
    x = t[:, None] ^ t[None, :]
    top = np.where(x > 0, np.floor(np.log2(np.maximum(x, 1))).astype(np.int32), -1)
    same = (t[:, None] // SUBLANES) == (t[None, :] // SUBLANES)
    lvl_f = np.where(same, np.where(t[:, None] >= t[None, :], 0, -1),
                     np.where(t[:, None] > t[None, :], top, -1)).astype(np.int32)
    lvl_b = np.where(same, np.where(t[:, None] <= t[None, :], 0, -1),
                     np.where(t[:, None] < t[None, :], top, -1)).astype(np.int32)
    grp = np.arange(LANES) // HG_DK
    blk = (grp[:, None] == grp[None, :]).astype(np.float32)
    stk = np.zeros((SUBLANES, LANES, 2, C), np.float32)
    for s in range(SUBLANES):
        for hb in range(2):
            stk[s, hb * HG_DK:(hb + 1) * HG_DK, hb, s::SUBLANES] = 1.0
    stk = stk.reshape(SUBLANES * LANES, 2 * C)
    place = np.zeros((MLA_ROPE, MLA_QW), np.float32)
    for hd in range(MLA_HEADS):
        place[np.arange(MLA_ROPE), hd * HEAD_PAD + MLA_NOPE + np.arange(MLA_ROPE)] = 1.0
    mla_c, mla_sa, mla_sb = _rope_tables(HEAD_PAD, MLA_NOPE, MLA_ROPE)
    gqa_c, gqa_sa, gqa_sb = _rope_tables(GQA_HD, 0, GQA_HD)
    return {
        "tri_f": jnp.asarray(tri_f, BF16), "tri_b": jnp.asarray(tri_b, BF16),
        "lvl_f": jnp.asarray(np.concatenate([lvl_f, lvl_f], axis=0)),
        "lvl_b": jnp.asarray(np.concatenate([lvl_b, lvl_b], axis=0)),
        "stk": jnp.asarray(stk, BF16), "ones": jnp.asarray(blk, BF16), "ones64": jnp.asarray(blk / GQA_HD, BF16),
        "blk": jnp.asarray(blk, F32), "kpe_place": jnp.asarray(place, BF16),
        "mla_c": mla_c, "mla_sa": mla_sa, "mla_sb": mla_sb, "gqa_c": gqa_c, "gqa_sa": gqa_sa, "gqa_sb": gqa_sb,
    }


def kernel(x_prompt, x_sample, cache_mla_ckv, cache_mla_kpe, cache_gqa_k, cache_gqa_v, state_hgrn, c, c_ctx, w_ada,
           b_ada, w_in, hg_lb, hg_norm, mla_q_norm, mla_w_uq, mla_kv_norm, mla_w_ukv, gqa_q_norm, gqa_k_norm, w_out,
           ln1_g, ln1_b, w_ffn_in, w_ffn_out, ln2_g, ln2_b):
    cst = _constants()
    prm = {"mla_q_norm": mla_q_norm, "mla_kv_norm": mla_kv_norm, "gqa_q_norm": gqa_q_norm, "gqa_k_norm": gqa_k_norm,
           "ln1_g": ln1_g, "ln1_b": ln1_b, "ln2_g": ln2_g, "ln2_b": ln2_b}
    w = {"w_in": _prep_in_call(w_in)}
    w["w_uq"], w["w_uk"], w["w_uv"] = _prep_mla_call(mla_w_uq, mla_w_ukv)
    w["w_out"], w["w_gu"], w["w_dn"] = _prep_cast_call(w_out, w_ffn_in, w_ffn_out)

    cvec = jnp.concatenate([c_ctx[None, :], c, jnp.zeros((MOD_ROWS - 1 - DEC_BATCH, D_MODEL), F32)], axis=0)
    mod = _ada_call(cvec, w_ada, b_ada)
    cache_gk = cache_gqa_k.reshape(DEC_BATCH, DEPTH, PAST_LEN, LANES)
    cache_gv = cache_gqa_v.reshape(DEC_BATCH, DEPTH, PAST_LEN, LANES)

    x_ctx = x_prompt.reshape(BATCH * SEQ, D_MODEL)
    x_dec = x_sample.reshape(DEC_BATCH * DEC_SEQ, D_MODEL)
    states, hg_state = None, None
    for l in range(DEPTH):
        zhf, zhb, qm, km, vm, gq, gk, gv, *states = _inproj_call(x_ctx, mod, w, prm, cst, l, rope=False,
                                                          prev_states=states)
        hg_out, hg_state = _hgrn_call(zhf, zhb, hg_lb, hg_norm, cst, l, BATCH, SEQ, prev_state_out=hg_state)
        at = _attn_call(qm, gq, km, vm, gk, gv, None, l)
        x_ctx = _ffn_call(hg_out, at, x_ctx, mod, w, prm, l, dec=False)

        zhf, zhb, qm, km, vm, gq, gk, gv = _inproj_call(x_dec, mod, w, prm, cst, l, rope=True)
        hg_out = _hgrn_call(zhf, zhb, hg_lb, hg_norm, cst, l, DEC_BATCH, DEC_SEQ, state_in=state_hgrn)[0]
        kc, vc = _cachekv_call(cache_mla_ckv, cache_mla_kpe, w, cst, l)
        at = _attn_call(qm, gq, km, vm, gk, gv, (kc, vc, cache_gk, cache_gv), l)
        x_dec = _ffn_call(hg_out, at, x_dec, mod, w, prm, l, dec=True)

    ckv, kpe, gk32, gv32 = states
    kv_shape = (BATCH, DEPTH, SEQ, GQA_KV_HEADS, GQA_HD)
    return (x_ctx.reshape(BATCH, SEQ, D_MODEL), x_dec.reshape(DEC_BATCH, DEC_SEQ, D_MODEL),
            ckv, kpe, gk32.reshape(kv_shape), gv32.reshape(kv_shape), hg_state)
```
